```python
import jax, jax.numpy as jnp
from jax import lax
import numpy as np

D_MODEL = 1024
BATCH = 32
SEQ = 256
DEPTH = 2
DEC_BATCH = 2
DEC_SEQ = 4096
PAST_LEN = 256

GRID_W = 64
N_MIXERS = 2
HGRN_EXPAND = 128
HGRN_HEADS = D_MODEL // HGRN_EXPAND
HGRN_DK = HGRN_EXPAND
HGRN_DV = D_MODEL // HGRN_HEADS
CHUNK = 32
N_HGRN = (DEPTH + N_MIXERS - 1) // N_MIXERS
N_CONV = DEPTH // N_MIXERS
CONV_W = 3
D_FF = 2816
N_MOD = 6
EPS = 1e-6

kernel_name = "hybrid_hgrn2_shortconv_dit_step"


def rms_norm(x, w):
    xf = x.astype(jnp.float32)
    y = xf * lax.rsqrt(jnp.mean(xf * xf, axis=-1, keepdims=True) + EPS)
    return (y * w.astype(jnp.float32)).astype(x.dtype)


def modulate(x, w, shift, scale):
    return rms_norm(x, w) * (1.0 + scale) + shift


def dwconv3(u, w, b, axis):
    n = u.shape[axis]
    pad = [(0, 0)] * u.ndim
    pad[axis] = (1, 1)
    up = jnp.pad(u, pad)
    prev = lax.slice_in_dim(up, 0, n, axis=axis)
    nxt = lax.slice_in_dim(up, 2, n + 2, axis=axis)
    return prev * w[0] + u * w[1] + nxt * w[2] + b


def token_conv(u, w, b, grid_axis):
    if grid_axis is None:
        return dwconv3(u, w, b, axis=1)
    bsz, n, ch = u.shape
    rows = n // GRID_W
    ug = u.reshape(bsz, rows, GRID_W, ch)
    return dwconv3(ug, w, b, axis=grid_axis).reshape(bsz, n, ch)


def gla_chunk(q, k, v, log_f, s0):
    bsz, nh, n, dk = q.shape
    dv = v.shape[-1]
    nc = n // CHUNK
    q, k, log_f = (t.reshape(bsz, nh, nc, CHUNK, dk) for t in (q, k, log_f))
    v = v.reshape(bsz, nh, nc, CHUNK, dv)
    g_cum = jnp.cumsum(log_f, axis=3)
    g_last = g_cum[:, :, :, -1:, :]
    q_dec = q * jnp.exp(g_cum)
    k_intra = k * jnp.exp(-g_cum)
    k_state = k * jnp.exp(g_last - g_cum)
    lower = jnp.tril(jnp.ones((CHUNK, CHUNK), dtype=bool))
    scores = jnp.einsum("bhncd,bhnsd->bhncs", q_dec, k_intra)
    scores = jnp.where(lower, scores, 0.0)
    o_intra = jnp.einsum("bhncs,bhnse->bhnce", scores, v)
    chunk_kv = jnp.einsum("bhnsd,bhnse->nbhde", k_state, v)
    chunk_decay = jnp.moveaxis(jnp.exp(g_last[:, :, :, 0, :]), 2, 0)

    def step(s, inp):
        d, kv = inp
        return d[..., None] * s + kv, s

    s_final, s_prev = lax.scan(step, s0, (chunk_decay, chunk_kv))
    o_inter = jnp.einsum("bhncd,nbhde->bhnce", q_dec, s_prev)
    o = (o_intra + o_inter).reshape(bsz, nh, n, dv)
    return o, s_final


def hgrn2_mixer(h, w_in, lb_fwd, lb_bwd, gnorm_w, w_out, s0_fwd, s0_bwd):
    bsz, n, _ = h.shape
    hk = HGRN_HEADS * HGRN_DK
    hv = HGRN_HEADS * HGRN_DV
    proj = h @ w_in
    q, f_fwd, f_bwd, v, g = jnp.split(proj, [hk, 2 * hk, 3 * hk, 3 * hk + hv], axis=-1)

    def heads(t, d):
        return t.reshape(bsz, n, HGRN_HEADS, d).transpose(0, 2, 1, 3).astype(jnp.float32)

    qh = jax.nn.silu(heads(q, HGRN_DK))
    vh = heads(v, HGRN_DV)

    def gates(f_logit, lb):
        lb = lb.reshape(HGRN_HEADS, 1, HGRN_DK)
        f = lb + (1.0 - lb) * jax.nn.sigmoid(heads(f_logit, HGRN_DK))
        return 1.0 - f, jnp.log(f)

    k_f, lf_f = gates(f_fwd, lb_fwd)
    k_b, lf_b = gates(f_bwd, lb_bwd)
    o_f, s_f = gla_chunk(qh, k_f, vh, lf_f, s0_fwd.astype(jnp.float32))
    rev = lambda t: jnp.flip(t, axis=2)
    o_b, s_b = gla_chunk(rev(qh), rev(k_b), rev(vh), rev(lf_b), s0_bwd.astype(jnp.float32))
    o = o_f + rev(o_b)
    o = rms_norm(o, gnorm_w) * jax.nn.silu(heads(g, HGRN_DV))
    o = o.transpose(0, 2, 1, 3).reshape(bsz, n, hv).astype(h.dtype)
    return o @ w_out, s_f, s_b


def short_conv_mixer(h, w_in, conv_w, conv_b, w_out, grid_axis):
    gate_b, gate_c, u = jnp.split(h @ w_in, 3, axis=-1)
    z = token_conv(gate_c * u, conv_w, conv_b, grid_axis)
    return (gate_b * z) @ w_out


def conv_ffn(h, w_up, conv_w, conv_b, w_down, grid_axis):
    a, g = jnp.split(h @ w_up, 2, axis=-1)
    a = token_conv(a, conv_w, conv_b, grid_axis)
    return (jax.nn.silu(a) * g) @ w_down


def setup_inputs(seed: int = 0) -> dict:
    key = jax.random.key(seed)
    ks = jax.random.split(key, 22)
    hk = HGRN_HEADS * HGRN_DK
    hv = HGRN_HEADS * HGRN_DV

    def nrm(k, shape, scale):
        return jax.random.normal(k, shape, jnp.float32) * scale

    return {
        "x_prompt": nrm(ks[0], (BATCH, SEQ, D_MODEL), 1.0),
        "x_sample": nrm(ks[1], (DEC_BATCH, DEC_SEQ, D_MODEL), 1.0),
        "state_hgrn": nrm(ks[2], (DEC_BATCH, N_HGRN, 2, HGRN_HEADS, HGRN_DK, HGRN_DV), 0.5),
        "c": nrm(ks[3], (DEC_BATCH, D_MODEL), 1.0),
        "c_ctx": nrm(ks[4], (D_MODEL,), 1.0),
        "ada_w": nrm(ks[5], (DEPTH, D_MODEL, N_MOD * D_MODEL), 0.5 * D_MODEL ** -0.5),
        "ada_b": nrm(ks[6], (DEPTH, N_MOD * D_MODEL), 0.02),
        "norm_mix_w": 1.0 + nrm(ks[7], (DEPTH, D_MODEL), 0.05),
        "norm_ffn_w": 1.0 + nrm(ks[8], (DEPTH, D_MODEL), 0.05),
        "hgrn_w_in": nrm(ks[9], (N_HGRN, D_MODEL, 3 * hk + 2 * hv), D_MODEL ** -0.5),
        "hgrn_lower_bounds": nrm(ks[10], (2, N_HGRN + 1, hk), 0.1),
        "hgrn_gnorm_w": 1.0 + nrm(ks[11], (N_HGRN, HGRN_DV), 0.05),
        "hgrn_w_out": nrm(ks[12], (N_HGRN, hv, D_MODEL), hv ** -0.5),
        "sconv_w_in": nrm(ks[13], (N_CONV, D_MODEL, 3 * D_MODEL), D_MODEL ** -0.5),
        "sconv_conv_w": nrm(ks[14], (N_CONV, CONV_W, D_MODEL), CONV_W ** -0.5),
        "sconv_conv_b": nrm(ks[15], (N_CONV, D_MODEL), 0.01),
        "sconv_w_out": nrm(ks[16], (N_CONV, D_MODEL, D_MODEL), D_MODEL ** -0.5),
        "ffn_w_up": nrm(ks[17], (DEPTH, D_MODEL, 2 * D_FF), D_MODEL ** -0.5),
        "ffn_conv_w": nrm(ks[18], (DEPTH, CONV_W, D_FF), CONV_W ** -0.5),
        "ffn_conv_b": nrm(ks[19], (DEPTH, D_FF), 0.01),
        "ffn_w_down": nrm(ks[20], (DEPTH, D_FF, D_MODEL), D_FF ** -0.5),
        "final_norm_w": 1.0 + nrm(ks[21], (D_MODEL,), 0.05),
    }


def reference(x_prompt, x_sample, state_hgrn, c, c_ctx, ada_w, ada_b, norm_mix_w, norm_ffn_w,
              hgrn_w_in, hgrn_lower_bounds, hgrn_gnorm_w, hgrn_w_out,
              sconv_w_in, sconv_conv_w, sconv_conv_b, sconv_w_out,
              ffn_w_up, ffn_conv_w, ffn_conv_b, ffn_w_down, final_norm_w):
    lb_all = jnp.cumsum(jax.nn.softmax(hgrn_lower_bounds.astype(jnp.float32), axis=1), axis=1)
    zeros_state = jnp.zeros((x_prompt.shape[0], HGRN_HEADS, HGRN_DK, HGRN_DV), jnp.float32)
    xp, xs = x_prompt, x_sample
    ctx_states = []
    for l in range(DEPTH):
        mod_p = jnp.split((jax.nn.silu(c_ctx) @ ada_w[l] + ada_b[l])[None, None, :], N_MOD, axis=-1)
        mod_s = jnp.split((jax.nn.silu(c) @ ada_w[l] + ada_b[l])[:, None, :], N_MOD, axis=-1)
        hp = modulate(xp, norm_mix_w[l], mod_p[0], mod_p[1])
        hs = modulate(xs, norm_mix_w[l], mod_s[0], mod_s[1])
        j = l // N_MIXERS
        if l % N_MIXERS == 0:
            yp, sp_f, sp_b = hgrn2_mixer(hp, hgrn_w_in[j], lb_all[0, j], lb_all[1, j],
                                         hgrn_gnorm_w[j], hgrn_w_out[j], zeros_state, zeros_state)
            ctx_states.append(jnp.stack([sp_f, sp_b], axis=1))
            ys, _, _ = hgrn2_mixer(hs, hgrn_w_in[j], lb_all[0, j], lb_all[1, j],
                                   hgrn_gnorm_w[j], hgrn_w_out[j],
                                   state_hgrn[:, j, 0], state_hgrn[:, j, 1])
        else:
            yp = short_conv_mixer(hp, sconv_w_in[j], sconv_conv_w[j], sconv_conv_b[j],
                                  sconv_w_out[j], None)
            ys = short_conv_mixer(hs, sconv_w_in[j], sconv_conv_w[j], sconv_conv_b[j],
                                  sconv_w_out[j], 1)
        xp = xp + mod_p[2] * yp
        xs = xs + mod_s[2] * ys
        hp = modulate(xp, norm_ffn_w[l], mod_p[3], mod_p[4])
        hs = modulate(xs, norm_ffn_w[l], mod_s[3], mod_s[4])
        xp = xp + mod_p[5] * conv_ffn(hp, ffn_w_up[l], ffn_conv_w[l], ffn_conv_b[l], ffn_w_down[l], None)
        xs = xs + mod_s[5] * conv_ffn(hs, ffn_w_up[l], ffn_conv_w[l], ffn_conv_b[l], ffn_w_down[l], 2)
    y_prompt = rms_norm(xp, final_norm_w)
    y_sample = rms_norm(xs, final_norm_w)
    new_state_hgrn = jnp.stack(ctx_states, axis=1).astype(state_hgrn.dtype)
    return (y_prompt, y_sample, new_state_hgrn)
```

```python
import functools

import jax
import jax.numpy as jnp
from jax import lax
from jax.experimental import pallas as pl
from jax.experimental.pallas import tpu as pltpu

D_MODEL = 1024
N_HEADS = 8
D_HEAD = 128
D_FF = 2816
N_MOD = 6
GRID_W = 64
EPS = 1e-6

LANES = 128
SCAN_CHUNK = 64
TOKEN_TILE = 256
FF_CHUNK = 256
ADA_COLS = 1024
VMEM_LIMIT = 56 * 1024 * 1024

F32 = jnp.float32
BF16 = jnp.bfloat16


def _sigmoid(x):
    return 1.0 / (1.0 + jnp.exp(-x))


def _silu(x):
    return x * _sigmoid(x)


def _rms(x, w):
    return x * lax.rsqrt(jnp.mean(x * x, axis=-1, keepdims=True) + EPS) * w


def _modulate(x, w, shift, scale):
    return _rms(x, w) * (1.0 + scale) + shift


def _dot(a, b):
    return jnp.dot(a, b, preferred_element_type=F32)


def _dot_nt(a, b):
    return lax.dot_general(a, b, (((1,), (1,)), ((), ())), preferred_element_type=F32)


def _dot_tn(a, b):
    return lax.dot_general(a, b, (((0,), (0,)), ((), ())), preferred_element_type=F32)


def _params(n_grid):
    return pltpu.CompilerParams(
        dimension_semantics=("arbitrary",) * n_grid,
        vmem_limit_bytes=VMEM_LIMIT,
    )


def _const_spec(shape):
    nd = len(shape)
    return pl.BlockSpec(shape, lambda *_: (0,) * nd, pipeline_mode=pl.Buffered(1))


def _ada_kernel(c_ref, w_ref, b_ref, o_ref):
    cols = w_ref.shape[2]
    o_ref[0] = jnp.zeros(o_ref.shape[1:], F32)
    for r in range(c_ref.shape[0]):
        s = _silu(c_ref[r])
        parts = [
            jnp.sum(w_ref[0, :, j * LANES:(j + 1) * LANES] * s, axis=0, keepdims=True)
            for j in range(cols // LANES)
        ]
        o_ref[0, r:r + 1, :] = jnp.concatenate(parts, axis=1) + b_ref[0]


def _ada(cvecs, ada_w, ada_b):
    depth = ada_w.shape[0]
    n_out = ada_w.shape[2]
    cb = jnp.broadcast_to(cvecs[:, :, None], cvecs.shape + (LANES,))
    return pl.pallas_call(
        _ada_kernel,
        grid=(depth, n_out // ADA_COLS),
        in_specs=[
            pl.BlockSpec(cb.shape, lambda l, j: (0, 0, 0)),
            pl.BlockSpec((1, D_MODEL, ADA_COLS), lambda l, j: (l, 0, j)),
            pl.BlockSpec((1, 1, ADA_COLS), lambda l, j: (l, 0, j)),
        ],
        out_specs=pl.BlockSpec((1, 8, ADA_COLS), lambda l, j: (l, 0, j)),
        out_shape=jax.ShapeDtypeStruct((depth, 8, n_out), F32),
        compiler_params=_params(2),
        name="ada",
    )(cb, ada_w, ada_b.reshape(depth, 1, n_out))


def _inproj0_kernel(x_ref, mod_ref, nw_ref, w_ref, lbp_ref, q_ref, ff_ref, fb_ref, v_ref, g_ref):
    h = _modulate(x_ref[0], nw_ref[...], mod_ref[0, 0:1, :], mod_ref[0, 1:2, :]).astype(BF16)

    n_lb = lbp_ref.shape[0] // 2

    def lower_bound(d):
        rows = [lbp_ref[d * n_lb + i:d * n_lb + i + 1, :] for i in range(n_lb)]
        top = functools.reduce(jnp.maximum, rows)
        e = [jnp.exp(r - top) for r in rows]
        return e[0] / functools.reduce(jnp.add, e)

    lb = [lower_bound(0), lower_bound(1)]

    def proj(j):
        return _dot(h, w_ref[:, j * D_MODEL:(j + 1) * D_MODEL])

    def put(ref, val):
        for hh in range(N_HEADS):
            ref[0, hh] = val[:, hh * D_HEAD:(hh + 1) * D_HEAD].astype(ref.dtype)

    put(q_ref, _silu(proj(0)))
    put(ff_ref, lb[0] + (1.0 - lb[0]) * _sigmoid(proj(1)))
    put(fb_ref, lb[1] + (1.0 - lb[1]) * _sigmoid(proj(2)))
    put(v_ref, proj(3))
    put(g_ref, _silu(proj(4)))


def _inproj0(x, mod, nw, w_in, lbp, per_seq_mod):
    bsz, n, _ = x.shape
    t = TOKEN_TILE
    mod_map = (lambda b, i: (b, 0, 0)) if per_seq_mod else (lambda b, i: (0, 0, 0))
    head_spec = pl.BlockSpec((1, N_HEADS, t, D_HEAD), lambda b, i: (b, 0, i, 0))
    shp = (bsz, N_HEADS, n, D_HEAD)
    return pl.pallas_call(
        _inproj0_kernel,
        grid=(bsz, n // t),
        in_specs=[
            pl.BlockSpec((1, t, D_MODEL), lambda b, i: (b, i, 0)),
            pl.BlockSpec((1, N_MOD, D_MODEL), mod_map),
            _const_spec((1, D_MODEL)),
            _const_spec(w_in.shape),
            _const_spec(lbp.shape),
        ],
        out_specs=[head_spec] * 5,
        out_shape=[
            jax.ShapeDtypeStruct(shp, BF16),
            jax.ShapeDtypeStruct(shp, F32),
            jax.ShapeDtypeStruct(shp, F32),
            jax.ShapeDtypeStruct(shp, BF16),
            jax.ShapeDtypeStruct(shp, BF16),
        ],
        compiler_params=_params(2),
        name="inproj0",
    )(x, mod, nw, w_in, lbp)


def _cumsum_rows(tri, x):
    hi = x.astype(BF16)
    r1 = x - hi.astype(F32)
    mid = r1.astype(BF16)
    lo = (r1 - mid.astype(F32)).astype(BF16)
    return _dot(tri, hi) + _dot(tri, mid) + _dot(tri, lo)


def _gla_kernel(*refs, seq_len, zero_init, emit_state):
    q_ref, ff_ref, fb_ref, v_ref, g_ref = refs[:5]
    pos = 5
    s0_ref = None
    if not zero_init:
        s0_ref = refs[pos]
        pos += 1
    gnw_ref = refs[pos]
    o_ref = refs[pos + 1]
    pos += 2
    sf_ref = None
    if emit_state:
        sf_ref = refs[pos]
        pos += 1
    of_scr = refs[pos]

    c = SCAN_CHUNK
    n_chunks = seq_len // c
    row = lax.broadcasted_iota(jnp.int32, (c, c), 0)
    col = lax.broadcasted_iota(jnp.int32, (c, c), 1)
    lower = col <= row
    upper = col >= row
    tri_lo = lower.astype(BF16)
    tri_up = upper.astype(BF16)

    def chunk(n, st, f_ref, tri, mask, mid, last):
        r0 = pl.multiple_of(n * c, c)
        f = f_ref[0, 0, pl.ds(r0, c), :]
        k = 1.0 - f
        cum = _cumsum_rows(tri, jnp.log(f))
        cum_mid = cum[mid:mid + 1]
        cum_last = cum[last:last + 1]
        q = q_ref[0, 0, pl.ds(r0, c), :].astype(F32)
        v = v_ref[0, 0, pl.ds(r0, c), :]
        q_dec = (q * jnp.exp(cum)).astype(BF16)
        q_mid = (q * jnp.exp(cum - cum_mid)).astype(BF16)
        k_mid = (k * jnp.exp(cum_mid - cum)).astype(BF16)
        k_state = (k * jnp.exp(cum_last - cum)).astype(BF16)
        scores = jnp.where(mask, _dot_nt(q_mid, k_mid), 0.0).astype(BF16)
        o = _dot(scores, v) + _dot_nt(q_dec, st.astype(BF16))
        st_new = st * jnp.exp(cum_last) + _dot_tn(v, k_state)
        return r0, o, st_new

    if zero_init:
        st_f0 = jnp.zeros((D_HEAD, D_HEAD), F32)
        st_b0 = jnp.zeros((D_HEAD, D_HEAD), F32)
    else:
        st_f0 = s0_ref[0, 0, 0, 0].T
        st_b0 = s0_ref[0, 0, 1, 0].T

    def fwd_body(n, st):
        r0, o, st_new = chunk(n, st, ff_ref, tri_lo, lower, c // 2 - 1, c - 1)
        of_scr[pl.ds(r0, c), :] = o
        return st_new

    st_f = lax.fori_loop(0, n_chunks, fwd_body, st_f0)

    gnw = gnw_ref[...]

    def bwd_body(i, st):
        n = n_chunks - 1 - i
        r0, o, st_new = chunk(n, st, fb_ref, tri_up, upper, c // 2, 0)
        tot = of_scr[pl.ds(r0, c), :] + o
        gate = g_ref[0, 0, pl.ds(r0, c), :].astype(F32)
        o_ref[0, 0, pl.ds(r0, c), :] = (_rms(tot, gnw) * gate).astype(o_ref.dtype)
        return st_new

    st_b = lax.fori_loop(0, n_chunks, bwd_body, st_b0)

    if emit_state:
        sf_ref[0, 0, 0, 0] = st_f.T
        sf_ref[0, 0, 1, 0] = st_b.T


def _gla(q, ff, fb, v, g, s0, gnw, emit_state):
    bsz, _, n, _ = q.shape
    zero_init = s0 is None
    seq_spec = pl.BlockSpec((1, 1, n, D_HEAD), lambda b, h: (b, h, 0, 0))
    state_spec = pl.BlockSpec((1, 1, 2, 1, D_HEAD, D_HEAD), lambda b, h: (b, 0, 0, h, 0, 0))
    in_specs = [seq_spec] * 5
    args = [q, ff, fb, v, g]
    if not zero_init:
        in_specs.append(state_spec)
        args.append(s0)
    in_specs.append(_const_spec((1, D_HEAD)))
    args.append(gnw)
    out_specs = [seq_spec]
    out_shape = [jax.ShapeDtypeStruct(q.shape, BF16)]
    if emit_state:
        out_specs.append(state_spec)
        out_shape.append(jax.ShapeDtypeStruct((bsz, 1, 2, N_HEADS, D_HEAD, D_HEAD), F32))
    outs = pl.pallas_call(
        functools.partial(_gla_kernel, seq_len=n, zero_init=zero_init, emit_state=emit_state),
        grid=(bsz, N_HEADS),
        in_specs=in_specs,
        out_specs=out_specs,
        out_shape=out_shape,
        scratch_shapes=[pltpu.VMEM((n, D_HEAD), F32)],
        compiler_params=_params(2),
        name="gla",
    )(*args)
    return outs if emit_state else (outs[0], None)


def _conv_ffn(x1, mod_ref, nfw_ref, wup_ref, cw_ref, cb_ref, wdn_ref, period):
    t = x1.shape[0]
    h = _modulate(x1, nfw_ref[...], mod_ref[0, 3:4, :], mod_ref[0, 4:5, :]).astype(BF16)
    pos = lax.broadcasted_iota(jnp.int32, (t, 1), 0) % period
    first = pos == 0
    last = pos == period - 1
    acc = jnp.zeros((t, D_MODEL), F32)
    for j in range(D_FF // FF_CHUNK):
        lo = j * FF_CHUNK
        a = _dot(h, wup_ref[:, lo:lo + FF_CHUNK])
        g = _dot(h, wup_ref[:, D_FF + lo:D_FF + lo + FF_CHUNK])
        prev = jnp.where(first, 0.0, pltpu.roll(a, 1, 0))
        nxt = jnp.where(last, 0.0, pltpu.roll(a, t - 1, 0))
        ac = (prev * cw_ref[0:1, lo:lo + FF_CHUNK] + a * cw_ref[1:2, lo:lo + FF_CHUNK]
              + nxt * cw_ref[2:3, lo:lo + FF_CHUNK] + cb_ref[:, lo:lo + FF_CHUNK])
        s = (_silu(ac) * g).astype(BF16)
        acc = acc + _dot(s, wdn_ref[lo:lo + FF_CHUNK, :])
    return x1 + mod_ref[0, 5:6, :] * acc


def _ffn_specs():
    return [
        _const_spec((1, D_MODEL)),
        _const_spec((D_MODEL, 2 * D_FF)),
        _const_spec((3, D_FF)),
        _const_spec((1, D_FF)),
        _const_spec((D_FF, D_MODEL)),
    ]


def _post0_kernel(x_ref, o_ref, mod_ref, wo_ref, nfw_ref, wup_ref, cw_ref, cb_ref, wdn_ref,
                  out_ref, *, period):
    o = jnp.concatenate([o_ref[0, hh] for hh in range(N_HEADS)], axis=1)
    x1 = x_ref[0] + mod_ref[0, 2:3, :] * _dot(o, wo_ref[...])
    out_ref[0] = _conv_ffn(x1, mod_ref, nfw_ref, wup_ref, cw_ref, cb_ref, wdn_ref, period)


def _post0(x, o, mod, wo, nfw, wup, cw, cb, wdn, per_seq_mod, period):
    bsz, n, _ = x.shape
    t = TOKEN_TILE
    mod_map = (lambda b, i: (b, 0, 0)) if per_seq_mod else (lambda b, i: (0, 0, 0))
    tok_spec = pl.BlockSpec((1, t, D_MODEL), lambda b, i: (b, i, 0))
    return pl.pallas_call(
        functools.partial(_post0_kernel, period=period),
        grid=(bsz, n // t),
        in_specs=[
            tok_spec,
            pl.BlockSpec((1, N_HEADS, t, D_HEAD), lambda b, i: (b, 0, i, 0)),
            pl.BlockSpec((1, N_MOD, D_MODEL), mod_map),
            _const_spec((D_MODEL, D_MODEL)),
        ] + _ffn_specs(),
        out_specs=tok_spec,
        out_shape=jax.ShapeDtypeStruct(x.shape, F32),
        compiler_params=_params(2),
        name="post0",
    )(x, o, mod, wo, nfw, wup, cw, cb, wdn)


def _inproj1_kernel(x_ref, mod_ref, nw_ref, w_ref, gb_ref, cu_ref):
    h = _modulate(x_ref[0], nw_ref[...], mod_ref[0, 0:1, :], mod_ref[0, 1:2, :]).astype(BF16)
    gb_ref[0] = _dot(h, w_ref[:, 0:D_MODEL]).astype(gb_ref.dtype)
    gate_c = _dot(h, w_ref[:, D_MODEL:2 * D_MODEL])
    u = _dot(h, w_ref[:, 2 * D_MODEL:3 * D_MODEL])
    cu_ref[0] = (gate_c * u).astype(cu_ref.dtype)


def _inproj1(x, mod, nw, w_in, per_seq_mod):
    bsz, n, _ = x.shape
    t = TOKEN_TILE
    mod_map = (lambda b, i: (b, 0, 0)) if per_seq_mod else (lambda b, i: (0, 0, 0))
    tok_spec = pl.BlockSpec((1, t, D_MODEL), lambda b, i: (b, i, 0))
    return pl.pallas_call(
        _inproj1_kernel,
        grid=(bsz, n // t),
        in_specs=[
            tok_spec,
            pl.BlockSpec((1, N_MOD, D_MODEL), mod_map),
            _const_spec((1, D_MODEL)),
            _const_spec(w_in.shape),
        ],
        out_specs=[tok_spec, tok_spec],
        out_shape=[jax.ShapeDtypeStruct(x.shape, BF16), jax.ShapeDtypeStruct(x.shape, BF16)],
        compiler_params=_params(2),
        name="inproj1",
    )(x, mod, nw, w_in)


def _post1_kernel(*refs, grid_rows, period):
    if grid_rows:
        (x_ref, gb_ref, cu_ref, cup_ref, cun_ref, mod_ref, scw_ref, scb_ref, wo_ref,
         nfw_ref, wup_ref, cw_ref, cb_ref, wdn_ref, fnw_ref, out_ref) = refs
    else:
        (x_ref, gb_ref, cu_ref, mod_ref, scw_ref, scb_ref, wo_ref,
         nfw_ref, wup_ref, cw_ref, cb_ref, wdn_ref, fnw_ref, out_ref) = refs
    cu = cu_ref[0].astype(F32)
    t = cu.shape[0]
    if grid_rows:
        i = pl.program_id(1)
        halo_p = jnp.where(i == 0, 0.0, cup_ref[0].astype(F32))
        halo_n = jnp.where(i == pl.num_programs(1) - 1, 0.0, cun_ref[0].astype(F32))
        prev = jnp.concatenate([halo_p, cu[:t - GRID_W]], axis=0)
        nxt = jnp.concatenate([cu[GRID_W:], halo_n], axis=0)
    else:
        pos = lax.broadcasted_iota(jnp.int32, (t, 1), 0) % period
        prev = jnp.where(pos == 0, 0.0, pltpu.roll(cu, 1, 0))
        nxt = jnp.where(pos == period - 1, 0.0, pltpu.roll(cu, t - 1, 0))
    z = prev * scw_ref[0:1, :] + cu * scw_ref[1:2, :] + nxt * scw_ref[2:3, :] + scb_ref[...]
    m = (gb_ref[0].astype(F32) * z).astype(BF16)
    x1 = x_ref[0] + mod_ref[0, 2:3, :] * _dot(m, wo_ref[...])
    x2 = _conv_ffn(x1, mod_ref, nfw_ref, wup_ref, cw_ref, cb_ref, wdn_ref, period)
    out_ref[0] = _rms(x2, fnw_ref[...])


def _post1(x, gb, cu, mod, scw, scb, wo, nfw, wup, cw, cb, wdn, fnw, per_seq_mod, grid_rows, period):
    bsz, n, _ = x.shape
    t = TOKEN_TILE
    mod_map = (lambda b, i: (b, 0, 0)) if per_seq_mod else (lambda b, i: (0, 0, 0))
    tok_spec = pl.BlockSpec((1, t, D_MODEL), lambda b, i: (b, i, 0))
    in_specs = [tok_spec, tok_spec, tok_spec]
    args = [x, gb, cu]
    if grid_rows:
        r = t // GRID_W
        n_rows = n // GRID_W
        in_specs += [
            pl.BlockSpec((1, GRID_W, D_MODEL), lambda b, i: (b, jnp.maximum(i * r - 1, 0), 0)),
            pl.BlockSpec((1, GRID_W, D_MODEL), lambda b, i: (b, jnp.minimum((i + 1) * r, n_rows - 1), 0)),
        ]
        args += [cu, cu]
    in_specs += [
        pl.BlockSpec((1, N_MOD, D_MODEL), mod_map),
        _const_spec((3, D_MODEL)),
        _const_spec((1, D_MODEL)),
        _const_spec((D_MODEL, D_MODEL)),
    ] + _ffn_specs() + [_const_spec((1, D_MODEL))]
    args += [mod, scw, scb, wo, nfw, wup, cw, cb, wdn, fnw]
    return pl.pallas_call(
        functools.partial(_post1_kernel, grid_rows=grid_rows, period=period),
        grid=(bsz, n // t),
        in_specs=in_specs,
        out_specs=tok_spec,
        out_shape=jax.ShapeDtypeStruct(x.shape, F32),
        compiler_params=_params(2),
        name="post1",
    )(*args)


def kernel(x_prompt, x_sample, state_hgrn, c, c_ctx, ada_w, ada_b, norm_mix_w, norm_ffn_w,
           hgrn_w_in, hgrn_lower_bounds, hgrn_gnorm_w, hgrn_w_out,
           sconv_w_in, sconv_conv_w, sconv_conv_b, sconv_w_out,
           ffn_w_up, ffn_conv_w, ffn_conv_b, ffn_w_down, final_norm_w):
    seq_p = x_prompt.shape[1]

    cvecs = jnp.concatenate([c_ctx[None, :], c], axis=0)
    mod = _ada(cvecs, ada_w, ada_b)
    mod = mod.reshape(mod.shape[0], 8, N_MOD, D_MODEL)
    mod_p = mod[:, 0:1]
    mod_s = mod[:, 1:1 + c.shape[0]]

    row = lambda a: a.reshape(1, -1)
    w_in0 = hgrn_w_in[0].astype(BF16)
    w_out0 = hgrn_w_out[0].astype(BF16)
    w_in1 = sconv_w_in[0].astype(BF16)
    w_out1 = sconv_w_out[0].astype(BF16)
    w_up = ffn_w_up.astype(BF16)
    w_dn = ffn_w_down.astype(BF16)
    gnw = row(hgrn_gnorm_w[0])
    lbp = hgrn_lower_bounds.reshape(-1, hgrn_lower_bounds.shape[-1])

    def layers(x, mods, s0, per_seq_mod, ffn_period, grid_rows, emit_state):
        q, ff, fb, v, g = _inproj0(x, mods[0], row(norm_mix_w[0]), w_in0, lbp, per_seq_mod)
        o, s_fin = _gla(q, ff, fb, v, g, s0, gnw, emit_state)
        x = _post0(x, o, mods[0], w_out0, row(norm_ffn_w[0]), w_up[0], ffn_conv_w[0],
                   row(ffn_conv_b[0]), w_dn[0], per_seq_mod, ffn_period)
        gb, cu = _inproj1(x, mods[1], row(norm_mix_w[1]), w_in1, per_seq_mod)
        y = _post1(x, gb, cu, mods[1], sconv_conv_w[0], row(sconv_conv_b[0]), w_out1,
                   row(norm_ffn_w[1]), w_up[1], ffn_conv_w[1], row(ffn_conv_b[1]), w_dn[1],
                   row(final_norm_w), per_seq_mod, grid_rows, ffn_period)
        return y, s_fin

    y_prompt, new_state = layers(x_prompt, mod_p, None, False, seq_p, False, True)
    y_sample, _ = layers(x_sample, mod_s, state_hgrn, True, GRID_W, True, False)
    return (y_prompt, y_sample, new_state.astype(state_hgrn.dtype))
```

```python
import functools

import jax
import jax.numpy as jnp
from jax import lax
from jax.experimental import pallas as pl
from jax.experimental.pallas import tpu as pltpu

D_MODEL = 1024
N_HEADS = 8
D_HEAD = 128
D_FF = 2816
N_MOD = 6
GRID_W = 64
EPS = 1e-6

LANES = 128
SCAN_CHUNK = 64
SCAN_LANES = 16
TOKEN_TILE = 256
FF_CHUNK = 256
ADA_COLS = 1024
VMEM_LIMIT = 56 * 1024 * 1024

F32 = jnp.float32
BF16 = jnp.bfloat16


def _sigmoid(x):
    return 1.0 / (1.0 + jnp.exp(-x))


def _silu(x):
    return x * _sigmoid(x)


def _rms(x, w):
    return x * lax.rsqrt(jnp.mean(x * x, axis=-1, keepdims=True) + EPS) * w


def _modulate(x, w, shift, scale):
    return _rms(x, w) * (1.0 + scale) + shift


def _dot(a, b):
    return jnp.dot(a, b, preferred_element_type=F32)


def _dot_nt(a, b):
    return lax.dot_general(a, b, (((1,), (1,)), ((), ())), preferred_element_type=F32)


def _dot_tn(a, b):
    return lax.dot_general(a, b, (((0,), (0,)), ((), ())), preferred_element_type=F32)


def _params(n_grid):
    return pltpu.CompilerParams(
        dimension_semantics=("arbitrary",) * n_grid,
        vmem_limit_bytes=VMEM_LIMIT,
    )


def _const_spec(shape):
    nd = len(shape)
    return pl.BlockSpec(shape, lambda *_: (0,) * nd, pipeline_mode=pl.Buffered(1))


def _ada_kernel(c_ref, w_ref, b_ref, o_ref):
    cols = w_ref.shape[2]
    o_ref[0] = jnp.zeros(o_ref.shape[1:], F32)
    for r in range(c_ref.shape[0]):
        s = _silu(c_ref[r])
        parts = [
            jnp.sum(w_ref[0, :, j * LANES:(j + 1) * LANES] * s, axis=0, keepdims=True)
            for j in range(cols // LANES)
        ]
        o_ref[0, r:r + 1, :] = jnp.concatenate(parts, axis=1) + b_ref[0]


def _ada(cvecs, ada_w, ada_b):
    depth = ada_w.shape[0]
    n_out = ada_w.shape[2]
    cb = jnp.broadcast_to(cvecs[:, :, None], cvecs.shape + (LANES,))
    return pl.pallas_call(
        _ada_kernel,
        grid=(depth, n_out // ADA_COLS),
        in_specs=[
            pl.BlockSpec(cb.shape, lambda l, j: (0, 0, 0)),
            pl.BlockSpec((1, D_MODEL, ADA_COLS), lambda l, j: (l, 0, j)),
            pl.BlockSpec((1, 1, ADA_COLS), lambda l, j: (l, 0, j)),
        ],
        out_specs=pl.BlockSpec((1, 8, ADA_COLS), lambda l, j: (l, 0, j)),
        out_shape=jax.ShapeDtypeStruct((depth, 8, n_out), F32),
        compiler_params=_params(2),
        name="ada",
    )(cb, ada_w, ada_b.reshape(depth, 1, n_out))


def _inproj0_kernel(x_ref, mod_ref, nw_ref, w_ref, lbp_ref, q_ref, ff_ref, fb_ref, v_ref, g_ref):
    h = _modulate(x_ref[0], nw_ref[...], mod_ref[0, 0:1, :], mod_ref[0, 1:2, :]).astype(BF16)

    n_lb = lbp_ref.shape[0] // 2

    def lower_bound(d):
        rows = [lbp_ref[d * n_lb + i:d * n_lb + i + 1, :] for i in range(n_lb)]
        top = functools.reduce(jnp.maximum, rows)
        e = [jnp.exp(r - top) for r in rows]
        return e[0] / functools.reduce(jnp.add, e)

    lb = [lower_bound(0), lower_bound(1)]

    def proj(j):
        return _dot(h, w_ref[:, j * D_MODEL:(j + 1) * D_MODEL])

    def put(ref, val):
        for hh in range(N_HEADS):
            ref[0, hh] = val[:, hh * D_HEAD:(hh + 1) * D_HEAD].astype(ref.dtype)

    put(q_ref, _silu(proj(0)))
    put(ff_ref, lb[0] + (1.0 - lb[0]) * _sigmoid(proj(1)))
    put(fb_ref, lb[1] + (1.0 - lb[1]) * _sigmoid(proj(2)))
    put(v_ref, proj(3))
    put(g_ref, _silu(proj(4)))


def _inproj0(x, mod, nw, w_in, lbp, per_seq_mod):
    bsz, n, _ = x.shape
    t = TOKEN_TILE
    mod_map = (lambda b, i: (b, 0, 0)) if per_seq_mod else (lambda b, i: (0, 0, 0))
    head_spec = pl.BlockSpec((1, N_HEADS, t, D_HEAD), lambda b, i: (b, 0, i, 0))
    shp = (bsz, N_HEADS, n, D_HEAD)
    return pl.pallas_call(
        _inproj0_kernel,
        grid=(bsz, n // t),
        in_specs=[
            pl.BlockSpec((1, t, D_MODEL), lambda b, i: (b, i, 0)),
            pl.BlockSpec((1, N_MOD, D_MODEL), mod_map),
            _const_spec((1, D_MODEL)),
            _const_spec(w_in.shape),
            _const_spec(lbp.shape),
        ],
        out_specs=[head_spec] * 5,
        out_shape=[
            jax.ShapeDtypeStruct(shp, BF16),
            jax.ShapeDtypeStruct(shp, F32),
            jax.ShapeDtypeStruct(shp, F32),
            jax.ShapeDtypeStruct(shp, BF16),
            jax.ShapeDtypeStruct(shp, BF16),
        ],
        compiler_params=_params(2),
        name="inproj0",
    )(x, mod, nw, w_in, lbp)


def _cumsum_rows(tri, x):
    hi = x.astype(BF16)
    lo = (x - hi.astype(F32)).astype(BF16)
    s = _dot(tri, jnp.concatenate([hi, lo], axis=1))
    return s[:, :D_HEAD] + s[:, D_HEAD:]


def _gla_kernel(*refs, seq_len, heads, head_group, unroll, zero_init, emit_state):
    q_ref, ff_ref, fb_ref, v_ref, g_ref = refs[:5]
    pos = 5
    s0_ref = None
    if not zero_init:
        s0_ref = refs[pos]
        pos += 1
    gnw_ref = refs[pos]
    o_ref = refs[pos + 1]
    pos += 2
    sf_ref = None
    if emit_state:
        sf_ref = refs[pos]
        pos += 1
    of_scr, ob_scr = refs[pos], refs[pos + 1]

    c = SCAN_CHUNK
    n_chunks = seq_len // c
    row = lax.broadcasted_iota(jnp.int32, (c, c), 0)
    col = lax.broadcasted_iota(jnp.int32, (c, c), 1)
    lower = col <= row
    upper = col >= row
    tri_lo = lower.astype(BF16)
    tri_up = upper.astype(BF16)
    gnw = gnw_ref[...]

    directions = (
        (ff_ref, tri_lo, lower, c // 2 - 1, c - 1, of_scr),
        (fb_ref, tri_up, upper, c // 2, 0, ob_scr),
    )

    def trip(hs, i, states):
        lanes = []
        for slot, h in enumerate(hs):
            for d, (f_ref, tri, mask, mid, last, out_scr) in enumerate(directions):
                for u in range(unroll):
                    n = i * unroll + u
                    if d == 1:
                        n = n_chunks - 1 - n
                    r0 = n * c if isinstance(n, int) else pl.multiple_of(n * c, c)
                    lanes.append(dict(slot=slot, h=h, d=d, r0=r0))
        for ln in lanes:
            f_ref, tri = directions[ln["d"]][:2]
            ln["f"] = f_ref[0, ln["h"], pl.ds(ln["r0"], c), :]
            ln["cum"] = _cumsum_rows(tri, jnp.log(ln["f"]))
        for ln in lanes:
            _, _, _, mid, last, _ = directions[ln["d"]]
            cum = ln["cum"]
            cum_mid = cum[mid:mid + 1]
            cum_last = cum[last:last + 1]
            k = 1.0 - ln["f"]
            q = q_ref[0, ln["h"], pl.ds(ln["r0"], c), :].astype(F32)
            ln["v"] = v_ref[0, ln["h"], pl.ds(ln["r0"], c), :]
            ln["q_dec"] = (q * jnp.exp(cum)).astype(BF16)
            q_mid = (q * jnp.exp(cum - cum_mid)).astype(BF16)
            k_mid = (k * jnp.exp(cum_mid - cum)).astype(BF16)
            k_state = (k * jnp.exp(cum_last - cum)).astype(BF16)
            ln["scores"] = _dot_nt(q_mid, k_mid)
            ln["kv"] = _dot_tn(ln["v"], k_state)
            ln["decay"] = jnp.exp(cum_last)
        new_states = []
        for idx, st in enumerate(states):
            for ln in lanes[idx * unroll:(idx + 1) * unroll]:
                ln["st"] = st.astype(BF16)
                st = st * ln["decay"] + ln["kv"]
            new_states.append(st)
        for ln in lanes:
            mask, out_scr = directions[ln["d"]][2], directions[ln["d"]][5]
            p = jnp.where(mask, ln["scores"], 0.0).astype(BF16)
            out_scr[ln["slot"], pl.ds(ln["r0"], c), :] = _dot(p, ln["v"]) + _dot_nt(ln["q_dec"], ln["st"])
        return tuple(new_states)

    def head_group_body(hg, carry):
        hs = [hg * head_group + t for t in range(head_group)]
        states = []
        for h in hs:
            for d in range(2):
                if zero_init:
                    states.append(jnp.zeros((D_HEAD, D_HEAD), F32))
                else:
                    states.append(s0_ref[0, 0, d, h].T)
        states = tuple(states)
        n_trips = n_chunks // unroll
        if n_trips == 1:
            states = trip(hs, 0, states)
        else:
            states = lax.fori_loop(0, n_trips, functools.partial(trip, hs), states)

        eb = min(seq_len, TOKEN_TILE)
        for slot, h in enumerate(hs):
            def norm_body(j, carry2, slot=slot, h=h):
                r0 = pl.multiple_of(j * eb, eb)
                tot = of_scr[slot, pl.ds(r0, eb), :] + ob_scr[slot, pl.ds(r0, eb), :]
                gate = g_ref[0, h, pl.ds(r0, eb), :].astype(F32)
                o_ref[0, h, pl.ds(r0, eb), :] = (_rms(tot, gnw) * gate).astype(o_ref.dtype)
                return carry2

            if seq_len == eb:
                norm_body(0, 0)
            else:
                lax.fori_loop(0, seq_len // eb, norm_body, 0)
            if emit_state:
                sf_ref[0, 0, 0, h] = states[2 * slot].T
                sf_ref[0, 0, 1, h] = states[2 * slot + 1].T
        return carry

    if heads == head_group:
        head_group_body(0, 0)
    else:
        lax.fori_loop(0, heads // head_group, head_group_body, 0)


def _gla(q, ff, fb, v, g, s0, gnw, emit_state, heads, head_group, unroll):
    bsz, _, n, _ = q.shape
    zero_init = s0 is None
    seq_spec = pl.BlockSpec((1, heads, n, D_HEAD), lambda b, h: (b, h, 0, 0))
    state_spec = pl.BlockSpec((1, 1, 2, heads, D_HEAD, D_HEAD), lambda b, h: (b, 0, 0, h, 0, 0))
    in_specs = [seq_spec] * 5
    args = [q, ff, fb, v, g]
    if not zero_init:
        in_specs.append(state_spec)
        args.append(s0)
    in_specs.append(_const_spec((1, D_HEAD)))
    args.append(gnw)
    out_specs = [seq_spec]
    out_shape = [jax.ShapeDtypeStruct(q.shape, BF16)]
    if emit_state:
        out_specs.append(state_spec)
        out_shape.append(jax.ShapeDtypeStruct((bsz, 1, 2, N_HEADS, D_HEAD, D_HEAD), F32))
    outs = pl.pallas_call(
        functools.partial(_gla_kernel, seq_len=n, heads=heads, head_group=head_group, unroll=unroll,
                          zero_init=zero_init, emit_state=emit_state),
        grid=(bsz, N_HEADS // heads),
        in_specs=in_specs,
        out_specs=out_specs,
        out_shape=out_shape,
        scratch_shapes=[pltpu.VMEM((head_group, n, D_HEAD), F32)] * 2,
        compiler_params=_params(2),
        name="gla",
    )(*args)
    return outs if emit_state else (outs[0], None)


def _conv_ffn(x1, mod_ref, nfw_ref, wup_ref, cw_ref, cb_ref, wdn_ref, period):
    t = x1.shape[0]
    h = _modulate(x1, nfw_ref[...], mod_ref[0, 3:4, :], mod_ref[0, 4:5, :]).astype(BF16)
    pos = lax.broadcasted_iota(jnp.int32, (t, 1), 0) % period
    first = pos == 0
    last = pos == period - 1
    acc = jnp.zeros((t, D_MODEL), F32)
    for j in range(D_FF // FF_CHUNK):
        lo = j * FF_CHUNK
        a = _dot(h, wup_ref[:, lo:lo + FF_CHUNK])
        g = _dot(h, wup_ref[:, D_FF + lo:D_FF + lo + FF_CHUNK])
        prev = jnp.where(first, 0.0, pltpu.roll(a, 1, 0))
        nxt = jnp.where(last, 0.0, pltpu.roll(a, t - 1, 0))
        ac = (prev * cw_ref[0:1, lo:lo + FF_CHUNK] + a * cw_ref[1:2, lo:lo + FF_CHUNK]
              + nxt * cw_ref[2:3, lo:lo + FF_CHUNK] + cb_ref[:, lo:lo + FF_CHUNK])
        s = (_silu(ac) * g).astype(BF16)
        acc = acc + _dot(s, wdn_ref[lo:lo + FF_CHUNK, :])
    return x1 + mod_ref[0, 5:6, :] * acc


def _ffn_specs():
    return [
        _const_spec((1, D_MODEL)),
        _const_spec((D_MODEL, 2 * D_FF)),
        _const_spec((3, D_FF)),
        _const_spec((1, D_FF)),
        _const_spec((D_FF, D_MODEL)),
    ]


def _post0_kernel(x_ref, o_ref, mod_ref, wo_ref, nfw_ref, wup_ref, cw_ref, cb_ref, wdn_ref,
                  out_ref, *, period):
    o = jnp.concatenate([o_ref[0, hh] for hh in range(N_HEADS)], axis=1)
    x1 = x_ref[0] + mod_ref[0, 2:3, :] * _dot(o, wo_ref[...])
    out_ref[0] = _conv_ffn(x1, mod_ref, nfw_ref, wup_ref, cw_ref, cb_ref, wdn_ref, period)


def _post0(x, o, mod, wo, nfw, wup, cw, cb, wdn, per_seq_mod, period):
    bsz, n, _ = x.shape
    t = TOKEN_TILE
    mod_map = (lambda b, i: (b, 0, 0)) if per_seq_mod else (lambda b, i: (0, 0, 0))
    tok_spec = pl.BlockSpec((1, t, D_MODEL), lambda b, i: (b, i, 0))
    return pl.pallas_call(
        functools.partial(_post0_kernel, period=period),
        grid=(bsz, n // t),
        in_specs=[
            tok_spec,
            pl.BlockSpec((1, N_HEADS, t, D_HEAD), lambda b, i: (b, 0, i, 0)),
            pl.BlockSpec((1, N_MOD, D_MODEL), mod_map),
            _const_spec((D_MODEL, D_MODEL)),
        ] + _ffn_specs(),
        out_specs=tok_spec,
        out_shape=jax.ShapeDtypeStruct(x.shape, F32),
        compiler_params=_params(2),
        name="post0",
    )(x, o, mod, wo, nfw, wup, cw, cb, wdn)


def _inproj1_kernel(x_ref, mod_ref, nw_ref, w_ref, gb_ref, cu_ref):
    h = _modulate(x_ref[0], nw_ref[...], mod_ref[0, 0:1, :], mod_ref[0, 1:2, :]).astype(BF16)
    gb_ref[0] = _dot(h, w_ref[:, 0:D_MODEL]).astype(gb_ref.dtype)
    gate_c = _dot(h, w_ref[:, D_MODEL:2 * D_MODEL])
    u = _dot(h, w_ref[:, 2 * D_MODEL:3 * D_MODEL])
    cu_ref[0] = (gate_c * u).astype(cu_ref.dtype)


def _inproj1(x, mod, nw, w_in, per_seq_mod):
    bsz, n, _ = x.shape
    t = TOKEN_TILE
    mod_map = (lambda b, i: (b, 0, 0)) if per_seq_mod else (lambda b, i: (0, 0, 0))
    tok_spec = pl.BlockSpec((1, t, D_MODEL), lambda b, i: (b, i, 0))
    return pl.pallas_call(
        _inproj1_kernel,
        grid=(bsz, n // t),
        in_specs=[
            tok_spec,
            pl.BlockSpec((1, N_MOD, D_MODEL), mod_map),
            _const_spec((1, D_MODEL)),
            _const_spec(w_in.shape),
        ],
        out_specs=[tok_spec, tok_spec],
        out_shape=[jax.ShapeDtypeStruct(x.shape, BF16), jax.ShapeDtypeStruct(x.shape, BF16)],
        compiler_params=_params(2),
        name="inproj1",
    )(x, mod, nw, w_in)


def _post1_kernel(*refs, grid_rows, period):
    if grid_rows:
        (x_ref, gb_ref, cu_ref, cup_ref, cun_ref, mod_ref, scw_ref, scb_ref, wo_ref,
         nfw_ref, wup_ref, cw_ref, cb_ref, wdn_ref, fnw_ref, out_ref) = refs
    else:
        (x_ref, gb_ref, cu_ref, mod_ref, scw_ref, scb_ref, wo_ref,
         nfw_ref, wup_ref, cw_ref, cb_ref, wdn_ref, fnw_ref, out_ref) = refs
    cu = cu_ref[0].astype(F32)
    t = cu.shape[0]
    if grid_rows:
        i = pl.program_id(1)
        halo_p = jnp.where(i == 0, 0.0, cup_ref[0].astype(F32))
        halo_n = jnp.where(i == pl.num_programs(1) - 1, 0.0, cun_ref[0].astype(F32))
        prev = jnp.concatenate([halo_p, cu[:t - GRID_W]], axis=0)
        nxt = jnp.concatenate([cu[GRID_W:], halo_n], axis=0)
    else:
        pos = lax.broadcasted_iota(jnp.int32, (t, 1), 0) % period
        prev = jnp.where(pos == 0, 0.0, pltpu.roll(cu, 1, 0))
        nxt = jnp.where(pos == period - 1, 0.0, pltpu.roll(cu, t - 1, 0))
    z = prev * scw_ref[0:1, :] + cu * scw_ref[1:2, :] + nxt * scw_ref[2:3, :] + scb_ref[...]
    m = (gb_ref[0].astype(F32) * z).astype(BF16)
    x1 = x_ref[0] + mod_ref[0, 2:3, :] * _dot(m, wo_ref[...])
    x2 = _conv_ffn(x1, mod_ref, nfw_ref, wup_ref, cw_ref, cb_ref, wdn_ref, period)
    out_ref[0] = _rms(x2, fnw_ref[...])


def _post1(x, gb, cu, mod, scw, scb, wo, nfw, wup, cw, cb, wdn, fnw, per_seq_mod, grid_rows, period):
    bsz, n, _ = x.shape
    t = TOKEN_TILE
    mod_map = (lambda b, i: (b, 0, 0)) if per_seq_mod else (lambda b, i: (0, 0, 0))
    tok_spec = pl.BlockSpec((1, t, D_MODEL), lambda b, i: (b, i, 0))
    in_specs = [tok_spec, tok_spec, tok_spec]
    args = [x, gb, cu]
    if grid_rows:
        r = t // GRID_W
        n_rows = n // GRID_W
        in_specs += [
            pl.BlockSpec((1, GRID_W, D_MODEL), lambda b, i: (b, jnp.maximum(i * r - 1, 0), 0)),
            pl.BlockSpec((1, GRID_W, D_MODEL), lambda b, i: (b, jnp.minimum((i + 1) * r, n_rows - 1), 0)),
        ]
        args += [cu, cu]
    in_specs += [
        pl.BlockSpec((1, N_MOD, D_MODEL), mod_map),
        _const_spec((3, D_MODEL)),
        _const_spec((1, D_MODEL)),
        _const_spec((D_MODEL, D_MODEL)),
    ] + _ffn_specs() + [_const_spec((1, D_MODEL))]
    args += [mod, scw, scb, wo, nfw, wup, cw, cb, wdn, fnw]
    return pl.pallas_call(
        functools.partial(_post1_kernel, grid_rows=grid_rows, period=period),
        grid=(bsz, n // t),
        in_specs=in_specs,
        out_specs=tok_spec,
        out_shape=jax.ShapeDtypeStruct(x.shape, F32),
        compiler_params=_params(2),
        name="post1",
    )(*args)


def kernel(x_prompt, x_sample, state_hgrn, c, c_ctx, ada_w, ada_b, norm_mix_w, norm_ffn_w,
           hgrn_w_in, hgrn_lower_bounds, hgrn_gnorm_w, hgrn_w_out,
           sconv_w_in, sconv_conv_w, sconv_conv_b, sconv_w_out,
           ffn_w_up, ffn_conv_w, ffn_conv_b, ffn_w_down, final_norm_w):
    seq_p = x_prompt.shape[1]

    cvecs = jnp.concatenate([c_ctx[None, :], c], axis=0)
    mod = _ada(cvecs, ada_w, ada_b)
    mod = mod.reshape(mod.shape[0], 8, N_MOD, D_MODEL)
    mod_p = mod[:, 0:1]
    mod_s = mod[:, 1:1 + c.shape[0]]

    row = lambda a: a.reshape(1, -1)
    w_in0 = hgrn_w_in[0].astype(BF16)
    w_out0 = hgrn_w_out[0].astype(BF16)
    w_in1 = sconv_w_in[0].astype(BF16)
    w_out1 = sconv_w_out[0].astype(BF16)
    w_up = ffn_w_up.astype(BF16)
    w_dn = ffn_w_down.astype(BF16)
    gnw = row(hgrn_gnorm_w[0])
    lbp = hgrn_lower_bounds.reshape(-1, hgrn_lower_bounds.shape[-1])

    def layers(x, mods, s0, per_seq_mod, ffn_period, grid_rows, emit_state, gla_heads):
        q, ff, fb, v, g = _inproj0(x, mods[0], row(norm_mix_w[0]), w_in0, lbp, per_seq_mod)
        n_chunks = x.shape[1] // SCAN_CHUNK
        unroll = min(n_chunks, SCAN_LANES // 2)
        head_group = min(gla_heads, SCAN_LANES // (2 * unroll))
        o, s_fin = _gla(q, ff, fb, v, g, s0, gnw, emit_state, gla_heads, head_group, unroll)
        x = _post0(x, o, mods[0], w_out0, row(norm_ffn_w[0]), w_up[0], ffn_conv_w[0],
                   row(ffn_conv_b[0]), w_dn[0], per_seq_mod, ffn_period)
        gb, cu = _inproj1(x, mods[1], row(norm_mix_w[1]), w_in1, per_seq_mod)
        y = _post1(x, gb, cu, mods[1], sconv_conv_w[0], row(sconv_conv_b[0]), w_out1,
                   row(norm_ffn_w[1]), w_up[1], ffn_conv_w[1], row(ffn_conv_b[1]), w_dn[1],
                   row(final_norm_w), per_seq_mod, grid_rows, ffn_period)
        return y, s_fin

    y_prompt, new_state = layers(x_prompt, mod_p, None, False, seq_p, False, True, N_HEADS)
    y_sample, _ = layers(x_sample, mod_s, state_hgrn, True, GRID_W, True, False, 1)
    return (y_prompt, y_sample, new_state.astype(state_hgrn.dtype))
```

```python
import functools

import jax
import jax.numpy as jnp
from jax import lax
from jax.experimental import pallas as pl
from jax.experimental.pallas import tpu as pltpu

D_MODEL = 1024
N_HEADS = 8
D_HEAD = 128
D_FF = 2816
N_MOD = 6
GRID_W = 64
EPS = 1e-6

LANES = 128
SCAN_CHUNK = 64
SCAN_LANES = 16
ROW_TILE = 256
TOKEN_TILE = 512
FF_CHUNK = 256
ADA_COLS = 1024
VMEM_LIMIT = 56 * 1024 * 1024

F32 = jnp.float32
BF16 = jnp.bfloat16


def _sigmoid(x):
    return 1.0 / (1.0 + jnp.exp(-x))


def _silu(x):
    return x * _sigmoid(x)


def _rms(x, w):
    return x * lax.rsqrt(jnp.mean(x * x, axis=-1, keepdims=True) + EPS) * w


def _modulate(x, w, shift, scale):
    return _rms(x, w) * (1.0 + scale) + shift


def _dot(a, b):
    return jnp.dot(a, b, preferred_element_type=F32)


def _dot_nt(a, b):
    return lax.dot_general(a, b, (((1,), (1,)), ((), ())), preferred_element_type=F32)


def _dot_tn(a, b):
    return lax.dot_general(a, b, (((0,), (0,)), ((), ())), preferred_element_type=F32)


def _params(n_grid):
    return pltpu.CompilerParams(
        dimension_semantics=("arbitrary",) * n_grid,
        vmem_limit_bytes=VMEM_LIMIT,
    )


def _const_spec(shape):
    nd = len(shape)
    return pl.BlockSpec(shape, lambda *_: (0,) * nd, pipeline_mode=pl.Buffered(1))


def _ada_kernel(c_ref, w_ref, b_ref, o_ref):
    cols = w_ref.shape[2]
    o_ref[0] = jnp.zeros(o_ref.shape[1:], F32)
    for r in range(c_ref.shape[0]):
        s = _silu(c_ref[r])
        parts = [
            jnp.sum(w_ref[0, :, j * LANES:(j + 1) * LANES] * s, axis=0, keepdims=True)
            for j in range(cols // LANES)
        ]
        o_ref[0, r:r + 1, :] = jnp.concatenate(parts, axis=1) + b_ref[0]


def _ada(cvecs, ada_w, ada_b):
    depth = ada_w.shape[0]
    n_out = ada_w.shape[2]
    cb = jnp.broadcast_to(cvecs[:, :, None], cvecs.shape + (LANES,))
    return pl.pallas_call(
        _ada_kernel,
        grid=(depth, n_out // ADA_COLS),
        in_specs=[
            pl.BlockSpec(cb.shape, lambda l, j: (0, 0, 0)),
            pl.BlockSpec((1, D_MODEL, ADA_COLS), lambda l, j: (l, 0, j)),
            pl.BlockSpec((1, 1, ADA_COLS), lambda l, j: (l, 0, j)),
        ],
        out_specs=pl.BlockSpec((1, 8, ADA_COLS), lambda l, j: (l, 0, j)),
        out_shape=jax.ShapeDtypeStruct((depth, 8, n_out), F32),
        compiler_params=_params(2),
        name="ada",
    )(cb, ada_w, ada_b.reshape(depth, 1, n_out))


def _inproj0_kernel(x_ref, mod_ref, nw_ref, w_ref, lbp_ref, q_ref, ff_ref, fb_ref, v_ref, g_ref):
    h = _modulate(x_ref[0], nw_ref[...], mod_ref[0, 0:1, :], mod_ref[0, 1:2, :]).astype(BF16)

    n_lb = lbp_ref.shape[0] // 2

    def lower_bound(d):
        rows = [lbp_ref[d * n_lb + i:d * n_lb + i + 1, :] for i in range(n_lb)]
        top = functools.reduce(jnp.maximum, rows)
        e = [jnp.exp(r - top) for r in rows]
        return e[0] / functools.reduce(jnp.add, e)

    lb = [lower_bound(0), lower_bound(1)]

    def proj(j):
        return _dot(h, w_ref[:, j * D_MODEL:(j + 1) * D_MODEL])

    def put(ref, val):
        for hh in range(N_HEADS):
            ref[0, hh] = val[:, hh * D_HEAD:(hh + 1) * D_HEAD].astype(ref.dtype)

    put(q_ref, _silu(proj(0)))
    put(ff_ref, lb[0] + (1.0 - lb[0]) * _sigmoid(proj(1)))
    put(fb_ref, lb[1] + (1.0 - lb[1]) * _sigmoid(proj(2)))
    put(v_ref, proj(3))
    put(g_ref, _silu(proj(4)))


def _inproj0(x, mod, nw, w_in, lbp, per_seq_mod):
    bsz, n, _ = x.shape
    t = ROW_TILE
    mod_map = (lambda b, i: (b, 0, 0)) if per_seq_mod else (lambda b, i: (0, 0, 0))
    head_spec = pl.BlockSpec((1, N_HEADS, t, D_HEAD), lambda b, i: (b, 0, i, 0))
    shp = (bsz, N_HEADS, n, D_HEAD)
    return pl.pallas_call(
        _inproj0_kernel,
        grid=(bsz, n // t),
        in_specs=[
            pl.BlockSpec((1, t, D_MODEL), lambda b, i: (b, i, 0)),
            pl.BlockSpec((1, N_MOD, D_MODEL), mod_map),
            _const_spec((1, D_MODEL)),
            _const_spec(w_in.shape),
            _const_spec(lbp.shape),
        ],
        out_specs=[head_spec] * 5,
        out_shape=[
            jax.ShapeDtypeStruct(shp, BF16),
            jax.ShapeDtypeStruct(shp, F32),
            jax.ShapeDtypeStruct(shp, F32),
            jax.ShapeDtypeStruct(shp, BF16),
            jax.ShapeDtypeStruct(shp, BF16),
        ],
        compiler_params=_params(2),
        name="inproj0",
    )(x, mod, nw, w_in, lbp)


def _cumsum_rows(tri, x):
    hi = x.astype(BF16)
    lo = (x - hi.astype(F32)).astype(BF16)
    s = _dot(tri, jnp.concatenate([hi, lo], axis=1))
    return s[:, :D_HEAD] + s[:, D_HEAD:]


def _gla_kernel(*refs, seq_len, heads, head_group, unroll, zero_init, emit_state):
    q_ref, ff_ref, fb_ref, v_ref, g_ref = refs[:5]
    pos = 5
    s0_ref = None
    if not zero_init:
        s0_ref = refs[pos]
        pos += 1
    gnw_ref = refs[pos]
    o_ref = refs[pos + 1]
    pos += 2
    sf_ref = None
    if emit_state:
        sf_ref = refs[pos]
        pos += 1
    of_scr, ob_scr = refs[pos], refs[pos + 1]

    c = SCAN_CHUNK
    n_chunks = seq_len // c
    row = lax.broadcasted_iota(jnp.int32, (c, c), 0)
    col = lax.broadcasted_iota(jnp.int32, (c, c), 1)
    lower = col <= row
    upper = col >= row
    tri_lo = lower.astype(BF16)
    tri_up = upper.astype(BF16)
    gnw = gnw_ref[...]

    directions = (
        (ff_ref, tri_lo, lower, c // 2 - 1, c - 1, of_scr),
        (fb_ref, tri_up, upper, c // 2, 0, ob_scr),
    )

    def trip(hs, i, states):
        lanes = []
        for slot, h in enumerate(hs):
            for d, (f_ref, tri, mask, mid, last, out_scr) in enumerate(directions):
                for u in range(unroll):
                    n = i * unroll + u
                    if d == 1:
                        n = n_chunks - 1 - n
                    r0 = n * c if isinstance(n, int) else pl.multiple_of(n * c, c)
                    lanes.append(dict(slot=slot, h=h, d=d, r0=r0))
        for ln in lanes:
            f_ref, tri = directions[ln["d"]][:2]
            ln["f"] = f_ref[0, ln["h"], pl.ds(ln["r0"], c), :]
            ln["cum"] = _cumsum_rows(tri, jnp.log(ln["f"]))
        for ln in lanes:
            _, _, _, mid, last, _ = directions[ln["d"]]
            cum = ln["cum"]
            cum_mid = cum[mid:mid + 1]
            cum_last = cum[last:last + 1]
            k = 1.0 - ln["f"]
            q = q_ref[0, ln["h"], pl.ds(ln["r0"], c), :].astype(F32)
            ln["v"] = v_ref[0, ln["h"], pl.ds(ln["r0"], c), :]
            ln["q_dec"] = (q * jnp.exp(cum)).astype(BF16)
            q_mid = (q * jnp.exp(cum - cum_mid)).astype(BF16)
            k_mid = (k * jnp.exp(cum_mid - cum)).astype(BF16)
            k_state = (k * jnp.exp(cum_last - cum)).astype(BF16)
            ln["scores"] = _dot_nt(q_mid, k_mid)
            ln["kv"] = _dot_tn(ln["v"], k_state)
            ln["decay"] = jnp.exp(cum_last)
        new_states = []
        for idx, st in enumerate(states):
            for ln in lanes[idx * unroll:(idx + 1) * unroll]:
                ln["st"] = st.astype(BF16)
                st = st * ln["decay"] + ln["kv"]
            new_states.append(st)
        for ln in lanes:
            mask, out_scr = directions[ln["d"]][2], directions[ln["d"]][5]
            p = jnp.where(mask, ln["scores"], 0.0).astype(BF16)
            out_scr[ln["slot"], pl.ds(ln["r0"], c), :] = _dot(p, ln["v"]) + _dot_nt(ln["q_dec"], ln["st"])
        return tuple(new_states)

    def head_group_body(hg, carry):
        hs = [hg * head_group + t for t in range(head_group)]
        states = []
        for h in hs:
            for d in range(2):
                if zero_init:
                    states.append(jnp.zeros((D_HEAD, D_HEAD), F32))
                else:
                    states.append(s0_ref[0, 0, d, h].T)
        states = tuple(states)
        n_trips = n_chunks // unroll
        if n_trips == 1:
            states = trip(hs, 0, states)
        else:
            states = lax.fori_loop(0, n_trips, functools.partial(trip, hs), states)

        eb = min(seq_len, ROW_TILE)
        for slot, h in enumerate(hs):
            def norm_body(j, carry2, slot=slot, h=h):
                r0 = pl.multiple_of(j * eb, eb)
                tot = of_scr[slot, pl.ds(r0, eb), :] + ob_scr[slot, pl.ds(r0, eb), :]
                gate = g_ref[0, h, pl.ds(r0, eb), :].astype(F32)
                o_ref[0, h, pl.ds(r0, eb), :] = (_rms(tot, gnw) * gate).astype(o_ref.dtype)
                return carry2

            if seq_len == eb:
                norm_body(0, 0)
            else:
                lax.fori_loop(0, seq_len // eb, norm_body, 0)
            if emit_state:
                sf_ref[0, 0, 0, h] = states[2 * slot].T
                sf_ref[0, 0, 1, h] = states[2 * slot + 1].T
        return carry

    if heads == head_group:
        head_group_body(0, 0)
    else:
        lax.fori_loop(0, heads // head_group, head_group_body, 0)


def _gla(q, ff, fb, v, g, s0, gnw, emit_state, heads, head_group, unroll):
    bsz, _, n, _ = q.shape
    zero_init = s0 is None
    seq_spec = pl.BlockSpec((1, heads, n, D_HEAD), lambda b, h: (b, h, 0, 0))
    state_spec = pl.BlockSpec((1, 1, 2, heads, D_HEAD, D_HEAD), lambda b, h: (b, 0, 0, h, 0, 0))
    in_specs = [seq_spec] * 5
    args = [q, ff, fb, v, g]
    if not zero_init:
        in_specs.append(state_spec)
        args.append(s0)
    in_specs.append(_const_spec((1, D_HEAD)))
    args.append(gnw)
    out_specs = [seq_spec]
    out_shape = [jax.ShapeDtypeStruct(q.shape, BF16)]
    if emit_state:
        out_specs.append(state_spec)
        out_shape.append(jax.ShapeDtypeStruct((bsz, 1, 2, N_HEADS, D_HEAD, D_HEAD), F32))
    outs = pl.pallas_call(
        functools.partial(_gla_kernel, seq_len=n, heads=heads, head_group=head_group, unroll=unroll,
                          zero_init=zero_init, emit_state=emit_state),
        grid=(bsz, N_HEADS // heads),
        in_specs=in_specs,
        out_specs=out_specs,
        out_shape=out_shape,
        scratch_shapes=[pltpu.VMEM((head_group, n, D_HEAD), F32)] * 2,
        compiler_params=_params(2),
        name="gla",
    )(*args)
    return outs if emit_state else (outs[0], None)


def _conv_ffn(x1, mod_ref, nfw_ref, wup_ref, cw_ref, cb_ref, wdn_ref, hid_scr, period):
    t = x1.shape[0]
    h = _modulate(x1, nfw_ref[...], mod_ref[0, 3:4, :], mod_ref[0, 4:5, :]).astype(BF16)
    pos = lax.broadcasted_iota(jnp.int32, (t, 1), 0) % period
    first = pos == 0
    last = pos == period - 1
    for j in range(D_FF // FF_CHUNK):
        lo = j * FF_CHUNK
        a = _dot(h, wup_ref[:, lo:lo + FF_CHUNK])
        g = _dot(h, wup_ref[:, D_FF + lo:D_FF + lo + FF_CHUNK])
        prev = jnp.where(first, 0.0, pltpu.roll(a, 1, 0))
        nxt = jnp.where(last, 0.0, pltpu.roll(a, t - 1, 0))
        ac = (prev * cw_ref[0:1, lo:lo + FF_CHUNK] + a * cw_ref[1:2, lo:lo + FF_CHUNK]
              + nxt * cw_ref[2:3, lo:lo + FF_CHUNK] + cb_ref[:, lo:lo + FF_CHUNK])
        hid_scr[:, lo:lo + FF_CHUNK] = (_silu(ac) * g).astype(BF16)
    return x1 + mod_ref[0, 5:6, :] * _dot(hid_scr[...], wdn_ref[...])


def _tile(n):
    return (1, TOKEN_TILE) if n >= TOKEN_TILE else (TOKEN_TILE // n, n)


def _rows(ref):
    v = ref[...]
    return v.reshape(v.shape[0] * v.shape[1], v.shape[2])


def _ffn_specs():
    return [
        _const_spec((1, D_MODEL)),
        _const_spec((D_MODEL, 2 * D_FF)),
        _const_spec((3, D_FF)),
        _const_spec((1, D_FF)),
        _const_spec((D_FF, D_MODEL)),
    ]


def _post0_kernel(x_ref, o_ref, mod_ref, wo_ref, nfw_ref, wup_ref, cw_ref, cb_ref, wdn_ref,
                  out_ref, hid_scr, *, period):
    o = jnp.concatenate(
        [jnp.concatenate([o_ref[b, hh] for hh in range(N_HEADS)], axis=1)
         for b in range(o_ref.shape[0])], axis=0)
    x1 = _rows(x_ref) + mod_ref[0, 2:3, :] * _dot(o, wo_ref[...])
    y = _conv_ffn(x1, mod_ref, nfw_ref, wup_ref, cw_ref, cb_ref, wdn_ref, hid_scr, period)
    out_ref[...] = y.reshape(out_ref.shape)


def _post0(x, o, mod, wo, nfw, wup, cw, cb, wdn, per_seq_mod, period):
    bsz, n, _ = x.shape
    tb, tl = _tile(n)
    mod_map = (lambda b, i: (b, 0, 0)) if per_seq_mod else (lambda b, i: (0, 0, 0))
    tok_spec = pl.BlockSpec((tb, tl, D_MODEL), lambda b, i: (b, i, 0))
    return pl.pallas_call(
        functools.partial(_post0_kernel, period=period),
        grid=(bsz // tb, n // tl),
        in_specs=[
            tok_spec,
            pl.BlockSpec((tb, N_HEADS, tl, D_HEAD), lambda b, i: (b, 0, i, 0)),
            pl.BlockSpec((1, N_MOD, D_MODEL), mod_map),
            _const_spec((D_MODEL, D_MODEL)),
        ] + _ffn_specs(),
        out_specs=tok_spec,
        out_shape=jax.ShapeDtypeStruct(x.shape, F32),
        scratch_shapes=[pltpu.VMEM((tb * tl, D_FF), BF16)],
        compiler_params=_params(2),
        name="post0",
    )(x, o, mod, wo, nfw, wup, cw, cb, wdn)


def _inproj1_kernel(x_ref, mod_ref, nw_ref, w_ref, gb_ref, cu_ref):
    h = _modulate(x_ref[0], nw_ref[...], mod_ref[0, 0:1, :], mod_ref[0, 1:2, :]).astype(BF16)
    gb_ref[0] = _dot(h, w_ref[:, 0:D_MODEL]).astype(gb_ref.dtype)
    gate_c = _dot(h, w_ref[:, D_MODEL:2 * D_MODEL])
    u = _dot(h, w_ref[:, 2 * D_MODEL:3 * D_MODEL])
    cu_ref[0] = (gate_c * u).astype(cu_ref.dtype)


def _inproj1(x, mod, nw, w_in, per_seq_mod):
    bsz, n, _ = x.shape
    t = ROW_TILE
    mod_map = (lambda b, i: (b, 0, 0)) if per_seq_mod else (lambda b, i: (0, 0, 0))
    tok_spec = pl.BlockSpec((1, t, D_MODEL), lambda b, i: (b, i, 0))
    return pl.pallas_call(
        _inproj1_kernel,
        grid=(bsz, n // t),
        in_specs=[
            tok_spec,
            pl.BlockSpec((1, N_MOD, D_MODEL), mod_map),
            _const_spec((1, D_MODEL)),
            _const_spec(w_in.shape),
        ],
        out_specs=[tok_spec, tok_spec],
        out_shape=[jax.ShapeDtypeStruct(x.shape, BF16), jax.ShapeDtypeStruct(x.shape, BF16)],
        compiler_params=_params(2),
        name="inproj1",
    )(x, mod, nw, w_in)


def _post1_kernel(*refs, grid_rows, period):
    if grid_rows:
        (x_ref, gb_ref, cu_ref, cup_ref, cun_ref, mod_ref, scw_ref, scb_ref, wo_ref,
         nfw_ref, wup_ref, cw_ref, cb_ref, wdn_ref, fnw_ref, out_ref, hid_scr) = refs
    else:
        (x_ref, gb_ref, cu_ref, mod_ref, scw_ref, scb_ref, wo_ref,
         nfw_ref, wup_ref, cw_ref, cb_ref, wdn_ref, fnw_ref, out_ref, hid_scr) = refs
    cu = _rows(cu_ref).astype(F32)
    t = cu.shape[0]
    if grid_rows:
        i = pl.program_id(1)
        halo_p = jnp.where(i == 0, 0.0, cup_ref[0].astype(F32))
        halo_n = jnp.where(i == pl.num_programs(1) - 1, 0.0, cun_ref[0].astype(F32))
        prev = jnp.concatenate([halo_p, cu[:t - GRID_W]], axis=0)
        nxt = jnp.concatenate([cu[GRID_W:], halo_n], axis=0)
    else:
        pos = lax.broadcasted_iota(jnp.int32, (t, 1), 0) % period
        prev = jnp.where(pos == 0, 0.0, pltpu.roll(cu, 1, 0))
        nxt = jnp.where(pos == period - 1, 0.0, pltpu.roll(cu, t - 1, 0))
    z = prev * scw_ref[0:1, :] + cu * scw_ref[1:2, :] + nxt * scw_ref[2:3, :] + scb_ref[...]
    m = (_rows(gb_ref).astype(F32) * z).astype(BF16)
    x1 = _rows(x_ref) + mod_ref[0, 2:3, :] * _dot(m, wo_ref[...])
    x2 = _conv_ffn(x1, mod_ref, nfw_ref, wup_ref, cw_ref, cb_ref, wdn_ref, hid_scr, period)
    out_ref[...] = _rms(x2, fnw_ref[...]).reshape(out_ref.shape)


def _post1(x, gb, cu, mod, scw, scb, wo, nfw, wup, cw, cb, wdn, fnw, per_seq_mod, grid_rows, period):
    bsz, n, _ = x.shape
    tb, tl = _tile(n)
    mod_map = (lambda b, i: (b, 0, 0)) if per_seq_mod else (lambda b, i: (0, 0, 0))
    tok_spec = pl.BlockSpec((tb, tl, D_MODEL), lambda b, i: (b, i, 0))
    in_specs = [tok_spec, tok_spec, tok_spec]
    args = [x, gb, cu]
    if grid_rows:
        r = tl // GRID_W
        n_rows = n // GRID_W
        in_specs += [
            pl.BlockSpec((1, GRID_W, D_MODEL), lambda b, i: (b, jnp.maximum(i * r - 1, 0), 0)),
            pl.BlockSpec((1, GRID_W, D_MODEL), lambda b, i: (b, jnp.minimum((i + 1) * r, n_rows - 1), 0)),
        ]
        args += [cu, cu]
    in_specs += [
        pl.BlockSpec((1, N_MOD, D_MODEL), mod_map),
        _const_spec((3, D_MODEL)),
        _const_spec((1, D_MODEL)),
        _const_spec((D_MODEL, D_MODEL)),
    ] + _ffn_specs() + [_const_spec((1, D_MODEL))]
    args += [mod, scw, scb, wo, nfw, wup, cw, cb, wdn, fnw]
    return pl.pallas_call(
        functools.partial(_post1_kernel, grid_rows=grid_rows, period=period),
        grid=(bsz // tb, n // tl),
        in_specs=in_specs,
        out_specs=tok_spec,
        out_shape=jax.ShapeDtypeStruct(x.shape, F32),
        scratch_shapes=[pltpu.VMEM((tb * tl, D_FF), BF16)],
        compiler_params=_params(2),
        name="post1",
    )(*args)


def kernel(x_prompt, x_sample, state_hgrn, c, c_ctx, ada_w, ada_b, norm_mix_w, norm_ffn_w,
           hgrn_w_in, hgrn_lower_bounds, hgrn_gnorm_w, hgrn_w_out,
           sconv_w_in, sconv_conv_w, sconv_conv_b, sconv_w_out,
           ffn_w_up, ffn_conv_w, ffn_conv_b, ffn_w_down, final_norm_w):
    seq_p = x_prompt.shape[1]

    cvecs = jnp.concatenate([c_ctx[None, :], c], axis=0)
    mod = _ada(cvecs, ada_w, ada_b)
    mod = mod.reshape(mod.shape[0], 8, N_MOD, D_MODEL)
    mod_p = mod[:, 0:1]
    mod_s = mod[:, 1:1 + c.shape[0]]

    row = lambda a: a.reshape(1, -1)
    w_in0 = hgrn_w_in[0].astype(BF16)
    w_out0 = hgrn_w_out[0].astype(BF16)
    w_in1 = sconv_w_in[0].astype(BF16)
    w_out1 = sconv_w_out[0].astype(BF16)
    w_up = ffn_w_up.astype(BF16)
    w_dn = ffn_w_down.astype(BF16)
    gnw = row(hgrn_gnorm_w[0])
    lbp = hgrn_lower_bounds.reshape(-1, hgrn_lower_bounds.shape[-1])

    def layers(x, mods, s0, per_seq_mod, ffn_period, grid_rows, emit_state, gla_heads):
        q, ff, fb, v, g = _inproj0(x, mods[0], row(norm_mix_w[0]), w_in0, lbp, per_seq_mod)
        n_chunks = x.shape[1] // SCAN_CHUNK
        unroll = min(n_chunks, SCAN_LANES // 2)
        head_group = min(gla_heads, SCAN_LANES // (2 * unroll))
        o, s_fin = _gla(q, ff, fb, v, g, s0, gnw, emit_state, gla_heads, head_group, unroll)
        x = _post0(x, o, mods[0], w_out0, row(norm_ffn_w[0]), w_up[0], ffn_conv_w[0],
                   row(ffn_conv_b[0]), w_dn[0], per_seq_mod, ffn_period)
        gb, cu = _inproj1(x, mods[1], row(norm_mix_w[1]), w_in1, per_seq_mod)
        y = _post1(x, gb, cu, mods[1], sconv_conv_w[0], row(sconv_conv_b[0]), w_out1,
                   row(norm_ffn_w[1]), w_up[1], ffn_conv_w[1], row(ffn_conv_b[1]), w_dn[1],
                   row(final_norm_w), per_seq_mod, grid_rows, ffn_period)
        return y, s_fin

    y_prompt, new_state = layers(x_prompt, mod_p, None, False, seq_p, False, True, N_HEADS)
    y_sample, _ = layers(x_sample, mod_s, state_hgrn, True, GRID_W, True, False, 1)
    return (y_prompt, y_sample, new_state.astype(state_hgrn.dtype))
```

```python
import functools

import jax
import jax.numpy as jnp
from jax import lax
from jax.experimental import pallas as pl
from jax.experimental.pallas import tpu as pltpu

D_MODEL = 1024
N_HEADS = 8
D_HEAD = 128
D_FF = 2816
N_MOD = 6
GRID_W = 64
EPS = 1e-6

LANES = 128
SCAN_CHUNK = 64
SCAN_LANES = 16
ROW_TILE = 256
TOKEN_TILE = 512
FF_CHUNK = 256
ADA_COLS = 1024
VMEM_LIMIT = 56 * 1024 * 1024

F32 = jnp.float32
BF16 = jnp.bfloat16


def _sigmoid(x):
    return 1.0 / (1.0 + jnp.exp(-x))


def _silu(x):
    return x * _sigmoid(x)


def _rms(x, w):
    return x * lax.rsqrt(jnp.mean(x * x, axis=-1, keepdims=True) + EPS) * w


def _modulate(x, w, shift, scale):
    return _rms(x, w) * (1.0 + scale) + shift


def _dot(a, b):
    return jnp.dot(a, b, preferred_element_type=F32)


def _dot_nt(a, b):
    return lax.dot_general(a, b, (((1,), (1,)), ((), ())), preferred_element_type=F32)


def _dot_tn(a, b):
    return lax.dot_general(a, b, (((0,), (0,)), ((), ())), preferred_element_type=F32)


def _params(n_grid):
    return pltpu.CompilerParams(
        dimension_semantics=("arbitrary",) * n_grid,
        vmem_limit_bytes=VMEM_LIMIT,
    )


def _const_spec(shape):
    nd = len(shape)
    return pl.BlockSpec(shape, lambda *_: (0,) * nd, pipeline_mode=pl.Buffered(1))


def _ada_kernel(c_ref, w_ref, b_ref, o_ref):
    cols = w_ref.shape[2]
    o_ref[0] = jnp.zeros(o_ref.shape[1:], F32)
    for r in range(c_ref.shape[0]):
        s = _silu(c_ref[r])
        parts = [
            jnp.sum(w_ref[0, :, j * LANES:(j + 1) * LANES] * s, axis=0, keepdims=True)
            for j in range(cols // LANES)
        ]
        o_ref[0, r:r + 1, :] = jnp.concatenate(parts, axis=1) + b_ref[0]


def _ada(cvecs, ada_w, ada_b):
    depth = ada_w.shape[0]
    n_out = ada_w.shape[2]
    cb = jnp.broadcast_to(cvecs[:, :, None], cvecs.shape + (LANES,))
    return pl.pallas_call(
        _ada_kernel,
        grid=(depth, n_out // ADA_COLS),
        in_specs=[
            pl.BlockSpec(cb.shape, lambda l, j: (0, 0, 0)),
            pl.BlockSpec((1, D_MODEL, ADA_COLS), lambda l, j: (l, 0, j)),
            pl.BlockSpec((1, 1, ADA_COLS), lambda l, j: (l, 0, j)),
        ],
        out_specs=pl.BlockSpec((1, 8, ADA_COLS), lambda l, j: (l, 0, j)),
        out_shape=jax.ShapeDtypeStruct((depth, 8, n_out), F32),
        compiler_params=_params(2),
        name="ada",
    )(cb, ada_w, ada_b.reshape(depth, 1, n_out))


def _inproj0_kernel(x_ref, mod_ref, nw_ref, w_ref, lbp_ref, q_ref, ff_ref, fb_ref, v_ref, g_ref):
    h = _modulate(_rows(x_ref), nw_ref[...], mod_ref[0, 0:1, :], mod_ref[0, 1:2, :]).astype(BF16)
    tb, _, tl, _ = q_ref.shape

    n_lb = lbp_ref.shape[0] // 2

    def lower_bound(d):
        rows = [lbp_ref[d * n_lb + i:d * n_lb + i + 1, :] for i in range(n_lb)]
        top = functools.reduce(jnp.maximum, rows)
        e = [jnp.exp(r - top) for r in rows]
        return e[0] / functools.reduce(jnp.add, e)

    lb = [lower_bound(0), lower_bound(1)]

    def proj(j):
        return _dot(h, w_ref[:, j * D_MODEL:(j + 1) * D_MODEL])

    def put(ref, val):
        for b in range(tb):
            for hh in range(N_HEADS):
                ref[b, hh] = val[b * tl:(b + 1) * tl, hh * D_HEAD:(hh + 1) * D_HEAD].astype(ref.dtype)

    put(q_ref, _silu(proj(0)))
    put(ff_ref, lb[0] + (1.0 - lb[0]) * _sigmoid(proj(1)))
    put(fb_ref, lb[1] + (1.0 - lb[1]) * _sigmoid(proj(2)))
    put(v_ref, proj(3))
    put(g_ref, _silu(proj(4)))


def _inproj0(x, mod, nw, w_in, lbp, per_seq_mod):
    bsz, n, _ = x.shape
    tb, tl = _tile(n)
    mod_map = (lambda b, i: (b, 0, 0)) if per_seq_mod else (lambda b, i: (0, 0, 0))
    head_spec = pl.BlockSpec((tb, N_HEADS, tl, D_HEAD), lambda b, i: (b, 0, i, 0))
    shp = (bsz, N_HEADS, n, D_HEAD)
    return pl.pallas_call(
        _inproj0_kernel,
        grid=(bsz // tb, n // tl),
        in_specs=[
            pl.BlockSpec((tb, tl, D_MODEL), lambda b, i: (b, i, 0)),
            pl.BlockSpec((1, N_MOD, D_MODEL), mod_map),
            _const_spec((1, D_MODEL)),
            _const_spec(w_in.shape),
            _const_spec(lbp.shape),
        ],
        out_specs=[head_spec] * 5,
        out_shape=[
            jax.ShapeDtypeStruct(shp, BF16),
            jax.ShapeDtypeStruct(shp, F32),
            jax.ShapeDtypeStruct(shp, F32),
            jax.ShapeDtypeStruct(shp, BF16),
            jax.ShapeDtypeStruct(shp, BF16),
        ],
        compiler_params=_params(2),
        name="inproj0",
    )(x, mod, nw, w_in, lbp)


def _cumsum_rows(tri, x):
    hi = x.astype(BF16)
    lo = (x - hi.astype(F32)).astype(BF16)
    s = _dot(tri, jnp.concatenate([hi, lo], axis=1))
    return s[:, :D_HEAD] + s[:, D_HEAD:]


def _gla_kernel(*refs, seq_len, heads, head_group, unroll, zero_init, emit_state):
    q_ref, ff_ref, fb_ref, v_ref, g_ref = refs[:5]
    pos = 5
    s0_ref = None
    if not zero_init:
        s0_ref = refs[pos]
        pos += 1
    gnw_ref = refs[pos]
    o_ref = refs[pos + 1]
    pos += 2
    sf_ref = None
    if emit_state:
        sf_ref = refs[pos]
        pos += 1
    of_scr, ob_scr = refs[pos], refs[pos + 1]

    c = SCAN_CHUNK
    n_chunks = seq_len // c
    row = lax.broadcasted_iota(jnp.int32, (c, c), 0)
    col = lax.broadcasted_iota(jnp.int32, (c, c), 1)
    lower = col <= row
    upper = col >= row
    tri_lo = lower.astype(BF16)
    tri_up = upper.astype(BF16)
    gnw = gnw_ref[...]

    directions = (
        (ff_ref, tri_lo, lower, c // 2 - 1, c - 1, of_scr),
        (fb_ref, tri_up, upper, c // 2, 0, ob_scr),
    )

    def trip(hs, i, states):
        lanes = []
        for slot, h in enumerate(hs):
            for d, (f_ref, tri, mask, mid, last, out_scr) in enumerate(directions):
                for u in range(unroll):
                    n = i * unroll + u
                    if d == 1:
                        n = n_chunks - 1 - n
                    r0 = n * c if isinstance(n, int) else pl.multiple_of(n * c, c)
                    lanes.append(dict(slot=slot, h=h, d=d, r0=r0))
        for ln in lanes:
            f_ref, tri = directions[ln["d"]][:2]
            ln["f"] = f_ref[0, ln["h"], pl.ds(ln["r0"], c), :]
            ln["cum"] = _cumsum_rows(tri, jnp.log2(ln["f"]))
        for ln in lanes:
            _, _, _, mid, last, _ = directions[ln["d"]]
            cum = ln["cum"]
            cum_mid = cum[mid:mid + 1]
            cum_last = cum[last:last + 1]
            k = 1.0 - ln["f"]
            q = q_ref[0, ln["h"], pl.ds(ln["r0"], c), :].astype(F32)
            ln["v"] = v_ref[0, ln["h"], pl.ds(ln["r0"], c), :]
            q_mid = q * jnp.exp2(cum - cum_mid)
            k_mid = k * jnp.exp2(cum_mid - cum)
            ln["q_dec"] = (q_mid * jnp.exp2(cum_mid)).astype(BF16)
            k_state = (k_mid * jnp.exp2(cum_last - cum_mid)).astype(BF16)
            ln["scores"] = _dot_nt(q_mid.astype(BF16), k_mid.astype(BF16))
            ln["kv"] = _dot_tn(ln["v"], k_state)
            ln["decay"] = jnp.exp2(cum_last)
        new_states = []
        for idx, st in enumerate(states):
            for ln in lanes[idx * unroll:(idx + 1) * unroll]:
                ln["st"] = st.astype(BF16)
                st = st * ln["decay"] + ln["kv"]
            new_states.append(st)
        for ln in lanes:
            mask, out_scr = directions[ln["d"]][2], directions[ln["d"]][5]
            p = jnp.where(mask, ln["scores"], 0.0).astype(BF16)
            out_scr[ln["slot"], pl.ds(ln["r0"], c), :] = _dot(p, ln["v"]) + _dot_nt(ln["q_dec"], ln["st"])
        return tuple(new_states)

    def head_group_body(hg, carry):
        hs = [hg * head_group + t for t in range(head_group)]
        states = []
        for h in hs:
            for d in range(2):
                if zero_init:
                    states.append(jnp.zeros((D_HEAD, D_HEAD), F32))
                else:
                    states.append(s0_ref[0, 0, d, h].T)
        states = tuple(states)
        n_trips = n_chunks // unroll
        if n_trips == 1:
            states = trip(hs, 0, states)
        else:
            states = lax.fori_loop(0, n_trips, functools.partial(trip, hs), states)

        eb = min(seq_len, ROW_TILE)
        for slot, h in enumerate(hs):
            def norm_body(j, carry2, slot=slot, h=h):
                r0 = pl.multiple_of(j * eb, eb)
                tot = of_scr[slot, pl.ds(r0, eb), :] + ob_scr[slot, pl.ds(r0, eb), :]
                gate = g_ref[0, h, pl.ds(r0, eb), :].astype(F32)
                o_ref[0, h, pl.ds(r0, eb), :] = (_rms(tot, gnw) * gate).astype(o_ref.dtype)
                return carry2

            if seq_len == eb:
                norm_body(0, 0)
            else:
                lax.fori_loop(0, seq_len // eb, norm_body, 0)
            if emit_state:
                sf_ref[0, 0, 0, h] = states[2 * slot].T
                sf_ref[0, 0, 1, h] = states[2 * slot + 1].T
        return carry

    if heads == head_group:
        head_group_body(0, 0)
    else:
        lax.fori_loop(0, heads // head_group, head_group_body, 0)


def _gla(q, ff, fb, v, g, s0, gnw, emit_state, heads, head_group, unroll):
    bsz, _, n, _ = q.shape
    zero_init = s0 is None
    seq_spec = pl.BlockSpec((1, heads, n, D_HEAD), lambda b, h: (b, h, 0, 0))
    state_spec = pl.BlockSpec((1, 1, 2, heads, D_HEAD, D_HEAD), lambda b, h: (b, 0, 0, h, 0, 0))
    in_specs = [seq_spec] * 5
    args = [q, ff, fb, v, g]
    if not zero_init:
        in_specs.append(state_spec)
        args.append(s0)
    in_specs.append(_const_spec((1, D_HEAD)))
    args.append(gnw)
    out_specs = [seq_spec]
    out_shape = [jax.ShapeDtypeStruct(q.shape, BF16)]
    if emit_state:
        out_specs.append(state_spec)
        out_shape.append(jax.ShapeDtypeStruct((bsz, 1, 2, N_HEADS, D_HEAD, D_HEAD), F32))
    outs = pl.pallas_call(
        functools.partial(_gla_kernel, seq_len=n, heads=heads, head_group=head_group, unroll=unroll,
                          zero_init=zero_init, emit_state=emit_state),
        grid=(bsz, N_HEADS // heads),
        in_specs=in_specs,
        out_specs=out_specs,
        out_shape=out_shape,
        scratch_shapes=[pltpu.VMEM((head_group, n, D_HEAD), F32)] * 2,
        compiler_params=_params(2),
        name="gla",
    )(*args)
    return outs if emit_state else (outs[0], None)


def _conv_ffn(x1, mod_ref, nfw_ref, wup_ref, cw_ref, cb_ref, wdn_ref, hid_scr, period):
    t = x1.shape[0]
    h = _modulate(x1, nfw_ref[...], mod_ref[0, 3:4, :], mod_ref[0, 4:5, :]).astype(BF16)
    pos = lax.broadcasted_iota(jnp.int32, (t, 1), 0) % period
    first = pos == 0
    last = pos == period - 1
    for j in range(D_FF // FF_CHUNK):
        lo = j * FF_CHUNK
        a = _dot(h, wup_ref[:, lo:lo + FF_CHUNK])
        g = _dot(h, wup_ref[:, D_FF + lo:D_FF + lo + FF_CHUNK])
        prev = jnp.where(first, 0.0, pltpu.roll(a, 1, 0))
        nxt = jnp.where(last, 0.0, pltpu.roll(a, t - 1, 0))
        ac = (prev * cw_ref[0:1, lo:lo + FF_CHUNK] + a * cw_ref[1:2, lo:lo + FF_CHUNK]
              + nxt * cw_ref[2:3, lo:lo + FF_CHUNK] + cb_ref[:, lo:lo + FF_CHUNK])
        hid_scr[:, lo:lo + FF_CHUNK] = (_silu(ac) * g).astype(BF16)
    return x1 + mod_ref[0, 5:6, :] * _dot(hid_scr[...], wdn_ref[...])


def _tile(n):
    return (1, TOKEN_TILE) if n >= TOKEN_TILE else (TOKEN_TILE // n, n)


def _rows(ref):
    v = ref[...]
    return v.reshape(v.shape[0] * v.shape[1], v.shape[2])


def _ffn_specs():
    return [
        _const_spec((1, D_MODEL)),
        _const_spec((D_MODEL, 2 * D_FF)),
        _const_spec((3, D_FF)),
        _const_spec((1, D_FF)),
        _const_spec((D_FF, D_MODEL)),
    ]


def _post0_kernel(x_ref, o_ref, mod_ref, wo_ref, nfw_ref, wup_ref, cw_ref, cb_ref, wdn_ref,
                  mod1_ref, nw1_ref, w1_ref, out_ref, gb_ref, cu_ref, hid_scr, *, period):
    o = jnp.concatenate(
        [jnp.concatenate([o_ref[b, hh] for hh in range(N_HEADS)], axis=1)
         for b in range(o_ref.shape[0])], axis=0)
    x1 = _rows(x_ref) + mod_ref[0, 2:3, :] * _dot(o, wo_ref[...])
    x2 = _conv_ffn(x1, mod_ref, nfw_ref, wup_ref, cw_ref, cb_ref, wdn_ref, hid_scr, period)
    out_ref[...] = x2.reshape(out_ref.shape)
    h = _modulate(x2, nw1_ref[...], mod1_ref[0, 0:1, :], mod1_ref[0, 1:2, :]).astype(BF16)
    gb_ref[...] = _dot(h, w1_ref[:, 0:D_MODEL]).astype(gb_ref.dtype).reshape(gb_ref.shape)
    gate_c = _dot(h, w1_ref[:, D_MODEL:2 * D_MODEL])
    u = _dot(h, w1_ref[:, 2 * D_MODEL:3 * D_MODEL])
    cu_ref[...] = (gate_c * u).astype(cu_ref.dtype).reshape(cu_ref.shape)


def _post0(x, o, mod, wo, nfw, wup, cw, cb, wdn, mod1, nw1, w1, per_seq_mod, period):
    bsz, n, _ = x.shape
    tb, tl = _tile(n)
    mod_map = (lambda b, i: (b, 0, 0)) if per_seq_mod else (lambda b, i: (0, 0, 0))
    tok_spec = pl.BlockSpec((tb, tl, D_MODEL), lambda b, i: (b, i, 0))
    return pl.pallas_call(
        functools.partial(_post0_kernel, period=period),
        grid=(bsz // tb, n // tl),
        in_specs=[
            tok_spec,
            pl.BlockSpec((tb, N_HEADS, tl, D_HEAD), lambda b, i: (b, 0, i, 0)),
            pl.BlockSpec((1, N_MOD, D_MODEL), mod_map),
            _const_spec((D_MODEL, D_MODEL)),
        ] + _ffn_specs() + [
            pl.BlockSpec((1, N_MOD, D_MODEL), mod_map),
            _const_spec((1, D_MODEL)),
            _const_spec(w1.shape),
        ],
        out_specs=[tok_spec] * 3,
        out_shape=[jax.ShapeDtypeStruct(x.shape, F32), jax.ShapeDtypeStruct(x.shape, BF16),
                   jax.ShapeDtypeStruct(x.shape, BF16)],
        scratch_shapes=[pltpu.VMEM((tb * tl, D_FF), BF16)],
        compiler_params=_params(2),
        name="post0",
    )(x, o, mod, wo, nfw, wup, cw, cb, wdn, mod1, nw1, w1)


def _post1_kernel(*refs, grid_rows, period):
    if grid_rows:
        (x_ref, gb_ref, cu_ref, cup_ref, cun_ref, mod_ref, scw_ref, scb_ref, wo_ref,
         nfw_ref, wup_ref, cw_ref, cb_ref, wdn_ref, fnw_ref, out_ref, hid_scr) = refs
    else:
        (x_ref, gb_ref, cu_ref, mod_ref, scw_ref, scb_ref, wo_ref,
         nfw_ref, wup_ref, cw_ref, cb_ref, wdn_ref, fnw_ref, out_ref, hid_scr) = refs
    cu = _rows(cu_ref).astype(F32)
    t = cu.shape[0]
    if grid_rows:
        i = pl.program_id(1)
        halo_p = jnp.where(i == 0, 0.0, cup_ref[0].astype(F32))
        halo_n = jnp.where(i == pl.num_programs(1) - 1, 0.0, cun_ref[0].astype(F32))
        prev = jnp.concatenate([halo_p, cu[:t - GRID_W]], axis=0)
        nxt = jnp.concatenate([cu[GRID_W:], halo_n], axis=0)
    else:
        pos = lax.broadcasted_iota(jnp.int32, (t, 1), 0) % period
        prev = jnp.where(pos == 0, 0.0, pltpu.roll(cu, 1, 0))
        nxt = jnp.where(pos == period - 1, 0.0, pltpu.roll(cu, t - 1, 0))
    z = prev * scw_ref[0:1, :] + cu * scw_ref[1:2, :] + nxt * scw_ref[2:3, :] + scb_ref[...]
    m = (_rows(gb_ref).astype(F32) * z).astype(BF16)
    x1 = _rows(x_ref) + mod_ref[0, 2:3, :] * _dot(m, wo_ref[...])
    x2 = _conv_ffn(x1, mod_ref, nfw_ref, wup_ref, cw_ref, cb_ref, wdn_ref, hid_scr, period)
    out_ref[...] = _rms(x2, fnw_ref[...]).reshape(out_ref.shape)


def _post1(x, gb, cu, mod, scw, scb, wo, nfw, wup, cw, cb, wdn, fnw, per_seq_mod, grid_rows, period):
    bsz, n, _ = x.shape
    tb, tl = _tile(n)
    mod_map = (lambda b, i: (b, 0, 0)) if per_seq_mod else (lambda b, i: (0, 0, 0))
    tok_spec = pl.BlockSpec((tb, tl, D_MODEL), lambda b, i: (b, i, 0))
    in_specs = [tok_spec, tok_spec, tok_spec]
    args = [x, gb, cu]
    if grid_rows:
        r = tl // GRID_W
        n_rows = n // GRID_W
        in_specs += [
            pl.BlockSpec((1, GRID_W, D_MODEL), lambda b, i: (b, jnp.maximum(i * r - 1, 0), 0)),
            pl.BlockSpec((1, GRID_W, D_MODEL), lambda b, i: (b, jnp.minimum((i + 1) * r, n_rows - 1), 0)),
        ]
        args += [cu, cu]
    in_specs += [
        pl.BlockSpec((1, N_MOD, D_MODEL), mod_map),
        _const_spec((3, D_MODEL)),
        _const_spec((1, D_MODEL)),
        _const_spec((D_MODEL, D_MODEL)),
    ] + _ffn_specs() + [_const_spec((1, D_MODEL))]
    args += [mod, scw, scb, wo, nfw, wup, cw, cb, wdn, fnw]
    return pl.pallas_call(
        functools.partial(_post1_kernel, grid_rows=grid_rows, period=period),
        grid=(bsz // tb, n // tl),
        in_specs=in_specs,
        out_specs=tok_spec,
        out_shape=jax.ShapeDtypeStruct(x.shape, F32),
        scratch_shapes=[pltpu.VMEM((tb * tl, D_FF), BF16)],
        compiler_params=_params(2),
        name="post1",
    )(*args)


def kernel(x_prompt, x_sample, state_hgrn, c, c_ctx, ada_w, ada_b, norm_mix_w, norm_ffn_w,
           hgrn_w_in, hgrn_lower_bounds, hgrn_gnorm_w, hgrn_w_out,
           sconv_w_in, sconv_conv_w, sconv_conv_b, sconv_w_out,
           ffn_w_up, ffn_conv_w, ffn_conv_b, ffn_w_down, final_norm_w):
    seq_p = x_prompt.shape[1]

    cvecs = jnp.concatenate([c_ctx[None, :], c], axis=0)
    mod = _ada(cvecs, ada_w, ada_b)
    mod = mod.reshape(mod.shape[0], 8, N_MOD, D_MODEL)
    mod_p = mod[:, 0:1]
    mod_s = mod[:, 1:1 + c.shape[0]]

    row = lambda a: a.reshape(1, -1)
    w_in0 = hgrn_w_in[0].astype(BF16)
    w_out0 = hgrn_w_out[0].astype(BF16)
    w_in1 = sconv_w_in[0].astype(BF16)
    w_out1 = sconv_w_out[0].astype(BF16)
    w_up = ffn_w_up.astype(BF16)
    w_dn = ffn_w_down.astype(BF16)
    gnw = row(hgrn_gnorm_w[0])
    lbp = hgrn_lower_bounds.reshape(-1, hgrn_lower_bounds.shape[-1])

    def layers(x, mods, s0, per_seq_mod, ffn_period, grid_rows, emit_state, gla_heads):
        q, ff, fb, v, g = _inproj0(x, mods[0], row(norm_mix_w[0]), w_in0, lbp, per_seq_mod)
        n_chunks = x.shape[1] // SCAN_CHUNK
        unroll = min(n_chunks, SCAN_LANES // 2)
        head_group = min(gla_heads, SCAN_LANES // (2 * unroll))
        o, s_fin = _gla(q, ff, fb, v, g, s0, gnw, emit_state, gla_heads, head_group, unroll)
        x, gb, cu = _post0(x, o, mods[0], w_out0, row(norm_ffn_w[0]), w_up[0], ffn_conv_w[0],
                           row(ffn_conv_b[0]), w_dn[0], mods[1], row(norm_mix_w[1]), w_in1,
                           per_seq_mod, ffn_period)
        y = _post1(x, gb, cu, mods[1], sconv_conv_w[0], row(sconv_conv_b[0]), w_out1,
                   row(norm_ffn_w[1]), w_up[1], ffn_conv_w[1], row(ffn_conv_b[1]), w_dn[1],
                   row(final_norm_w), per_seq_mod, grid_rows, ffn_period)
        return y, s_fin

    y_prompt, new_state = layers(x_prompt, mod_p, None, False, seq_p, False, True, N_HEADS)
    y_sample, _ = layers(x_sample, mod_s, state_hgrn, True, GRID_W, True, False, 1)
    return (y_prompt, y_sample, new_state.astype(state_hgrn.dtype))
```

```python
import functools

import jax
import jax.numpy as jnp
from jax import lax
from jax.experimental import pallas as pl
from jax.experimental.pallas import tpu as pltpu

D_MODEL = 1024
N_HEADS = 8
D_HEAD = 128
D_FF = 2816
N_MOD = 6
GRID_W = 64
EPS = 1e-6

LANES = 128
SCAN_CHUNK = 64
SCAN_LANES = 32
ROW_TILE = 256
TOKEN_TILE = 512
FF_CHUNK = 256
ADA_COLS = 1024
VMEM_LIMIT = 56 * 1024 * 1024

F32 = jnp.float32
BF16 = jnp.bfloat16


def _sigmoid(x):
    return 1.0 / (1.0 + jnp.exp(-x))


def _silu(x):
    return x * _sigmoid(x)


def _rms(x, w):
    return x * lax.rsqrt(jnp.mean(x * x, axis=-1, keepdims=True) + EPS) * w


def _modulate(x, w, shift, scale):
    return _rms(x, w) * (1.0 + scale) + shift


def _dot(a, b):
    return jnp.dot(a, b, preferred_element_type=F32)


def _dot_nt(a, b):
    return lax.dot_general(a, b, (((1,), (1,)), ((), ())), preferred_element_type=F32)


def _dot_tn(a, b):
    return lax.dot_general(a, b, (((0,), (0,)), ((), ())), preferred_element_type=F32)


def _params(n_grid):
    return pltpu.CompilerParams(
        dimension_semantics=("arbitrary",) * n_grid,
        vmem_limit_bytes=VMEM_LIMIT,
    )


def _const_spec(shape):
    nd = len(shape)
    return pl.BlockSpec(shape, lambda *_: (0,) * nd, pipeline_mode=pl.Buffered(1))


def _ada_kernel(c_ref, w_ref, b_ref, o_ref):
    cols = w_ref.shape[2]
    o_ref[0] = jnp.zeros(o_ref.shape[1:], F32)
    for r in range(c_ref.shape[0]):
        s = _silu(c_ref[r])
        parts = [
            jnp.sum(w_ref[0, :, j * LANES:(j + 1) * LANES] * s, axis=0, keepdims=True)
            for j in range(cols // LANES)
        ]
        o_ref[0, r:r + 1, :] = jnp.concatenate(parts, axis=1) + b_ref[0]


def _ada(cvecs, ada_w, ada_b):
    depth = ada_w.shape[0]
    n_out = ada_w.shape[2]
    cb = jnp.broadcast_to(cvecs[:, :, None], cvecs.shape + (LANES,))
    return pl.pallas_call(
        _ada_kernel,
        grid=(depth, n_out // ADA_COLS),
        in_specs=[
            pl.BlockSpec(cb.shape, lambda l, j: (0, 0, 0)),
            pl.BlockSpec((1, D_MODEL, ADA_COLS), lambda l, j: (l, 0, j)),
            pl.BlockSpec((1, 1, ADA_COLS), lambda l, j: (l, 0, j)),
        ],
        out_specs=pl.BlockSpec((1, 8, ADA_COLS), lambda l, j: (l, 0, j)),
        out_shape=jax.ShapeDtypeStruct((depth, 8, n_out), F32),
        compiler_params=_params(2),
        name="ada",
    )(cb, ada_w, ada_b.reshape(depth, 1, n_out))


def _inproj0_kernel(x_ref, mod_ref, nw_ref, w_ref, lbp_ref, q_ref, ff_ref, fb_ref, v_ref, g_ref):
    h = _modulate(_rows(x_ref), nw_ref[...], mod_ref[0, 0:1, :], mod_ref[0, 1:2, :]).astype(BF16)
    tb, _, tl, _ = q_ref.shape

    n_lb = lbp_ref.shape[0] // 2

    def lower_bound(d):
        rows = [lbp_ref[d * n_lb + i:d * n_lb + i + 1, :] for i in range(n_lb)]
        top = functools.reduce(jnp.maximum, rows)
        e = [jnp.exp(r - top) for r in rows]
        return e[0] / functools.reduce(jnp.add, e)

    lb = [lower_bound(0), lower_bound(1)]

    def proj(j):
        return _dot(h, w_ref[:, j * D_MODEL:(j + 1) * D_MODEL])

    def put(ref, val):
        for b in range(tb):
            for hh in range(N_HEADS):
                ref[b, hh] = val[b * tl:(b + 1) * tl, hh * D_HEAD:(hh + 1) * D_HEAD].astype(ref.dtype)

    put(q_ref, _silu(proj(0)))
    put(ff_ref, lb[0] + (1.0 - lb[0]) * _sigmoid(proj(1)))
    put(fb_ref, lb[1] + (1.0 - lb[1]) * _sigmoid(proj(2)))
    put(v_ref, proj(3))
    put(g_ref, _silu(proj(4)))


def _inproj0(x, mod, nw, w_in, lbp, per_seq_mod):
    bsz, n, _ = x.shape
    tb, tl = _tile(n)
    mod_map = (lambda b, i: (b, 0, 0)) if per_seq_mod else (lambda b, i: (0, 0, 0))
    head_spec = pl.BlockSpec((tb, N_HEADS, tl, D_HEAD), lambda b, i: (b, 0, i, 0))
    shp = (bsz, N_HEADS, n, D_HEAD)
    return pl.pallas_call(
        _inproj0_kernel,
        grid=(bsz // tb, n // tl),
        in_specs=[
            pl.BlockSpec((tb, tl, D_MODEL), lambda b, i: (b, i, 0)),
            pl.BlockSpec((1, N_MOD, D_MODEL), mod_map),
            _const_spec((1, D_MODEL)),
            _const_spec(w_in.shape),
            _const_spec(lbp.shape),
        ],
        out_specs=[head_spec] * 5,
        out_shape=[
            jax.ShapeDtypeStruct(shp, BF16),
            jax.ShapeDtypeStruct(shp, F32),
            jax.ShapeDtypeStruct(shp, F32),
            jax.ShapeDtypeStruct(shp, BF16),
            jax.ShapeDtypeStruct(shp, BF16),
        ],
        compiler_params=_params(2),
        name="inproj0",
    )(x, mod, nw, w_in, lbp)


def _cumsum_rows(tri, x):
    hi = x.astype(BF16)
    lo = (x - hi.astype(F32)).astype(BF16)
    s = _dot(tri, jnp.concatenate([hi, lo], axis=1))
    return s[:, :D_HEAD] + s[:, D_HEAD:]


def _gla_kernel(*refs, seq_len, heads, head_group, unroll, zero_init, emit_state):
    q_ref, ff_ref, fb_ref, v_ref, g_ref = refs[:5]
    pos = 5
    s0_ref = None
    if not zero_init:
        s0_ref = refs[pos]
        pos += 1
    gnw_ref = refs[pos]
    o_ref = refs[pos + 1]
    pos += 2
    sf_ref = None
    if emit_state:
        sf_ref = refs[pos]
        pos += 1
    of_scr, ob_scr = refs[pos], refs[pos + 1]

    c = SCAN_CHUNK
    n_chunks = seq_len // c
    row = lax.broadcasted_iota(jnp.int32, (c, c), 0)
    col = lax.broadcasted_iota(jnp.int32, (c, c), 1)
    lower = col <= row
    upper = col >= row
    tri_lo = lower.astype(BF16)
    tri_up = upper.astype(BF16)
    gnw = gnw_ref[...]

    directions = (
        (ff_ref, tri_lo, lower, c // 2 - 1, c - 1, of_scr),
        (fb_ref, tri_up, upper, c // 2, 0, ob_scr),
    )

    def trip(hs, i, states):
        lanes = []
        for slot, h in enumerate(hs):
            for d, (f_ref, tri, mask, mid, last, out_scr) in enumerate(directions):
                for u in range(unroll):
                    n = i * unroll + u
                    if d == 1:
                        n = n_chunks - 1 - n
                    r0 = n * c if isinstance(n, int) else pl.multiple_of(n * c, c)
                    lanes.append(dict(slot=slot, h=h, d=d, r0=r0))
        for ln in lanes:
            f_ref, tri = directions[ln["d"]][:2]
            ln["f"] = f_ref[0, ln["h"], pl.ds(ln["r0"], c), :]
            ln["cum"] = _cumsum_rows(tri, jnp.log2(ln["f"]))
        for ln in lanes:
            _, _, _, mid, last, _ = directions[ln["d"]]
            cum = ln["cum"]
            cum_mid = cum[mid:mid + 1]
            cum_last = cum[last:last + 1]
            k = 1.0 - ln["f"]
            q = q_ref[0, ln["h"], pl.ds(ln["r0"], c), :].astype(F32)
            ln["v"] = v_ref[0, ln["h"], pl.ds(ln["r0"], c), :]
            q_mid = q * jnp.exp2(cum - cum_mid)
            k_mid = k * jnp.exp2(cum_mid - cum)
            ln["q_dec"] = (q_mid * jnp.exp2(cum_mid)).astype(BF16)
            k_state = (k_mid * jnp.exp2(cum_last - cum_mid)).astype(BF16)
            ln["scores"] = _dot_nt(q_mid.astype(BF16), k_mid.astype(BF16))
            ln["kv"] = _dot_tn(ln["v"], k_state)
            ln["decay"] = jnp.exp2(cum_last)
        new_states = []
        for idx, st in enumerate(states):
            for ln in lanes[idx * unroll:(idx + 1) * unroll]:
                ln["st"] = st.astype(BF16).T
                st = st * ln["decay"] + ln["kv"]
            new_states.append(st)
        for ln in lanes:
            mask, out_scr = directions[ln["d"]][2], directions[ln["d"]][5]
            p = jnp.where(mask, ln["scores"], 0.0).astype(BF16)
            lhs = jnp.concatenate([ln["q_dec"], p], axis=1)
            rhs = jnp.concatenate([ln["st"], ln["v"]], axis=0)
            out_scr[ln["slot"], pl.ds(ln["r0"], c), :] = _dot(lhs, rhs)
        return tuple(new_states)

    def head_group_body(hg, carry):
        hs = [hg * head_group + t for t in range(head_group)]
        states = []
        for h in hs:
            for d in range(2):
                if zero_init:
                    states.append(jnp.zeros((D_HEAD, D_HEAD), F32))
                else:
                    states.append(s0_ref[0, 0, d, h].T)
        states = tuple(states)
        n_trips = n_chunks // unroll
        if n_trips == 1:
            states = trip(hs, 0, states)
        else:
            states = lax.fori_loop(0, n_trips, functools.partial(trip, hs), states)

        eb = min(seq_len, ROW_TILE)
        for slot, h in enumerate(hs):
            def norm_body(j, carry2, slot=slot, h=h):
                r0 = pl.multiple_of(j * eb, eb)
                tot = of_scr[slot, pl.ds(r0, eb), :] + ob_scr[slot, pl.ds(r0, eb), :]
                gate = g_ref[0, h, pl.ds(r0, eb), :].astype(F32)
                o_ref[0, h, pl.ds(r0, eb), :] = (_rms(tot, gnw) * gate).astype(o_ref.dtype)
                return carry2

            if seq_len == eb:
                norm_body(0, 0)
            else:
                lax.fori_loop(0, seq_len // eb, norm_body, 0)
            if emit_state:
                sf_ref[0, 0, 0, h] = states[2 * slot].T
                sf_ref[0, 0, 1, h] = states[2 * slot + 1].T
        return carry

    if heads == head_group:
        head_group_body(0, 0)
    else:
        lax.fori_loop(0, heads // head_group, head_group_body, 0)


def _gla(q, ff, fb, v, g, s0, gnw, emit_state, heads, head_group, unroll):
    bsz, _, n, _ = q.shape
    zero_init = s0 is None
    seq_spec = pl.BlockSpec((1, heads, n, D_HEAD), lambda b, h: (b, h, 0, 0))
    state_spec = pl.BlockSpec((1, 1, 2, heads, D_HEAD, D_HEAD), lambda b, h: (b, 0, 0, h, 0, 0))
    in_specs = [seq_spec] * 5
    args = [q, ff, fb, v, g]
    if not zero_init:
        in_specs.append(state_spec)
        args.append(s0)
    in_specs.append(_const_spec((1, D_HEAD)))
    args.append(gnw)
    out_specs = [seq_spec]
    out_shape = [jax.ShapeDtypeStruct(q.shape, BF16)]
    if emit_state:
        out_specs.append(state_spec)
        out_shape.append(jax.ShapeDtypeStruct((bsz, 1, 2, N_HEADS, D_HEAD, D_HEAD), F32))
    outs = pl.pallas_call(
        functools.partial(_gla_kernel, seq_len=n, heads=heads, head_group=head_group, unroll=unroll,
                          zero_init=zero_init, emit_state=emit_state),
        grid=(bsz, N_HEADS // heads),
        in_specs=in_specs,
        out_specs=out_specs,
        out_shape=out_shape,
        scratch_shapes=[pltpu.VMEM((head_group, n, D_HEAD), F32)] * 2,
        compiler_params=_params(2),
        name="gla",
    )(*args)
    return outs if emit_state else (outs[0], None)


def _conv_ffn(x1, mod_ref, nfw_ref, wup_ref, cw_ref, cb_ref, wdn_ref, hid_scr, period):
    t = x1.shape[0]
    h = _modulate(x1, nfw_ref[...], mod_ref[0, 3:4, :], mod_ref[0, 4:5, :]).astype(BF16)
    pos = lax.broadcasted_iota(jnp.int32, (t, 1), 0) % period
    first = pos == 0
    last = pos == period - 1
    for j in range(D_FF // FF_CHUNK):
        lo = j * FF_CHUNK
        a = _dot(h, wup_ref[:, lo:lo + FF_CHUNK])
        g = _dot(h, wup_ref[:, D_FF + lo:D_FF + lo + FF_CHUNK])
        prev = jnp.where(first, 0.0, pltpu.roll(a, 1, 0))
        nxt = jnp.where(last, 0.0, pltpu.roll(a, t - 1, 0))
        ac = (prev * cw_ref[0:1, lo:lo + FF_CHUNK] + a * cw_ref[1:2, lo:lo + FF_CHUNK]
              + nxt * cw_ref[2:3, lo:lo + FF_CHUNK] + cb_ref[:, lo:lo + FF_CHUNK])
        hid_scr[:, lo:lo + FF_CHUNK] = (_silu(ac) * g).astype(BF16)
    return x1 + mod_ref[0, 5:6, :] * _dot(hid_scr[...], wdn_ref[...])


def _tile(n):
    return (1, TOKEN_TILE) if n >= TOKEN_TILE else (TOKEN_TILE // n, n)


def _rows(ref):
    v = ref[...]
    return v.reshape(v.shape[0] * v.shape[1], v.shape[2])


def _ffn_specs():
    return [
        _const_spec((1, D_MODEL)),
        _const_spec((D_MODEL, 2 * D_FF)),
        _const_spec((3, D_FF)),
        _const_spec((1, D_FF)),
        _const_spec((D_FF, D_MODEL)),
    ]


def _post0_kernel(x_ref, o_ref, mod_ref, wo_ref, nfw_ref, wup_ref, cw_ref, cb_ref, wdn_ref,
                  mod1_ref, nw1_ref, w1_ref, out_ref, gb_ref, cu_ref, hid_scr, *, period):
    o = jnp.concatenate(
        [jnp.concatenate([o_ref[b, hh] for hh in range(N_HEADS)], axis=1)
         for b in range(o_ref.shape[0])], axis=0)
    x1 = _rows(x_ref) + mod_ref[0, 2:3, :] * _dot(o, wo_ref[...])
    x2 = _conv_ffn(x1, mod_ref, nfw_ref, wup_ref, cw_ref, cb_ref, wdn_ref, hid_scr, period)
    out_ref[...] = x2.reshape(out_ref.shape)
    h = _modulate(x2, nw1_ref[...], mod1_ref[0, 0:1, :], mod1_ref[0, 1:2, :]).astype(BF16)
    gb_ref[...] = _dot(h, w1_ref[:, 0:D_MODEL]).astype(gb_ref.dtype).reshape(gb_ref.shape)
    gate_c = _dot(h, w1_ref[:, D_MODEL:2 * D_MODEL])
    u = _dot(h, w1_ref[:, 2 * D_MODEL:3 * D_MODEL])
    cu_ref[...] = (gate_c * u).astype(cu_ref.dtype).reshape(cu_ref.shape)


def _post0(x, o, mod, wo, nfw, wup, cw, cb, wdn, mod1, nw1, w1, per_seq_mod, period):
    bsz, n, _ = x.shape
    tb, tl = _tile(n)
    mod_map = (lambda b, i: (b, 0, 0)) if per_seq_mod else (lambda b, i: (0, 0, 0))
    tok_spec = pl.BlockSpec((tb, tl, D_MODEL), lambda b, i: (b, i, 0))
    return pl.pallas_call(
        functools.partial(_post0_kernel, period=period),
        grid=(bsz // tb, n // tl),
        in_specs=[
            tok_spec,
            pl.BlockSpec((tb, N_HEADS, tl, D_HEAD), lambda b, i: (b, 0, i, 0)),
            pl.BlockSpec((1, N_MOD, D_MODEL), mod_map),
            _const_spec((D_MODEL, D_MODEL)),
        ] + _ffn_specs() + [
            pl.BlockSpec((1, N_MOD, D_MODEL), mod_map),
            _const_spec((1, D_MODEL)),
            _const_spec(w1.shape),
        ],
        out_specs=[tok_spec] * 3,
        out_shape=[jax.ShapeDtypeStruct(x.shape, F32), jax.ShapeDtypeStruct(x.shape, BF16),
                   jax.ShapeDtypeStruct(x.shape, BF16)],
        scratch_shapes=[pltpu.VMEM((tb * tl, D_FF), BF16)],
        compiler_params=_params(2),
        name="post0",
    )(x, o, mod, wo, nfw, wup, cw, cb, wdn, mod1, nw1, w1)


def _post1_kernel(*refs, grid_rows, period):
    if grid_rows:
        (x_ref, gb_ref, cu_ref, cup_ref, cun_ref, mod_ref, scw_ref, scb_ref, wo_ref,
         nfw_ref, wup_ref, cw_ref, cb_ref, wdn_ref, fnw_ref, out_ref, hid_scr) = refs
    else:
        (x_ref, gb_ref, cu_ref, mod_ref, scw_ref, scb_ref, wo_ref,
         nfw_ref, wup_ref, cw_ref, cb_ref, wdn_ref, fnw_ref, out_ref, hid_scr) = refs
    cu = _rows(cu_ref).astype(F32)
    t = cu.shape[0]
    if grid_rows:
        i = pl.program_id(1)
        halo_p = jnp.where(i == 0, 0.0, cup_ref[0].astype(F32))
        halo_n = jnp.where(i == pl.num_programs(1) - 1, 0.0, cun_ref[0].astype(F32))
        prev = jnp.concatenate([halo_p, cu[:t - GRID_W]], axis=0)
        nxt = jnp.concatenate([cu[GRID_W:], halo_n], axis=0)
    else:
        pos = lax.broadcasted_iota(jnp.int32, (t, 1), 0) % period
        prev = jnp.where(pos == 0, 0.0, pltpu.roll(cu, 1, 0))
        nxt = jnp.where(pos == period - 1, 0.0, pltpu.roll(cu, t - 1, 0))
    z = prev * scw_ref[0:1, :] + cu * scw_ref[1:2, :] + nxt * scw_ref[2:3, :] + scb_ref[...]
    m = (_rows(gb_ref).astype(F32) * z).astype(BF16)
    x1 = _rows(x_ref) + mod_ref[0, 2:3, :] * _dot(m, wo_ref[...])
    x2 = _conv_ffn(x1, mod_ref, nfw_ref, wup_ref, cw_ref, cb_ref, wdn_ref, hid_scr, period)
    out_ref[...] = _rms(x2, fnw_ref[...]).reshape(out_ref.shape)


def _post1(x, gb, cu, mod, scw, scb, wo, nfw, wup, cw, cb, wdn, fnw, per_seq_mod, grid_rows, period):
    bsz, n, _ = x.shape
    tb, tl = _tile(n)
    mod_map = (lambda b, i: (b, 0, 0)) if per_seq_mod else (lambda b, i: (0, 0, 0))
    tok_spec = pl.BlockSpec((tb, tl, D_MODEL), lambda b, i: (b, i, 0))
    in_specs = [tok_spec, tok_spec, tok_spec]
    args = [x, gb, cu]
    if grid_rows:
        r = tl // GRID_W
        n_rows = n // GRID_W
        in_specs += [
            pl.BlockSpec((1, GRID_W, D_MODEL), lambda b, i: (b, jnp.maximum(i * r - 1, 0), 0)),
            pl.BlockSpec((1, GRID_W, D_MODEL), lambda b, i: (b, jnp.minimum((i + 1) * r, n_rows - 1), 0)),
        ]
        args += [cu, cu]
    in_specs += [
        pl.BlockSpec((1, N_MOD, D_MODEL), mod_map),
        _const_spec((3, D_MODEL)),
        _const_spec((1, D_MODEL)),
        _const_spec((D_MODEL, D_MODEL)),
    ] + _ffn_specs() + [_const_spec((1, D_MODEL))]
    args += [mod, scw, scb, wo, nfw, wup, cw, cb, wdn, fnw]
    return pl.pallas_call(
        functools.partial(_post1_kernel, grid_rows=grid_rows, period=period),
        grid=(bsz // tb, n // tl),
        in_specs=in_specs,
        out_specs=tok_spec,
        out_shape=jax.ShapeDtypeStruct(x.shape, F32),
        scratch_shapes=[pltpu.VMEM((tb * tl, D_FF), BF16)],
        compiler_params=_params(2),
        name="post1",
    )(*args)


def kernel(x_prompt, x_sample, state_hgrn, c, c_ctx, ada_w, ada_b, norm_mix_w, norm_ffn_w,
           hgrn_w_in, hgrn_lower_bounds, hgrn_gnorm_w, hgrn_w_out,
           sconv_w_in, sconv_conv_w, sconv_conv_b, sconv_w_out,
           ffn_w_up, ffn_conv_w, ffn_conv_b, ffn_w_down, final_norm_w):
    seq_p = x_prompt.shape[1]

    cvecs = jnp.concatenate([c_ctx[None, :], c], axis=0)
    mod = _ada(cvecs, ada_w, ada_b)
    mod = mod.reshape(mod.shape[0], 8, N_MOD, D_MODEL)
    mod_p = mod[:, 0:1]
    mod_s = mod[:, 1:1 + c.shape[0]]

    row = lambda a: a.reshape(1, -1)
    w_in0 = hgrn_w_in[0].astype(BF16)
    w_out0 = hgrn_w_out[0].astype(BF16)
    w_in1 = sconv_w_in[0].astype(BF16)
    w_out1 = sconv_w_out[0].astype(BF16)
    w_up = ffn_w_up.astype(BF16)
    w_dn = ffn_w_down.astype(BF16)
    gnw = row(hgrn_gnorm_w[0])
    lbp = hgrn_lower_bounds.reshape(-1, hgrn_lower_bounds.shape[-1])

    def layers(x, mods, s0, per_seq_mod, ffn_period, grid_rows, emit_state, gla_heads):
        q, ff, fb, v, g = _inproj0(x, mods[0], row(norm_mix_w[0]), w_in0, lbp, per_seq_mod)
        n_chunks = x.shape[1] // SCAN_CHUNK
        unroll = min(n_chunks, SCAN_LANES // 2)
        head_group = min(gla_heads, SCAN_LANES // (2 * unroll))
        o, s_fin = _gla(q, ff, fb, v, g, s0, gnw, emit_state, gla_heads, head_group, unroll)
        x, gb, cu = _post0(x, o, mods[0], w_out0, row(norm_ffn_w[0]), w_up[0], ffn_conv_w[0],
                           row(ffn_conv_b[0]), w_dn[0], mods[1], row(norm_mix_w[1]), w_in1,
                           per_seq_mod, ffn_period)
        y = _post1(x, gb, cu, mods[1], sconv_conv_w[0], row(sconv_conv_b[0]), w_out1,
                   row(norm_ffn_w[1]), w_up[1], ffn_conv_w[1], row(ffn_conv_b[1]), w_dn[1],
                   row(final_norm_w), per_seq_mod, grid_rows, ffn_period)
        return y, s_fin

    y_prompt, new_state = layers(x_prompt, mod_p, None, False, seq_p, False, True, N_HEADS)
    y_sample, _ = layers(x_sample, mod_s, state_hgrn, True, GRID_W, True, False, 1)
    return (y_prompt, y_sample, new_state.astype(state_hgrn.dtype))
```

```python
import functools

import jax
import jax.numpy as jnp
from jax import lax
from jax.experimental import pallas as pl
from jax.experimental.pallas import tpu as pltpu

D_MODEL = 1024
N_HEADS = 8
D_HEAD = 128
D_FF = 2816
N_MOD = 6
GRID_W = 64
EPS = 1e-6

LANES = 128
SCAN_CHUNK = 64
SCAN_LANES = 32
ROW_TILE = 256
TOKEN_TILE = 512
FF_CHUNK = 256
ADA_COLS = 1024
VMEM_LIMIT = 56 * 1024 * 1024

F32 = jnp.float32
BF16 = jnp.bfloat16


def _sigmoid(x):
    return 1.0 / (1.0 + jnp.exp(-x))


def _silu(x):
    return x * _sigmoid(x)


def _rms(x, w):
    return x * lax.rsqrt(jnp.mean(x * x, axis=-1, keepdims=True) + EPS) * w


def _modulate(x, w, shift, scale):
    return _rms(x, w) * (1.0 + scale) + shift


def _dot(a, b):
    return jnp.dot(a, b, preferred_element_type=F32)


def _dot_nt(a, b):
    return lax.dot_general(a, b, (((1,), (1,)), ((), ())), preferred_element_type=F32)


def _dot_tn(a, b):
    return lax.dot_general(a, b, (((0,), (0,)), ((), ())), preferred_element_type=F32)


def _params(n_grid):
    return pltpu.CompilerParams(
        dimension_semantics=("arbitrary",) * n_grid,
        vmem_limit_bytes=VMEM_LIMIT,
    )


def _const_spec(shape):
    nd = len(shape)
    return pl.BlockSpec(shape, lambda *_: (0,) * nd, pipeline_mode=pl.Buffered(1))


def _ada_kernel(c_ref, w_ref, b_ref, o_ref):
    cols = w_ref.shape[2]
    o_ref[0] = jnp.zeros(o_ref.shape[1:], F32)
    for r in range(c_ref.shape[0]):
        s = _silu(c_ref[r])
        parts = [
            jnp.sum(w_ref[0, :, j * LANES:(j + 1) * LANES] * s, axis=0, keepdims=True)
            for j in range(cols // LANES)
        ]
        o_ref[0, r:r + 1, :] = jnp.concatenate(parts, axis=1) + b_ref[0]


def _ada(cvecs, ada_w, ada_b):
    depth = ada_w.shape[0]
    n_out = ada_w.shape[2]
    cb = jnp.broadcast_to(cvecs[:, :, None], cvecs.shape + (LANES,))
    return pl.pallas_call(
        _ada_kernel,
        grid=(depth, n_out // ADA_COLS),
        in_specs=[
            pl.BlockSpec(cb.shape, lambda l, j: (0, 0, 0)),
            pl.BlockSpec((1, D_MODEL, ADA_COLS), lambda l, j: (l, 0, j)),
            pl.BlockSpec((1, 1, ADA_COLS), lambda l, j: (l, 0, j)),
        ],
        out_specs=pl.BlockSpec((1, 8, ADA_COLS), lambda l, j: (l, 0, j)),
        out_shape=jax.ShapeDtypeStruct((depth, 8, n_out), F32),
        compiler_params=_params(2),
        name="ada",
    )(cb, ada_w, ada_b.reshape(depth, 1, n_out))


def _inproj0_kernel(x_ref, mod_ref, nw_ref, w_ref, lbp_ref, q_ref, ff_ref, fb_ref, v_ref, g_ref):
    h = _modulate(_rows(x_ref), nw_ref[...], mod_ref[0, 0:1, :], mod_ref[0, 1:2, :]).astype(BF16)
    tb, _, tl, _ = q_ref.shape

    n_lb = lbp_ref.shape[0] // 2

    def lower_bound(d):
        rows = [lbp_ref[d * n_lb + i:d * n_lb + i + 1, :] for i in range(n_lb)]
        top = functools.reduce(jnp.maximum, rows)
        e = [jnp.exp(r - top) for r in rows]
        return e[0] / functools.reduce(jnp.add, e)

    lb = [lower_bound(0), lower_bound(1)]

    def proj(j):
        return _dot(h, w_ref[:, j * D_MODEL:(j + 1) * D_MODEL])

    def put(ref, val):
        for b in range(tb):
            for hh in range(N_HEADS):
                ref[b, hh] = val[b * tl:(b + 1) * tl, hh * D_HEAD:(hh + 1) * D_HEAD].astype(ref.dtype)

    put(q_ref, _silu(proj(0)))
    put(ff_ref, lb[0] + (1.0 - lb[0]) * _sigmoid(proj(1)))
    put(fb_ref, lb[1] + (1.0 - lb[1]) * _sigmoid(proj(2)))
    put(v_ref, proj(3))
    put(g_ref, _silu(proj(4)))


def _inproj0(x, mod, nw, w_in, lbp, per_seq_mod):
    bsz, n, _ = x.shape
    tb, tl = _tile(n)
    mod_map = (lambda b, i: (b, 0, 0)) if per_seq_mod else (lambda b, i: (0, 0, 0))
    head_spec = pl.BlockSpec((tb, N_HEADS, tl, D_HEAD), lambda b, i: (b, 0, i, 0))
    shp = (bsz, N_HEADS, n, D_HEAD)
    return pl.pallas_call(
        _inproj0_kernel,
        grid=(bsz // tb, n // tl),
        in_specs=[
            pl.BlockSpec((tb, tl, D_MODEL), lambda b, i: (b, i, 0)),
            pl.BlockSpec((1, N_MOD, D_MODEL), mod_map),
            _const_spec((1, D_MODEL)),
            _const_spec(w_in.shape),
            _const_spec(lbp.shape),
        ],
        out_specs=[head_spec] * 5,
        out_shape=[
            jax.ShapeDtypeStruct(shp, BF16),
            jax.ShapeDtypeStruct(shp, F32),
            jax.ShapeDtypeStruct(shp, F32),
            jax.ShapeDtypeStruct(shp, BF16),
            jax.ShapeDtypeStruct(shp, BF16),
        ],
        compiler_params=_params(2),
        name="inproj0",
    )(x, mod, nw, w_in, lbp)


def _cumsum_rows(tri, x):
    hi = x.astype(BF16)
    lo = (x - hi.astype(F32)).astype(BF16)
    s = _dot(tri, jnp.concatenate([hi, lo], axis=1))
    return s[:, :D_HEAD] + s[:, D_HEAD:]


def _gla_kernel(*refs, seq_len, heads, head_group, unroll, zero_init, emit_state, n_casts):
    q_ref, ff_ref, fb_ref, v_ref, g_ref = refs[:5]
    pos = 5
    s0_ref = None
    if not zero_init:
        s0_ref = refs[pos]
        pos += 1
    gnw_ref = refs[pos]
    cast_in = refs[pos + 1:pos + 1 + n_casts]
    pos += 1 + n_casts
    o_ref = refs[pos]
    pos += 1
    sf_ref = None
    if emit_state:
        sf_ref = refs[pos]
        pos += 1
    cast_out = refs[pos:pos + n_casts]
    pos += n_casts
    of_scr, ob_scr = refs[pos], refs[pos + 1]

    for src, dst in zip(cast_in, cast_out):
        dst[0] = src[0, 0].astype(dst.dtype)

    c = SCAN_CHUNK
    n_chunks = seq_len // c
    row = lax.broadcasted_iota(jnp.int32, (c, c), 0)
    col = lax.broadcasted_iota(jnp.int32, (c, c), 1)
    lower = col <= row
    upper = col >= row
    tri_lo = lower.astype(BF16)
    tri_up = upper.astype(BF16)
    gnw = gnw_ref[...]

    directions = (
        (ff_ref, tri_lo, lower, c // 2 - 1, c - 1, of_scr),
        (fb_ref, tri_up, upper, c // 2, 0, ob_scr),
    )

    def trip(hs, i, states):
        lanes = []
        for slot, h in enumerate(hs):
            for d, (f_ref, tri, mask, mid, last, out_scr) in enumerate(directions):
                for u in range(unroll):
                    n = i * unroll + u
                    if d == 1:
                        n = n_chunks - 1 - n
                    r0 = n * c if isinstance(n, int) else pl.multiple_of(n * c, c)
                    lanes.append(dict(slot=slot, h=h, d=d, r0=r0))
        for ln in lanes:
            f_ref, tri = directions[ln["d"]][:2]
            ln["f"] = f_ref[0, ln["h"], pl.ds(ln["r0"], c), :]
            ln["cum"] = _cumsum_rows(tri, jnp.log2(ln["f"]))
        for ln in lanes:
            _, _, _, mid, last, _ = directions[ln["d"]]
            cum = ln["cum"]
            cum_mid = cum[mid:mid + 1]
            cum_last = cum[last:last + 1]
            k = 1.0 - ln["f"]
            q = q_ref[0, ln["h"], pl.ds(ln["r0"], c), :].astype(F32)
            ln["v"] = v_ref[0, ln["h"], pl.ds(ln["r0"], c), :]
            q_mid = q * jnp.exp2(cum - cum_mid)
            k_mid = k * jnp.exp2(cum_mid - cum)
            ln["q_dec"] = (q_mid * jnp.exp2(cum_mid)).astype(BF16)
            k_state = (k_mid * jnp.exp2(cum_last - cum_mid)).astype(BF16)
            ln["scores"] = _dot_nt(q_mid.astype(BF16), k_mid.astype(BF16))
            ln["kv"] = _dot_tn(ln["v"], k_state)
            ln["decay"] = jnp.exp2(cum_last)
        new_states = []
        for idx, st in enumerate(states):
            for ln in lanes[idx * unroll:(idx + 1) * unroll]:
                ln["st"] = st.astype(BF16).T
                st = st * ln["decay"] + ln["kv"]
            new_states.append(st)
        for ln in lanes:
            mask, out_scr = directions[ln["d"]][2], directions[ln["d"]][5]
            p = jnp.where(mask, ln["scores"], 0.0).astype(BF16)
            lhs = jnp.concatenate([ln["q_dec"], p], axis=1)
            rhs = jnp.concatenate([ln["st"], ln["v"]], axis=0)
            out_scr[ln["slot"], pl.ds(ln["r0"], c), :] = _dot(lhs, rhs)
        return tuple(new_states)

    def head_group_body(hg, carry):
        hs = [hg * head_group + t for t in range(head_group)]
        states = []
        for h in hs:
            for d in range(2):
                if zero_init:
                    states.append(jnp.zeros((D_HEAD, D_HEAD), F32))
                else:
                    states.append(s0_ref[0, 0, d, h].T)
        states = tuple(states)
        n_trips = n_chunks // unroll
        if n_trips == 1:
            states = trip(hs, 0, states)
        else:
            states = lax.fori_loop(0, n_trips, functools.partial(trip, hs), states)

        eb = min(seq_len, ROW_TILE)
        for slot, h in enumerate(hs):
            def norm_body(j, carry2, slot=slot, h=h):
                r0 = pl.multiple_of(j * eb, eb)
                tot = of_scr[slot, pl.ds(r0, eb), :] + ob_scr[slot, pl.ds(r0, eb), :]
                gate = g_ref[0, h, pl.ds(r0, eb), :].astype(F32)
                o_ref[0, h, pl.ds(r0, eb), :] = (_rms(tot, gnw) * gate).astype(o_ref.dtype)
                return carry2

            if seq_len == eb:
                norm_body(0, 0)
            else:
                lax.fori_loop(0, seq_len // eb, norm_body, 0)
            if emit_state:
                sf_ref[0, 0, 0, h] = states[2 * slot].T
                sf_ref[0, 0, 1, h] = states[2 * slot + 1].T
        return carry

    if heads == head_group:
        head_group_body(0, 0)
    else:
        lax.fori_loop(0, heads // head_group, head_group_body, 0)


def _gla(q, ff, fb, v, g, s0, gnw, emit_state, heads, head_group, unroll, casts):
    bsz, _, n, _ = q.shape
    n_hb = N_HEADS // heads
    steps = bsz * n_hb
    zero_init = s0 is None
    seq_spec = pl.BlockSpec((1, heads, n, D_HEAD), lambda b, h: (b, h, 0, 0))
    state_spec = pl.BlockSpec((1, 1, 2, heads, D_HEAD, D_HEAD), lambda b, h: (b, 0, 0, h, 0, 0))
    in_specs = [seq_spec] * 5
    args = [q, ff, fb, v, g]
    if not zero_init:
        in_specs.append(state_spec)
        args.append(s0)
    in_specs.append(_const_spec((1, D_HEAD)))
    args.append(gnw)
    out_specs = [seq_spec]
    out_shape = [jax.ShapeDtypeStruct(q.shape, BF16)]
    if emit_state:
        out_specs.append(state_spec)
        out_shape.append(jax.ShapeDtypeStruct((bsz, 1, 2, N_HEADS, D_HEAD, D_HEAD), F32))
    for w, layer in casts:
        _, rows, cols = w.shape
        slab = rows // steps
        in_specs.append(pl.BlockSpec((1, 1, slab, cols),
                                     lambda b, h, layer=layer: (layer, b * n_hb + h, 0, 0)))
        args.append(w.reshape(w.shape[0], steps, slab, cols))
        out_specs.append(pl.BlockSpec((1, slab, cols), lambda b, h: (b * n_hb + h, 0, 0)))
        out_shape.append(jax.ShapeDtypeStruct((steps, slab, cols), BF16))
    outs = pl.pallas_call(
        functools.partial(_gla_kernel, seq_len=n, heads=heads, head_group=head_group, unroll=unroll,
                          zero_init=zero_init, emit_state=emit_state, n_casts=len(casts)),
        grid=(bsz, n_hb),
        in_specs=in_specs,
        out_specs=out_specs,
        out_shape=out_shape,
        scratch_shapes=[pltpu.VMEM((head_group, n, D_HEAD), F32)] * 2,
        compiler_params=_params(2),
        name="gla",
    )(*args)
    n_main = 2 if emit_state else 1
    cast_out = [o.reshape(w.shape[1], w.shape[2]) for o, (w, _) in zip(outs[n_main:], casts)]
    return outs[0], (outs[1] if emit_state else None), cast_out


def _conv_ffn(x1, mod_ref, nfw_ref, wup_ref, cw_ref, cb_ref, wdn_ref, hid_scr, period):
    t = x1.shape[0]
    h = _modulate(x1, nfw_ref[...], mod_ref[0, 3:4, :], mod_ref[0, 4:5, :]).astype(BF16)
    pos = lax.broadcasted_iota(jnp.int32, (t, 1), 0) % period
    first = pos == 0
    last = pos == period - 1
    for j in range(D_FF // FF_CHUNK):
        lo = j * FF_CHUNK
        a = _dot(h, wup_ref[:, lo:lo + FF_CHUNK])
        g = _dot(h, wup_ref[:, D_FF + lo:D_FF + lo + FF_CHUNK])
        prev = jnp.where(first, 0.0, pltpu.roll(a, 1, 0))
        nxt = jnp.where(last, 0.0, pltpu.roll(a, t - 1, 0))
        ac = (prev * cw_ref[0:1, lo:lo + FF_CHUNK] + a * cw_ref[1:2, lo:lo + FF_CHUNK]
              + nxt * cw_ref[2:3, lo:lo + FF_CHUNK] + cb_ref[:, lo:lo + FF_CHUNK])
        hid_scr[:, lo:lo + FF_CHUNK] = (_silu(ac) * g).astype(BF16)
    return x1 + mod_ref[0, 5:6, :] * _dot(hid_scr[...], wdn_ref[...])


def _tile(n):
    return (1, TOKEN_TILE) if n >= TOKEN_TILE else (TOKEN_TILE // n, n)


def _rows(ref):
    v = ref[...]
    return v.reshape(v.shape[0] * v.shape[1], v.shape[2])


def _ffn_specs():
    return [
        _const_spec((1, D_MODEL)),
        _const_spec((D_MODEL, 2 * D_FF)),
        _const_spec((3, D_FF)),
        _const_spec((1, D_FF)),
        _const_spec((D_FF, D_MODEL)),
    ]


def _post0_kernel(x_ref, o_ref, mod_ref, wo_ref, nfw_ref, wup_ref, cw_ref, cb_ref, wdn_ref,
                  mod1_ref, nw1_ref, w1_ref, out_ref, gb_ref, cu_ref, hid_scr, *, period):
    o = jnp.concatenate(
        [jnp.concatenate([o_ref[b, hh] for hh in range(N_HEADS)], axis=1)
         for b in range(o_ref.shape[0])], axis=0)
    x1 = _rows(x_ref) + mod_ref[0, 2:3, :] * _dot(o, wo_ref[...])
    x2 = _conv_ffn(x1, mod_ref, nfw_ref, wup_ref, cw_ref, cb_ref, wdn_ref, hid_scr, period)
    out_ref[...] = x2.reshape(out_ref.shape)
    h = _modulate(x2, nw1_ref[...], mod1_ref[0, 0:1, :], mod1_ref[0, 1:2, :]).astype(BF16)
    gb_ref[...] = _dot(h, w1_ref[:, 0:D_MODEL]).astype(gb_ref.dtype).reshape(gb_ref.shape)
    gate_c = _dot(h, w1_ref[:, D_MODEL:2 * D_MODEL])
    u = _dot(h, w1_ref[:, 2 * D_MODEL:3 * D_MODEL])
    cu_ref[...] = (gate_c * u).astype(cu_ref.dtype).reshape(cu_ref.shape)


def _post0(x, o, mod, wo, nfw, wup, cw, cb, wdn, mod1, nw1, w1, per_seq_mod, period):
    bsz, n, _ = x.shape
    tb, tl = _tile(n)
    mod_map = (lambda b, i: (b, 0, 0)) if per_seq_mod else (lambda b, i: (0, 0, 0))
    tok_spec = pl.BlockSpec((tb, tl, D_MODEL), lambda b, i: (b, i, 0))
    return pl.pallas_call(
        functools.partial(_post0_kernel, period=period),
        grid=(bsz // tb, n // tl),
        in_specs=[
            tok_spec,
            pl.BlockSpec((tb, N_HEADS, tl, D_HEAD), lambda b, i: (b, 0, i, 0)),
            pl.BlockSpec((1, N_MOD, D_MODEL), mod_map),
            _const_spec((D_MODEL, D_MODEL)),
        ] + _ffn_specs() + [
            pl.BlockSpec((1, N_MOD, D_MODEL), mod_map),
            _const_spec((1, D_MODEL)),
            _const_spec(w1.shape),
        ],
        out_specs=[tok_spec] * 3,
        out_shape=[jax.ShapeDtypeStruct(x.shape, F32), jax.ShapeDtypeStruct(x.shape, BF16),
                   jax.ShapeDtypeStruct(x.shape, BF16)],
        scratch_shapes=[pltpu.VMEM((tb * tl, D_FF), BF16)],
        compiler_params=_params(2),
        name="post0",
    )(x, o, mod, wo, nfw, wup, cw, cb, wdn, mod1, nw1, w1)


def _post1_kernel(*refs, grid_rows, period):
    if grid_rows:
        (x_ref, gb_ref, cu_ref, cup_ref, cun_ref, mod_ref, scw_ref, scb_ref, wo_ref,
         nfw_ref, wup_ref, cw_ref, cb_ref, wdn_ref, fnw_ref, out_ref, hid_scr) = refs
    else:
        (x_ref, gb_ref, cu_ref, mod_ref, scw_ref, scb_ref, wo_ref,
         nfw_ref, wup_ref, cw_ref, cb_ref, wdn_ref, fnw_ref, out_ref, hid_scr) = refs
    cu = _rows(cu_ref).astype(F32)
    t = cu.shape[0]
    if grid_rows:
        i = pl.program_id(1)
        halo_p = jnp.where(i == 0, 0.0, cup_ref[0].astype(F32))
        halo_n = jnp.where(i == pl.num_programs(1) - 1, 0.0, cun_ref[0].astype(F32))
        prev = jnp.concatenate([halo_p, cu[:t - GRID_W]], axis=0)
        nxt = jnp.concatenate([cu[GRID_W:], halo_n], axis=0)
    else:
        pos = lax.broadcasted_iota(jnp.int32, (t, 1), 0) % period
        prev = jnp.where(pos == 0, 0.0, pltpu.roll(cu, 1, 0))
        nxt = jnp.where(pos == period - 1, 0.0, pltpu.roll(cu, t - 1, 0))
    z = prev * scw_ref[0:1, :] + cu * scw_ref[1:2, :] + nxt * scw_ref[2:3, :] + scb_ref[...]
    m = (_rows(gb_ref).astype(F32) * z).astype(BF16)
    x1 = _rows(x_ref) + mod_ref[0, 2:3, :] * _dot(m, wo_ref[...])
    x2 = _conv_ffn(x1, mod_ref, nfw_ref, wup_ref, cw_ref, cb_ref, wdn_ref, hid_scr, period)
    out_ref[...] = _rms(x2, fnw_ref[...]).reshape(out_ref.shape)


def _post1(x, gb, cu, mod, scw, scb, wo, nfw, wup, cw, cb, wdn, fnw, per_seq_mod, grid_rows, period):
    bsz, n, _ = x.shape
    tb, tl = _tile(n)
    mod_map = (lambda b, i: (b, 0, 0)) if per_seq_mod else (lambda b, i: (0, 0, 0))
    tok_spec = pl.BlockSpec((tb, tl, D_MODEL), lambda b, i: (b, i, 0))
    in_specs = [tok_spec, tok_spec, tok_spec]
    args = [x, gb, cu]
    if grid_rows:
        r = tl // GRID_W
        n_rows = n // GRID_W
        in_specs += [
            pl.BlockSpec((1, GRID_W, D_MODEL), lambda b, i: (b, jnp.maximum(i * r - 1, 0), 0)),
            pl.BlockSpec((1, GRID_W, D_MODEL), lambda b, i: (b, jnp.minimum((i + 1) * r, n_rows - 1), 0)),
        ]
        args += [cu, cu]
    in_specs += [
        pl.BlockSpec((1, N_MOD, D_MODEL), mod_map),
        _const_spec((3, D_MODEL)),
        _const_spec((1, D_MODEL)),
        _const_spec((D_MODEL, D_MODEL)),
    ] + _ffn_specs() + [_const_spec((1, D_MODEL))]
    args += [mod, scw, scb, wo, nfw, wup, cw, cb, wdn, fnw]
    return pl.pallas_call(
        functools.partial(_post1_kernel, grid_rows=grid_rows, period=period),
        grid=(bsz // tb, n // tl),
        in_specs=in_specs,
        out_specs=tok_spec,
        out_shape=jax.ShapeDtypeStruct(x.shape, F32),
        scratch_shapes=[pltpu.VMEM((tb * tl, D_FF), BF16)],
        compiler_params=_params(2),
        name="post1",
    )(*args)


def kernel(x_prompt, x_sample, state_hgrn, c, c_ctx, ada_w, ada_b, norm_mix_w, norm_ffn_w,
           hgrn_w_in, hgrn_lower_bounds, hgrn_gnorm_w, hgrn_w_out,
           sconv_w_in, sconv_conv_w, sconv_conv_b, sconv_w_out,
           ffn_w_up, ffn_conv_w, ffn_conv_b, ffn_w_down, final_norm_w):
    seq_p = x_prompt.shape[1]

    cvecs = jnp.concatenate([c_ctx[None, :], c], axis=0)
    mod = _ada(cvecs, ada_w, ada_b)
    mod = mod.reshape(mod.shape[0], 8, N_MOD, D_MODEL)
    mod_p = mod[:, 0:1]
    mod_s = mod[:, 1:1 + c.shape[0]]

    row = lambda a: a.reshape(1, -1)
    w_in0 = hgrn_w_in[0].astype(BF16)
    gnw = row(hgrn_gnorm_w[0])
    lbp = hgrn_lower_bounds.reshape(-1, hgrn_lower_bounds.shape[-1])

    def mixer0(x, mods, s0, per_seq_mod, emit_state, gla_heads, casts):
        q, ff, fb, v, g = _inproj0(x, mods[0], row(norm_mix_w[0]), w_in0, lbp, per_seq_mod)
        n_chunks = x.shape[1] // SCAN_CHUNK
        unroll = min(n_chunks, SCAN_LANES // 2)
        head_group = min(gla_heads, SCAN_LANES // (2 * unroll))
        return _gla(q, ff, fb, v, g, s0, gnw, emit_state, gla_heads, head_group, unroll, casts)

    def post0(x, o, mods, per_seq_mod, ffn_period):
        return _post0(x, o, mods[0], w_out0, row(norm_ffn_w[0]), w_up0, ffn_conv_w[0],
                      row(ffn_conv_b[0]), w_dn0, mods[1], row(norm_mix_w[1]), w_in1,
                      per_seq_mod, ffn_period)

    def post1(x, gb, cu, mods, per_seq_mod, grid_rows, ffn_period):
        return _post1(x, gb, cu, mods[1], sconv_conv_w[0], row(sconv_conv_b[0]), w_out1,
                      row(norm_ffn_w[1]), w_up1, ffn_conv_w[1], row(ffn_conv_b[1]), w_dn1,
                      row(final_norm_w), per_seq_mod, grid_rows, ffn_period)

    o_p, new_state, (w_out0, w_up0, w_dn0, w_in1) = mixer0(
        x_prompt, mod_p, None, False, True, N_HEADS,
        [(hgrn_w_out, 0), (ffn_w_up, 0), (ffn_w_down, 0), (sconv_w_in, 0)])
    xp, gb_p, cu_p = post0(x_prompt, o_p, mod_p, False, seq_p)
    o_s, _, (w_out1, w_up1, w_dn1) = mixer0(
        x_sample, mod_s, state_hgrn, True, False, 1,
        [(sconv_w_out, 0), (ffn_w_up, 1), (ffn_w_down, 1)])
    y_prompt = post1(xp, gb_p, cu_p, mod_p, False, False, seq_p)
    xs, gb_s, cu_s = post0(x_sample, o_s, mod_s, True, GRID_W)
    y_sample = post1(xs, gb_s, cu_s, mod_s, True, True, GRID_W)
    return (y_prompt, y_sample, new_state.astype(state_hgrn.dtype))
```

```python
import functools

import jax
import jax.numpy as jnp
from jax import lax
from jax.experimental import pallas as pl
from jax.experimental.pallas import tpu as pltpu

D_MODEL = 1024
N_HEADS = 8
D_HEAD = 128
D_FF = 2816
N_MOD = 6
GRID_W = 64
EPS = 1e-6

LANES = 128
SCAN_CHUNK = 64
SCAN_LANES = 64
ROW_TILE = 256
TOKEN_TILE = 512
FF_CHUNK = 256
ADA_COLS = 1024
VMEM_LIMIT = 56 * 1024 * 1024

F32 = jnp.float32
BF16 = jnp.bfloat16


def _sigmoid(x):
    return 1.0 / (1.0 + jnp.exp(-x))


def _silu(x):
    return x * _sigmoid(x)


def _rms(x, w):
    return x * lax.rsqrt(jnp.mean(x * x, axis=-1, keepdims=True) + EPS) * w


def _modulate(x, w, shift, scale):
    return _rms(x, w) * (1.0 + scale) + shift


def _dot(a, b):
    return jnp.dot(a, b, preferred_element_type=F32)


def _dot_nt(a, b):
    return lax.dot_general(a, b, (((1,), (1,)), ((), ())), preferred_element_type=F32)


def _dot_tn(a, b):
    return lax.dot_general(a, b, (((0,), (0,)), ((), ())), preferred_element_type=F32)


def _params(n_grid):
    return pltpu.CompilerParams(
        dimension_semantics=("arbitrary",) * n_grid,
        vmem_limit_bytes=VMEM_LIMIT,
    )


def _const_spec(shape):
    nd = len(shape)
    return pl.BlockSpec(shape, lambda *_: (0,) * nd, pipeline_mode=pl.Buffered(1))


def _ada_kernel(c_ref, w_ref, b_ref, o_ref, s_scr):
    cols = w_ref.shape[2]

    @pl.when((pl.program_id(0) == 0) & (pl.program_id(1) == 0))
    def _():
        s_scr[...] = _silu(c_ref[...])

    o_ref[0] = jnp.zeros(o_ref.shape[1:], F32)
    for r in range(c_ref.shape[0]):
        s = s_scr[r]
        parts = [
            jnp.sum(w_ref[0, :, j * LANES:(j + 1) * LANES] * s, axis=0, keepdims=True)
            for j in range(cols // LANES)
        ]
        o_ref[0, r:r + 1, :] = jnp.concatenate(parts, axis=1) + b_ref[0]


def _ada(cvecs, ada_w, ada_b):
    depth = ada_w.shape[0]
    n_out = ada_w.shape[2]
    cb = jnp.broadcast_to(cvecs[:, :, None], cvecs.shape + (LANES,))
    return pl.pallas_call(
        _ada_kernel,
        grid=(depth, n_out // ADA_COLS),
        in_specs=[
            pl.BlockSpec(cb.shape, lambda l, j: (0, 0, 0)),
            pl.BlockSpec((1, D_MODEL, ADA_COLS), lambda l, j: (l, 0, j)),
            pl.BlockSpec((1, 1, ADA_COLS), lambda l, j: (l, 0, j)),
        ],
        out_specs=pl.BlockSpec((1, 8, ADA_COLS), lambda l, j: (l, 0, j)),
        out_shape=jax.ShapeDtypeStruct((depth, 8, n_out), F32),
        scratch_shapes=[pltpu.VMEM(cb.shape, F32)],
        compiler_params=_params(2),
        name="ada",
    )(cb, ada_w, ada_b.reshape(depth, 1, n_out))


def _inproj0_kernel(x_ref, mod_ref, nw_ref, w_ref, lbp_ref, q_ref, ff_ref, fb_ref, v_ref, g_ref):
    h = _modulate(_rows(x_ref), nw_ref[...], mod_ref[0, 0:1, :], mod_ref[0, 1:2, :]).astype(BF16)
    tb, _, tl, _ = q_ref.shape

    n_lb = lbp_ref.shape[0] // 2

    def lower_bound(d):
        rows = [lbp_ref[d * n_lb + i:d * n_lb + i + 1, :] for i in range(n_lb)]
        top = functools.reduce(jnp.maximum, rows)
        e = [jnp.exp(r - top) for r in rows]
        return e[0] / functools.reduce(jnp.add, e)

    lb = [lower_bound(0), lower_bound(1)]

    def proj(j):
        return _dot(h, w_ref[:, j * D_MODEL:(j + 1) * D_MODEL])

    def put(ref, val):
        for b in range(tb):
            for hh in range(N_HEADS):
                ref[b, hh] = val[b * tl:(b + 1) * tl, hh * D_HEAD:(hh + 1) * D_HEAD].astype(ref.dtype)

    put(q_ref, _silu(proj(0)))
    put(ff_ref, lb[0] + (1.0 - lb[0]) * _sigmoid(proj(1)))
    put(fb_ref, lb[1] + (1.0 - lb[1]) * _sigmoid(proj(2)))
    put(v_ref, proj(3))
    put(g_ref, _silu(proj(4)))


def _inproj0(x, mod, nw, w_in, lbp, per_seq_mod):
    bsz, n, _ = x.shape
    tb, tl = _tile(n)
    mod_map = (lambda b, i: (b, 0, 0)) if per_seq_mod else (lambda b, i: (0, 0, 0))
    head_spec = pl.BlockSpec((tb, N_HEADS, tl, D_HEAD), lambda b, i: (b, 0, i, 0))
    shp = (bsz, N_HEADS, n, D_HEAD)
    return pl.pallas_call(
        _inproj0_kernel,
        grid=(bsz // tb, n // tl),
        in_specs=[
            pl.BlockSpec((tb, tl, D_MODEL), lambda b, i: (b, i, 0)),
            pl.BlockSpec((1, N_MOD, D_MODEL), mod_map),
            _const_spec((1, D_MODEL)),
            _const_spec(w_in.shape),
            _const_spec(lbp.shape),
        ],
        out_specs=[head_spec] * 5,
        out_shape=[
            jax.ShapeDtypeStruct(shp, BF16),
            jax.ShapeDtypeStruct(shp, F32),
            jax.ShapeDtypeStruct(shp, F32),
            jax.ShapeDtypeStruct(shp, BF16),
            jax.ShapeDtypeStruct(shp, BF16),
        ],
        compiler_params=_params(2),
        name="inproj0",
    )(x, mod, nw, w_in, lbp)


def _cumsum_rows(tri, x):
    hi = x.astype(BF16)
    lo = (x - hi.astype(F32)).astype(BF16)
    s = _dot(tri, jnp.concatenate([hi, lo], axis=1))
    return s[:, :D_HEAD] + s[:, D_HEAD:]


def _gla_kernel(*refs, seq_len, heads, head_group, unroll, zero_init, emit_state, n_casts):
    q_ref, ff_ref, fb_ref, v_ref, g_ref = refs[:5]
    pos = 5
    s0_ref = None
    if not zero_init:
        s0_ref = refs[pos]
        pos += 1
    gnw_ref = refs[pos]
    cast_in = refs[pos + 1:pos + 1 + n_casts]
    pos += 1 + n_casts
    o_ref = refs[pos]
    pos += 1
    sf_ref = None
    if emit_state:
        sf_ref = refs[pos]
        pos += 1
    cast_out = refs[pos:pos + n_casts]
    pos += n_casts
    of_scr, ob_scr = refs[pos], refs[pos + 1]

    for src, dst in zip(cast_in, cast_out):
        dst[0] = src[0, 0].astype(dst.dtype)

    c = SCAN_CHUNK
    n_chunks = seq_len // c
    row = lax.broadcasted_iota(jnp.int32, (c, c), 0)
    col = lax.broadcasted_iota(jnp.int32, (c, c), 1)
    lower = col <= row
    upper = col >= row
    tri_lo = lower.astype(BF16)
    tri_up = upper.astype(BF16)
    gnw = gnw_ref[...]

    directions = (
        (ff_ref, tri_lo, lower, c // 2 - 1, c - 1, of_scr),
        (fb_ref, tri_up, upper, c // 2, 0, ob_scr),
    )

    def trip(hs, i, states):
        lanes = []
        for u in range(unroll):
            for slot, h in enumerate(hs):
                for d in range(2):
                    n = i * unroll + u
                    if d == 1:
                        n = n_chunks - 1 - n
                    r0 = n * c if isinstance(n, int) else pl.multiple_of(n * c, c)
                    lanes.append(dict(slot=slot, h=h, d=d, r0=r0, chain=2 * slot + d))
        states = list(states)

        def stage_a(ln):
            f_ref, tri = directions[ln["d"]][:2]
            ln["f"] = f_ref[0, ln["h"], pl.ds(ln["r0"], c), :]
            ln["cum"] = _cumsum_rows(tri, jnp.log2(ln["f"]))

        def stage_b(ln):
            _, _, _, mid, last, _ = directions[ln["d"]]
            cum = ln["cum"]
            cum_mid = cum[mid:mid + 1]
            cum_last = cum[last:last + 1]
            k = 1.0 - ln["f"]
            q = q_ref[0, ln["h"], pl.ds(ln["r0"], c), :].astype(F32)
            ln["v"] = v_ref[0, ln["h"], pl.ds(ln["r0"], c), :]
            q_mid = q * jnp.exp2(cum - cum_mid)
            k_mid = k * jnp.exp2(cum_mid - cum)
            ln["q_dec"] = (q_mid * jnp.exp2(cum_mid)).astype(BF16)
            k_state = (k_mid * jnp.exp2(cum_last - cum_mid)).astype(BF16)
            ln["scores"] = _dot_nt(q_mid.astype(BF16), k_mid.astype(BF16))
            ln["kv"] = _dot_tn(ln["v"], k_state)
            ln["decay"] = jnp.exp2(cum_last)

        def stage_c(ln):
            st = states[ln["chain"]]
            ln["st"] = st.astype(BF16).T
            states[ln["chain"]] = st * ln["decay"] + ln["kv"]

        def stage_d(ln):
            mask, out_scr = directions[ln["d"]][2], directions[ln["d"]][5]
            p = jnp.where(mask, ln["scores"], 0.0).astype(BF16)
            lhs = jnp.concatenate([ln["q_dec"], p], axis=1)
            rhs = jnp.concatenate([ln["st"], ln["v"]], axis=0)
            out_scr[ln["slot"], pl.ds(ln["r0"], c), :] = _dot(lhs, rhs)

        for stage in (stage_a, stage_b, stage_c, stage_d):
            for ln in lanes:
                stage(ln)
        return tuple(states)

    def head_group_body(hg, carry):
        hs = [hg * head_group + t for t in range(head_group)]
        states = []
        for h in hs:
            for d in range(2):
                if zero_init:
                    states.append(jnp.zeros((D_HEAD, D_HEAD), F32))
                else:
                    states.append(s0_ref[0, 0, d, h].T)
        states = tuple(states)
        n_trips = n_chunks // unroll
        if n_trips == 1:
            states = trip(hs, 0, states)
        else:
            states = lax.fori_loop(0, n_trips, functools.partial(trip, hs), states)

        eb = min(seq_len, ROW_TILE)
        for slot, h in enumerate(hs):
            def norm_body(j, carry2, slot=slot, h=h):
                r0 = pl.multiple_of(j * eb, eb)
                tot = of_scr[slot, pl.ds(r0, eb), :] + ob_scr[slot, pl.ds(r0, eb), :]
                gate = g_ref[0, h, pl.ds(r0, eb), :].astype(F32)
                o_ref[0, h, pl.ds(r0, eb), :] = (_rms(tot, gnw) * gate).astype(o_ref.dtype)
                return carry2

            if seq_len == eb:
                norm_body(0, 0)
            else:
                lax.fori_loop(0, seq_len // eb, norm_body, 0, unroll=4)
            if emit_state:
                sf_ref[0, 0, 0, h] = states[2 * slot].T
                sf_ref[0, 0, 1, h] = states[2 * slot + 1].T
        return carry

    if heads == head_group:
        head_group_body(0, 0)
    else:
        lax.fori_loop(0, heads // head_group, head_group_body, 0)


def _gla(q, ff, fb, v, g, s0, gnw, emit_state, heads, head_group, unroll, casts):
    bsz, _, n, _ = q.shape
    n_hb = N_HEADS // heads
    steps = bsz * n_hb
    zero_init = s0 is None
    seq_spec = pl.BlockSpec((1, heads, n, D_HEAD), lambda b, h: (b, h, 0, 0))
    state_spec = pl.BlockSpec((1, 1, 2, heads, D_HEAD, D_HEAD), lambda b, h: (b, 0, 0, h, 0, 0))
    in_specs = [seq_spec] * 5
    args = [q, ff, fb, v, g]
    if not zero_init:
        in_specs.append(state_spec)
        args.append(s0)
    in_specs.append(_const_spec((1, D_HEAD)))
    args.append(gnw)
    out_specs = [seq_spec]
    out_shape = [jax.ShapeDtypeStruct(q.shape, BF16)]
    if emit_state:
        out_specs.append(state_spec)
        out_shape.append(jax.ShapeDtypeStruct((bsz, 1, 2, N_HEADS, D_HEAD, D_HEAD), F32))
    for w, layer in casts:
        _, rows, cols = w.shape
        slab = rows // steps
        in_specs.append(pl.BlockSpec((1, 1, slab, cols),
                                     lambda b, h, layer=layer: (layer, b * n_hb + h, 0, 0)))
        args.append(w.reshape(w.shape[0], steps, slab, cols))
        out_specs.append(pl.BlockSpec((1, slab, cols), lambda b, h: (b * n_hb + h, 0, 0)))
        out_shape.append(jax.ShapeDtypeStruct((steps, slab, cols), BF16))
    outs = pl.pallas_call(
        functools.partial(_gla_kernel, seq_len=n, heads=heads, head_group=head_group, unroll=unroll,
                          zero_init=zero_init, emit_state=emit_state, n_casts=len(casts)),
        grid=(bsz, n_hb),
        in_specs=in_specs,
        out_specs=out_specs,
        out_shape=out_shape,
        scratch_shapes=[pltpu.VMEM((head_group, n, D_HEAD), F32)] * 2,
        compiler_params=_params(2),
        name="gla",
    )(*args)
    n_main = 2 if emit_state else 1
    cast_out = [o.reshape(w.shape[1], w.shape[2]) for o, (w, _) in zip(outs[n_main:], casts)]
    return outs[0], (outs[1] if emit_state else None), cast_out


def _conv_ffn(x1, mod_ref, nfw_ref, wup_ref, cw_ref, cb_ref, wdn_ref, hid_scr, period):
    t = x1.shape[0]
    h = _modulate(x1, nfw_ref[...], mod_ref[0, 3:4, :], mod_ref[0, 4:5, :]).astype(BF16)
    pos = lax.broadcasted_iota(jnp.int32, (t, 1), 0) % period
    first = pos == 0
    last = pos == period - 1
    for j in range(D_FF // FF_CHUNK):
        lo = j * FF_CHUNK
        a = _dot(h, wup_ref[:, lo:lo + FF_CHUNK])
        g = _dot(h, wup_ref[:, D_FF + lo:D_FF + lo + FF_CHUNK])
        prev = jnp.where(first, 0.0, pltpu.roll(a, 1, 0))
        nxt = jnp.where(last, 0.0, pltpu.roll(a, t - 1, 0))
        ac = (prev * cw_ref[0:1, lo:lo + FF_CHUNK] + a * cw_ref[1:2, lo:lo + FF_CHUNK]
              + nxt * cw_ref[2:3, lo:lo + FF_CHUNK] + cb_ref[:, lo:lo + FF_CHUNK])
        hid_scr[:, lo:lo + FF_CHUNK] = (_silu(ac) * g).astype(BF16)
    return x1 + mod_ref[0, 5:6, :] * _dot(hid_scr[...], wdn_ref[...])


def _tile(n):
    return (1, TOKEN_TILE) if n >= TOKEN_TILE else (TOKEN_TILE // n, n)


def _rows(ref):
    v = ref[...]
    return v.reshape(v.shape[0] * v.shape[1], v.shape[2])


def _ffn_specs():
    return [
        _const_spec((1, D_MODEL)),
        _const_spec((D_MODEL, 2 * D_FF)),
        _const_spec((3, D_FF)),
        _const_spec((1, D_FF)),
        _const_spec((D_FF, D_MODEL)),
    ]


def _post0_kernel(x_ref, o_ref, mod_ref, wo_ref, nfw_ref, wup_ref, cw_ref, cb_ref, wdn_ref,
                  mod1_ref, nw1_ref, w1_ref, out_ref, gb_ref, cu_ref, hid_scr, *, period):
    o = jnp.concatenate(
        [jnp.concatenate([o_ref[b, hh] for hh in range(N_HEADS)], axis=1)
         for b in range(o_ref.shape[0])], axis=0)
    x1 = _rows(x_ref) + mod_ref[0, 2:3, :] * _dot(o, wo_ref[...])
    x2 = _conv_ffn(x1, mod_ref, nfw_ref, wup_ref, cw_ref, cb_ref, wdn_ref, hid_scr, period)
    out_ref[...] = x2.reshape(out_ref.shape)
    h = _modulate(x2, nw1_ref[...], mod1_ref[0, 0:1, :], mod1_ref[0, 1:2, :]).astype(BF16)
    gb_ref[...] = _dot(h, w1_ref[:, 0:D_MODEL]).astype(gb_ref.dtype).reshape(gb_ref.shape)
    gate_c = _dot(h, w1_ref[:, D_MODEL:2 * D_MODEL])
    u = _dot(h, w1_ref[:, 2 * D_MODEL:3 * D_MODEL])
    cu_ref[...] = (gate_c * u).astype(cu_ref.dtype).reshape(cu_ref.shape)


def _post0(x, o, mod, wo, nfw, wup, cw, cb, wdn, mod1, nw1, w1, per_seq_mod, period):
    bsz, n, _ = x.shape
    tb, tl = _tile(n)
    mod_map = (lambda b, i: (b, 0, 0)) if per_seq_mod else (lambda b, i: (0, 0, 0))
    tok_spec = pl.BlockSpec((tb, tl, D_MODEL), lambda b, i: (b, i, 0))
    return pl.pallas_call(
        functools.partial(_post0_kernel, period=period),
        grid=(bsz // tb, n // tl),
        in_specs=[
            tok_spec,
            pl.BlockSpec((tb, N_HEADS, tl, D_HEAD), lambda b, i: (b, 0, i, 0)),
            pl.BlockSpec((1, N_MOD, D_MODEL), mod_map),
            _const_spec((D_MODEL, D_MODEL)),
        ] + _ffn_specs() + [
            pl.BlockSpec((1, N_MOD, D_MODEL), mod_map),
            _const_spec((1, D_MODEL)),
            _const_spec(w1.shape),
        ],
        out_specs=[tok_spec] * 3,
        out_shape=[jax.ShapeDtypeStruct(x.shape, F32), jax.ShapeDtypeStruct(x.shape, BF16),
                   jax.ShapeDtypeStruct(x.shape, BF16)],
        scratch_shapes=[pltpu.VMEM((tb * tl, D_FF), BF16)],
        compiler_params=_params(2),
        name="post0",
    )(x, o, mod, wo, nfw, wup, cw, cb, wdn, mod1, nw1, w1)


def _post1_kernel(*refs, grid_rows, period):
    if grid_rows:
        (x_ref, gb_ref, cu_ref, cup_ref, cun_ref, mod_ref, scw_ref, scb_ref, wo_ref,
         nfw_ref, wup_ref, cw_ref, cb_ref, wdn_ref, fnw_ref, out_ref, hid_scr) = refs
    else:
        (x_ref, gb_ref, cu_ref, mod_ref, scw_ref, scb_ref, wo_ref,
         nfw_ref, wup_ref, cw_ref, cb_ref, wdn_ref, fnw_ref, out_ref, hid_scr) = refs
    cu = _rows(cu_ref).astype(F32)
    t = cu.shape[0]
    if grid_rows:
        i = pl.program_id(1)
        halo_p = jnp.where(i == 0, 0.0, cup_ref[0].astype(F32))
        halo_n = jnp.where(i == pl.num_programs(1) - 1, 0.0, cun_ref[0].astype(F32))
        prev = jnp.concatenate([halo_p, cu[:t - GRID_W]], axis=0)
        nxt = jnp.concatenate([cu[GRID_W:], halo_n], axis=0)
    else:
        pos = lax.broadcasted_iota(jnp.int32, (t, 1), 0) % period
        prev = jnp.where(pos == 0, 0.0, pltpu.roll(cu, 1, 0))
        nxt = jnp.where(pos == period - 1, 0.0, pltpu.roll(cu, t - 1, 0))
    z = prev * scw_ref[0:1, :] + cu * scw_ref[1:2, :] + nxt * scw_ref[2:3, :] + scb_ref[...]
    m = (_rows(gb_ref).astype(F32) * z).astype(BF16)
    x1 = _rows(x_ref) + mod_ref[0, 2:3, :] * _dot(m, wo_ref[...])
    x2 = _conv_ffn(x1, mod_ref, nfw_ref, wup_ref, cw_ref, cb_ref, wdn_ref, hid_scr, period)
    out_ref[...] = _rms(x2, fnw_ref[...]).reshape(out_ref.shape)


def _post1(x, gb, cu, mod, scw, scb, wo, nfw, wup, cw, cb, wdn, fnw, per_seq_mod, grid_rows, period):
    bsz, n, _ = x.shape
    tb, tl = _tile(n)
    mod_map = (lambda b, i: (b, 0, 0)) if per_seq_mod else (lambda b, i: (0, 0, 0))
    tok_spec = pl.BlockSpec((tb, tl, D_MODEL), lambda b, i: (b, i, 0))
    in_specs = [tok_spec, tok_spec, tok_spec]
    args = [x, gb, cu]
    if grid_rows:
        r = tl // GRID_W
        n_rows = n // GRID_W
        in_specs += [
            pl.BlockSpec((1, GRID_W, D_MODEL), lambda b, i: (b, jnp.maximum(i * r - 1, 0), 0)),
            pl.BlockSpec((1, GRID_W, D_MODEL), lambda b, i: (b, jnp.minimum((i + 1) * r, n_rows - 1), 0)),
        ]
        args += [cu, cu]
    in_specs += [
        pl.BlockSpec((1, N_MOD, D_MODEL), mod_map),
        _const_spec((3, D_MODEL)),
        _const_spec((1, D_MODEL)),
        _const_spec((D_MODEL, D_MODEL)),
    ] + _ffn_specs() + [_const_spec((1, D_MODEL))]
    args += [mod, scw, scb, wo, nfw, wup, cw, cb, wdn, fnw]
    return pl.pallas_call(
        functools.partial(_post1_kernel, grid_rows=grid_rows, period=period),
        grid=(bsz // tb, n // tl),
        in_specs=in_specs,
        out_specs=tok_spec,
        out_shape=jax.ShapeDtypeStruct(x.shape, F32),
        scratch_shapes=[pltpu.VMEM((tb * tl, D_FF), BF16)],
        compiler_params=_params(2),
        name="post1",
    )(*args)


def kernel(x_prompt, x_sample, state_hgrn, c, c_ctx, ada_w, ada_b, norm_mix_w, norm_ffn_w,
           hgrn_w_in, hgrn_lower_bounds, hgrn_gnorm_w, hgrn_w_out,
           sconv_w_in, sconv_conv_w, sconv_conv_b, sconv_w_out,
           ffn_w_up, ffn_conv_w, ffn_conv_b, ffn_w_down, final_norm_w):
    seq_p = x_prompt.shape[1]

    cvecs = jnp.concatenate([c_ctx[None, :], c], axis=0)
    mod = _ada(cvecs, ada_w, ada_b)
    mod = mod.reshape(mod.shape[0], 8, N_MOD, D_MODEL)
    mod_p = mod[:, 0:1]
    mod_s = mod[:, 1:1 + c.shape[0]]

    row = lambda a: a.reshape(1, -1)
    w_in0 = hgrn_w_in[0].astype(BF16)
    gnw = row(hgrn_gnorm_w[0])
    lbp = hgrn_lower_bounds.reshape(-1, hgrn_lower_bounds.shape[-1])

    def mixer0(x, mods, s0, per_seq_mod, emit_state, gla_heads, casts):
        q, ff, fb, v, g = _inproj0(x, mods[0], row(norm_mix_w[0]), w_in0, lbp, per_seq_mod)
        n_chunks = x.shape[1] // SCAN_CHUNK
        unroll = min(n_chunks, SCAN_LANES // 2)
        head_group = min(gla_heads, SCAN_LANES // (2 * unroll))
        return _gla(q, ff, fb, v, g, s0, gnw, emit_state, gla_heads, head_group, unroll, casts)

    def post0(x, o, mods, per_seq_mod, ffn_period):
        return _post0(x, o, mods[0], w_out0, row(norm_ffn_w[0]), w_up0, ffn_conv_w[0],
                      row(ffn_conv_b[0]), w_dn0, mods[1], row(norm_mix_w[1]), w_in1,
                      per_seq_mod, ffn_period)

    def post1(x, gb, cu, mods, per_seq_mod, grid_rows, ffn_period):
        return _post1(x, gb, cu, mods[1], sconv_conv_w[0], row(sconv_conv_b[0]), w_out1,
                      row(norm_ffn_w[1]), w_up1, ffn_conv_w[1], row(ffn_conv_b[1]), w_dn1,
                      row(final_norm_w), per_seq_mod, grid_rows, ffn_period)

    o_p, new_state, (w_out0, w_up0, w_dn0, w_in1) = mixer0(
        x_prompt, mod_p, None, False, True, N_HEADS,
        [(hgrn_w_out, 0), (ffn_w_up, 0), (ffn_w_down, 0), (sconv_w_in, 0)])
    xp, gb_p, cu_p = post0(x_prompt, o_p, mod_p, False, seq_p)
    o_s, _, (w_out1, w_up1, w_dn1) = mixer0(
        x_sample, mod_s, state_hgrn, True, False, 1,
        [(sconv_w_out, 0), (ffn_w_up, 1), (ffn_w_down, 1)])
    y_prompt = post1(xp, gb_p, cu_p, mod_p, False, False, seq_p)
    xs, gb_s, cu_s = post0(x_sample, o_s, mod_s, True, GRID_W)
    y_sample = post1(xs, gb_s, cu_s, mod_s, True, True, GRID_W)
    return (y_prompt, y_sample, new_state.astype(state_hgrn.dtype))
```

```python
import functools

import jax
import jax.numpy as jnp
from jax import lax
from jax.experimental import pallas as pl
from jax.experimental.pallas import tpu as pltpu

D_MODEL = 1024
N_HEADS = 8
D_HEAD = 128
D_FF = 2816
N_MOD = 6
GRID_W = 64
EPS = 1e-6

LANES = 128
SCAN_CHUNK = 64
SCAN_LANES = 64
ROW_TILE = 256
TOKEN_TILE = 512
FF_CHUNK = 256
ADA_ROWS = 256
VMEM_LIMIT = 56 * 1024 * 1024

F32 = jnp.float32
BF16 = jnp.bfloat16


def _sigmoid(x):
    return 1.0 / (1.0 + jnp.exp(-x))


def _silu(x):
    return x * _sigmoid(x)


def _rms(x, w):
    return x * lax.rsqrt(jnp.mean(x * x, axis=-1, keepdims=True) + EPS) * w


def _modulate(x, w, shift, scale):
    return _rms(x, w) * (1.0 + scale) + shift


def _dot(a, b):
    return jnp.dot(a, b, preferred_element_type=F32)


def _dot_nt(a, b):
    return lax.dot_general(a, b, (((1,), (1,)), ((), ())), preferred_element_type=F32)


def _dot_tn(a, b):
    return lax.dot_general(a, b, (((0,), (0,)), ((), ())), preferred_element_type=F32)


def _params(n_grid):
    return pltpu.CompilerParams(
        dimension_semantics=("arbitrary",) * n_grid,
        vmem_limit_bytes=VMEM_LIMIT,
    )


def _const_spec(shape):
    nd = len(shape)
    return pl.BlockSpec(shape, lambda *_: (0,) * nd, pipeline_mode=pl.Buffered(1))


def _ada_kernel(c_ref, w_ref, b_ref, o_ref, s_scr):
    k = pl.program_id(1)
    rows, n_out = w_ref.shape[1], w_ref.shape[2]

    @pl.when((pl.program_id(0) == 0) & (k == 0))
    def _():
        s_scr[...] = _silu(c_ref[...])

    @pl.when(k == 0)
    def _():
        o_ref[0] = jnp.broadcast_to(b_ref[0], o_ref.shape[1:])

    r0 = pl.multiple_of(k * rows, rows)
    for r in range(c_ref.shape[0]):
        s = s_scr[r, pl.ds(r0, rows), :]
        parts = [
            jnp.sum(w_ref[0, :, j * LANES:(j + 1) * LANES] * s, axis=0, keepdims=True)
            for j in range(n_out // LANES)
        ]
        o_ref[0, r:r + 1, :] += jnp.concatenate(parts, axis=1)


def _ada(cvecs, ada_w, ada_b):
    depth, d_in, n_out = ada_w.shape
    cb = jnp.broadcast_to(cvecs[:, :, None], cvecs.shape + (LANES,))
    return pl.pallas_call(
        _ada_kernel,
        grid=(depth, d_in // ADA_ROWS),
        in_specs=[
            pl.BlockSpec(cb.shape, lambda l, k: (0, 0, 0)),
            pl.BlockSpec((1, ADA_ROWS, n_out), lambda l, k: (l, k, 0)),
            pl.BlockSpec((1, 1, n_out), lambda l, k: (l, 0, 0)),
        ],
        out_specs=pl.BlockSpec((1, 8, n_out), lambda l, k: (l, 0, 0)),
        out_shape=jax.ShapeDtypeStruct((depth, 8, n_out), F32),
        scratch_shapes=[pltpu.VMEM(cb.shape, F32)],
        compiler_params=_params(2),
        name="ada",
    )(cb, ada_w, ada_b.reshape(depth, 1, n_out))


def _inproj0_kernel(x_ref, mod_ref, nw_ref, w_ref, lbp_ref, q_ref, ff_ref, fb_ref, v_ref, g_ref):
    h = _modulate(_rows(x_ref), nw_ref[...], mod_ref[0, 0:1, :], mod_ref[0, 1:2, :]).astype(BF16)
    tb, _, tl, _ = q_ref.shape

    n_lb = lbp_ref.shape[0] // 2

    def lower_bound(d):
        rows = [lbp_ref[d * n_lb + i:d * n_lb + i + 1, :] for i in range(n_lb)]
        top = functools.reduce(jnp.maximum, rows)
        e = [jnp.exp(r - top) for r in rows]
        return e[0] / functools.reduce(jnp.add, e)

    lb = [lower_bound(0), lower_bound(1)]

    def proj(j):
        return _dot(h, w_ref[:, j * D_MODEL:(j + 1) * D_MODEL])

    def put(ref, val):
        for b in range(tb):
            for hh in range(N_HEADS):
                ref[b, hh] = val[b * tl:(b + 1) * tl, hh * D_HEAD:(hh + 1) * D_HEAD].astype(ref.dtype)

    put(q_ref, _silu(proj(0)))
    put(ff_ref, lb[0] + (1.0 - lb[0]) * _sigmoid(proj(1)))
    put(fb_ref, lb[1] + (1.0 - lb[1]) * _sigmoid(proj(2)))
    put(v_ref, proj(3))
    put(g_ref, _silu(proj(4)))


def _inproj0(x, mod, nw, w_in, lbp, per_seq_mod):
    bsz, n, _ = x.shape
    tb, tl = _tile(n)
    mod_map = (lambda b, i: (b, 0, 0)) if per_seq_mod else (lambda b, i: (0, 0, 0))
    head_spec = pl.BlockSpec((tb, N_HEADS, tl, D_HEAD), lambda b, i: (b, 0, i, 0))
    shp = (bsz, N_HEADS, n, D_HEAD)
    return pl.pallas_call(
        _inproj0_kernel,
        grid=(bsz // tb, n // tl),
        in_specs=[
            pl.BlockSpec((tb, tl, D_MODEL), lambda b, i: (b, i, 0)),
            pl.BlockSpec((1, N_MOD, D_MODEL), mod_map),
            _const_spec((1, D_MODEL)),
            _const_spec(w_in.shape),
            _const_spec(lbp.shape),
        ],
        out_specs=[head_spec] * 5,
        out_shape=[
            jax.ShapeDtypeStruct(shp, BF16),
            jax.ShapeDtypeStruct(shp, F32),
            jax.ShapeDtypeStruct(shp, F32),
            jax.ShapeDtypeStruct(shp, BF16),
            jax.ShapeDtypeStruct(shp, BF16),
        ],
        compiler_params=_params(2),
        name="inproj0",
    )(x, mod, nw, w_in, lbp)


def _cumsum_rows(tri, x):
    hi = x.astype(BF16)
    lo = (x - hi.astype(F32)).astype(BF16)
    s = _dot(tri, jnp.concatenate([hi, lo], axis=1))
    return s[:, :D_HEAD] + s[:, D_HEAD:]


def _gla_kernel(*refs, seq_len, heads, head_group, unroll, zero_init, emit_state, n_casts):
    q_ref, ff_ref, fb_ref, v_ref, g_ref = refs[:5]
    pos = 5
    s0_ref = None
    if not zero_init:
        s0_ref = refs[pos]
        pos += 1
    gnw_ref = refs[pos]
    cast_in = refs[pos + 1:pos + 1 + n_casts]
    pos += 1 + n_casts
    o_ref = refs[pos]
    pos += 1
    sf_ref = None
    if emit_state:
        sf_ref = refs[pos]
        pos += 1
    cast_out = refs[pos:pos + n_casts]
    pos += n_casts
    of_scr, ob_scr = refs[pos], refs[pos + 1]

    for src, dst in zip(cast_in, cast_out):
        dst[0] = src[0, 0].astype(dst.dtype)

    c = SCAN_CHUNK
    n_chunks = seq_len // c
    row = lax.broadcasted_iota(jnp.int32, (c, c), 0)
    col = lax.broadcasted_iota(jnp.int32, (c, c), 1)
    lower = col <= row
    upper = col >= row
    tri_lo = lower.astype(BF16)
    tri_up = upper.astype(BF16)
    gnw = gnw_ref[...]

    directions = (
        (ff_ref, tri_lo, lower, c // 2 - 1, c - 1, of_scr),
        (fb_ref, tri_up, upper, c // 2, 0, ob_scr),
    )

    def trip(hs, i, states):
        lanes = []
        for u in range(unroll):
            for slot, h in enumerate(hs):
                for d in range(2):
                    n = i * unroll + u
                    if d == 1:
                        n = n_chunks - 1 - n
                    r0 = n * c if isinstance(n, int) else pl.multiple_of(n * c, c)
                    lanes.append(dict(slot=slot, h=h, d=d, r0=r0, chain=2 * slot + d))
        states = list(states)

        def stage_a(ln):
            f_ref, tri = directions[ln["d"]][:2]
            ln["f"] = f_ref[0, ln["h"], pl.ds(ln["r0"], c), :]
            ln["cum"] = _cumsum_rows(tri, jnp.log2(ln["f"]))

        def stage_b(ln):
            _, _, _, mid, last, _ = directions[ln["d"]]
            cum = ln["cum"]
            cum_mid = cum[mid:mid + 1]
            cum_last = cum[last:last + 1]
            k = 1.0 - ln["f"]
            q = q_ref[0, ln["h"], pl.ds(ln["r0"], c), :].astype(F32)
            ln["v"] = v_ref[0, ln["h"], pl.ds(ln["r0"], c), :]
            q_mid = q * jnp.exp2(cum - cum_mid)
            k_mid = k * jnp.exp2(cum_mid - cum)
            ln["q_dec"] = (q_mid * jnp.exp2(cum_mid)).astype(BF16)
            k_state = (k_mid * jnp.exp2(cum_last - cum_mid)).astype(BF16)
            ln["scores"] = _dot_nt(q_mid.astype(BF16), k_mid.astype(BF16))
            ln["kv"] = _dot_tn(ln["v"], k_state)
            ln["decay"] = jnp.exp2(cum_last)

        def stage_c(ln):
            st = states[ln["chain"]]
            ln["st"] = st.astype(BF16).T
            states[ln["chain"]] = st * ln["decay"] + ln["kv"]

        def stage_d(ln):
            mask, out_scr = directions[ln["d"]][2], directions[ln["d"]][5]
            p = jnp.where(mask, ln["scores"], 0.0).astype(BF16)
            lhs = jnp.concatenate([ln["q_dec"], p], axis=1)
            rhs = jnp.concatenate([ln["st"], ln["v"]], axis=0)
            out_scr[ln["slot"], pl.ds(ln["r0"], c), :] = _dot(lhs, rhs)

        for stage in (stage_a, stage_b, stage_c, stage_d):
            for ln in lanes:
                stage(ln)
        return tuple(states)

    def head_group_body(hg, carry):
        hs = [hg * head_group + t for t in range(head_group)]
        states = []
        for h in hs:
            for d in range(2):
                if zero_init:
                    states.append(jnp.zeros((D_HEAD, D_HEAD), F32))
                else:
                    states.append(s0_ref[0, 0, d, h].T)
        states = tuple(states)
        n_trips = n_chunks // unroll
        if n_trips == 1:
            states = trip(hs, 0, states)
        else:
            states = lax.fori_loop(0, n_trips, functools.partial(trip, hs), states)

        eb = min(seq_len, ROW_TILE)
        for slot, h in enumerate(hs):
            def norm_body(j, carry2, slot=slot, h=h):
                r0 = pl.multiple_of(j * eb, eb)
                tot = of_scr[slot, pl.ds(r0, eb), :] + ob_scr[slot, pl.ds(r0, eb), :]
                gate = g_ref[0, h, pl.ds(r0, eb), :].astype(F32)
                o_ref[0, h, pl.ds(r0, eb), :] = (_rms(tot, gnw) * gate).astype(o_ref.dtype)
                return carry2

            if seq_len == eb:
                norm_body(0, 0)
            else:
                lax.fori_loop(0, seq_len // eb, norm_body, 0, unroll=4)
            if emit_state:
                sf_ref[0, 0, 0, h] = states[2 * slot].T
                sf_ref[0, 0, 1, h] = states[2 * slot + 1].T
        return carry

    if heads == head_group:
        head_group_body(0, 0)
    else:
        lax.fori_loop(0, heads // head_group, head_group_body, 0)


def _gla(q, ff, fb, v, g, s0, gnw, emit_state, heads, head_group, unroll, casts):
    bsz, _, n, _ = q.shape
    n_hb = N_HEADS // heads
    steps = bsz * n_hb
    zero_init = s0 is None
    seq_spec = pl.BlockSpec((1, heads, n, D_HEAD), lambda b, h: (b, h, 0, 0))
    state_spec = pl.BlockSpec((1, 1, 2, heads, D_HEAD, D_HEAD), lambda b, h: (b, 0, 0, h, 0, 0))
    in_specs = [seq_spec] * 5
    args = [q, ff, fb, v, g]
    if not zero_init:
        in_specs.append(state_spec)
        args.append(s0)
    in_specs.append(_const_spec((1, D_HEAD)))
    args.append(gnw)
    out_specs = [seq_spec]
    out_shape = [jax.ShapeDtypeStruct(q.shape, BF16)]
    if emit_state:
        out_specs.append(state_spec)
        out_shape.append(jax.ShapeDtypeStruct((bsz, 1, 2, N_HEADS, D_HEAD, D_HEAD), F32))
    for w, layer in casts:
        _, rows, cols = w.shape
        slab = rows // steps
        in_specs.append(pl.BlockSpec((1, 1, slab, cols),
                                     lambda b, h, layer=layer: (layer, b * n_hb + h, 0, 0)))
        args.append(w.reshape(w.shape[0], steps, slab, cols))
        out_specs.append(pl.BlockSpec((1, slab, cols), lambda b, h: (b * n_hb + h, 0, 0)))
        out_shape.append(jax.ShapeDtypeStruct((steps, slab, cols), BF16))
    outs = pl.pallas_call(
        functools.partial(_gla_kernel, seq_len=n, heads=heads, head_group=head_group, unroll=unroll,
                          zero_init=zero_init, emit_state=emit_state, n_casts=len(casts)),
        grid=(bsz, n_hb),
        in_specs=in_specs,
        out_specs=out_specs,
        out_shape=out_shape,
        scratch_shapes=[pltpu.VMEM((head_group, n, D_HEAD), F32)] * 2,
        compiler_params=_params(2),
        name="gla",
    )(*args)
    n_main = 2 if emit_state else 1
    cast_out = [o.reshape(w.shape[1], w.shape[2]) for o, (w, _) in zip(outs[n_main:], casts)]
    return outs[0], (outs[1] if emit_state else None), cast_out


def _conv_ffn(x1, mod_ref, nfw_ref, wup_ref, cw_ref, cb_ref, wdn_ref, hid_scr, period):
    t = x1.shape[0]
    h = _modulate(x1, nfw_ref[...], mod_ref[0, 3:4, :], mod_ref[0, 4:5, :]).astype(BF16)
    pos = lax.broadcasted_iota(jnp.int32, (t, 1), 0) % period
    first = pos == 0
    last = pos == period - 1
    for j in range(D_FF // FF_CHUNK):
        lo = j * FF_CHUNK
        a = _dot(h, wup_ref[:, lo:lo + FF_CHUNK])
        g = _dot(h, wup_ref[:, D_FF + lo:D_FF + lo + FF_CHUNK])
        prev = jnp.where(first, 0.0, pltpu.roll(a, 1, 0))
        nxt = jnp.where(last, 0.0, pltpu.roll(a, t - 1, 0))
        ac = (prev * cw_ref[0:1, lo:lo + FF_CHUNK] + a * cw_ref[1:2, lo:lo + FF_CHUNK]
              + nxt * cw_ref[2:3, lo:lo + FF_CHUNK] + cb_ref[:, lo:lo + FF_CHUNK])
        hid_scr[:, lo:lo + FF_CHUNK] = (_silu(ac) * g).astype(BF16)
    return x1 + mod_ref[0, 5:6, :] * _dot(hid_scr[...], wdn_ref[...])


def _tile(n):
    return (1, TOKEN_TILE) if n >= TOKEN_TILE else (TOKEN_TILE // n, n)


def _rows(ref):
    v = ref[...]
    return v.reshape(v.shape[0] * v.shape[1], v.shape[2])


def _ffn_specs():
    return [
        _const_spec((1, D_MODEL)),
        _const_spec((D_MODEL, 2 * D_FF)),
        _const_spec((3, D_FF)),
        _const_spec((1, D_FF)),
        _const_spec((D_FF, D_MODEL)),
    ]


def _post0_kernel(x_ref, o_ref, mod_ref, wo_ref, nfw_ref, wup_ref, cw_ref, cb_ref, wdn_ref,
                  mod1_ref, nw1_ref, w1_ref, out_ref, gb_ref, cu_ref, hid_scr, *, period):
    o = jnp.concatenate(
        [jnp.concatenate([o_ref[b, hh] for hh in range(N_HEADS)], axis=1)
         for b in range(o_ref.shape[0])], axis=0)
    x1 = _rows(x_ref) + mod_ref[0, 2:3, :] * _dot(o, wo_ref[...])
    x2 = _conv_ffn(x1, mod_ref, nfw_ref, wup_ref, cw_ref, cb_ref, wdn_ref, hid_scr, period)
    out_ref[...] = x2.reshape(out_ref.shape)
    h = _modulate(x2, nw1_ref[...], mod1_ref[0, 0:1, :], mod1_ref[0, 1:2, :]).astype(BF16)
    gb_ref[...] = _dot(h, w1_ref[:, 0:D_MODEL]).astype(gb_ref.dtype).reshape(gb_ref.shape)
    gate_c = _dot(h, w1_ref[:, D_MODEL:2 * D_MODEL])
    u = _dot(h, w1_ref[:, 2 * D_MODEL:3 * D_MODEL])
    cu_ref[...] = (gate_c * u).astype(cu_ref.dtype).reshape(cu_ref.shape)


def _post0(x, o, mod, wo, nfw, wup, cw, cb, wdn, mod1, nw1, w1, per_seq_mod, period):
    bsz, n, _ = x.shape
    tb, tl = _tile(n)
    mod_map = (lambda b, i: (b, 0, 0)) if per_seq_mod else (lambda b, i: (0, 0, 0))
    tok_spec = pl.BlockSpec((tb, tl, D_MODEL), lambda b, i: (b, i, 0))
    return pl.pallas_call(
        functools.partial(_post0_kernel, period=period),
        grid=(bsz // tb, n // tl),
        in_specs=[
            tok_spec,
            pl.BlockSpec((tb, N_HEADS, tl, D_HEAD), lambda b, i: (b, 0, i, 0)),
            pl.BlockSpec((1, N_MOD, D_MODEL), mod_map),
            _const_spec((D_MODEL, D_MODEL)),
        ] + _ffn_specs() + [
            pl.BlockSpec((1, N_MOD, D_MODEL), mod_map),
            _const_spec((1, D_MODEL)),
            _const_spec(w1.shape),
        ],
        out_specs=[tok_spec] * 3,
        out_shape=[jax.ShapeDtypeStruct(x.shape, F32), jax.ShapeDtypeStruct(x.shape, BF16),
                   jax.ShapeDtypeStruct(x.shape, BF16)],
        scratch_shapes=[pltpu.VMEM((tb * tl, D_FF), BF16)],
        compiler_params=_params(2),
        name="post0",
    )(x, o, mod, wo, nfw, wup, cw, cb, wdn, mod1, nw1, w1)


def _post1_kernel(*refs, grid_rows, period):
    if grid_rows:
        (x_ref, gb_ref, cu_ref, cup_ref, cun_ref, mod_ref, scw_ref, scb_ref, wo_ref,
         nfw_ref, wup_ref, cw_ref, cb_ref, wdn_ref, fnw_ref, out_ref, hid_scr) = refs
    else:
        (x_ref, gb_ref, cu_ref, mod_ref, scw_ref, scb_ref, wo_ref,
         nfw_ref, wup_ref, cw_ref, cb_ref, wdn_ref, fnw_ref, out_ref, hid_scr) = refs
    cu = _rows(cu_ref).astype(F32)
    t = cu.shape[0]
    if grid_rows:
        i = pl.program_id(1)
        halo_p = jnp.where(i == 0, 0.0, cup_ref[0].astype(F32))
        halo_n = jnp.where(i == pl.num_programs(1) - 1, 0.0, cun_ref[0].astype(F32))
        prev = jnp.concatenate([halo_p, cu[:t - GRID_W]], axis=0)
        nxt = jnp.concatenate([cu[GRID_W:], halo_n], axis=0)
    else:
        pos = lax.broadcasted_iota(jnp.int32, (t, 1), 0) % period
        prev = jnp.where(pos == 0, 0.0, pltpu.roll(cu, 1, 0))
        nxt = jnp.where(pos == period - 1, 0.0, pltpu.roll(cu, t - 1, 0))
    z = prev * scw_ref[0:1, :] + cu * scw_ref[1:2, :] + nxt * scw_ref[2:3, :] + scb_ref[...]
    m = (_rows(gb_ref).astype(F32) * z).astype(BF16)
    x1 = _rows(x_ref) + mod_ref[0, 2:3, :] * _dot(m, wo_ref[...])
    x2 = _conv_ffn(x1, mod_ref, nfw_ref, wup_ref, cw_ref, cb_ref, wdn_ref, hid_scr, period)
    out_ref[...] = _rms(x2, fnw_ref[...]).reshape(out_ref.shape)


def _post1(x, gb, cu, mod, scw, scb, wo, nfw, wup, cw, cb, wdn, fnw, per_seq_mod, grid_rows, period):
    bsz, n, _ = x.shape
    tb, tl = _tile(n)
    mod_map = (lambda b, i: (b, 0, 0)) if per_seq_mod else (lambda b, i: (0, 0, 0))
    tok_spec = pl.BlockSpec((tb, tl, D_MODEL), lambda b, i: (b, i, 0))
    in_specs = [tok_spec, tok_spec, tok_spec]
    args = [x, gb, cu]
    if grid_rows:
        r = tl // GRID_W
        n_rows = n // GRID_W
        in_specs += [
            pl.BlockSpec((1, GRID_W, D_MODEL), lambda b, i: (b, jnp.maximum(i * r - 1, 0), 0)),
            pl.BlockSpec((1, GRID_W, D_MODEL), lambda b, i: (b, jnp.minimum((i + 1) * r, n_rows - 1), 0)),
        ]
        args += [cu, cu]
    in_specs += [
        pl.BlockSpec((1, N_MOD, D_MODEL), mod_map),
        _const_spec((3, D_MODEL)),
        _const_spec((1, D_MODEL)),
        _const_spec((D_MODEL, D_MODEL)),
    ] + _ffn_specs() + [_const_spec((1, D_MODEL))]
    args += [mod, scw, scb, wo, nfw, wup, cw, cb, wdn, fnw]
    return pl.pallas_call(
        functools.partial(_post1_kernel, grid_rows=grid_rows, period=period),
        grid=(bsz // tb, n // tl),
        in_specs=in_specs,
        out_specs=tok_spec,
        out_shape=jax.ShapeDtypeStruct(x.shape, F32),
        scratch_shapes=[pltpu.VMEM((tb * tl, D_FF), BF16)],
        compiler_params=_params(2),
        name="post1",
    )(*args)


def kernel(x_prompt, x_sample, state_hgrn, c, c_ctx, ada_w, ada_b, norm_mix_w, norm_ffn_w,
           hgrn_w_in, hgrn_lower_bounds, hgrn_gnorm_w, hgrn_w_out,
           sconv_w_in, sconv_conv_w, sconv_conv_b, sconv_w_out,
           ffn_w_up, ffn_conv_w, ffn_conv_b, ffn_w_down, final_norm_w):
    seq_p = x_prompt.shape[1]

    cvecs = jnp.concatenate([c_ctx[None, :], c], axis=0)
    mod = _ada(cvecs, ada_w, ada_b)
    mod = mod.reshape(mod.shape[0], 8, N_MOD, D_MODEL)
    mod_p = mod[:, 0:1]
    mod_s = mod[:, 1:1 + c.shape[0]]

    row = lambda a: a.reshape(1, -1)
    w_in0 = hgrn_w_in[0].astype(BF16)
    gnw = row(hgrn_gnorm_w[0])
    lbp = hgrn_lower_bounds.reshape(-1, hgrn_lower_bounds.shape[-1])

    def mixer0(x, mods, s0, per_seq_mod, emit_state, gla_heads, casts):
        q, ff, fb, v, g = _inproj0(x, mods[0], row(norm_mix_w[0]), w_in0, lbp, per_seq_mod)
        n_chunks = x.shape[1] // SCAN_CHUNK
        unroll = min(n_chunks, SCAN_LANES // 2)
        head_group = min(gla_heads, SCAN_LANES // (2 * unroll))
        return _gla(q, ff, fb, v, g, s0, gnw, emit_state, gla_heads, head_group, unroll, casts)

    def post0(x, o, mods, per_seq_mod, ffn_period):
        return _post0(x, o, mods[0], w_out0, row(norm_ffn_w[0]), w_up0, ffn_conv_w[0],
                      row(ffn_conv_b[0]), w_dn0, mods[1], row(norm_mix_w[1]), w_in1,
                      per_seq_mod, ffn_period)

    def post1(x, gb, cu, mods, per_seq_mod, grid_rows, ffn_period):
        return _post1(x, gb, cu, mods[1], sconv_conv_w[0], row(sconv_conv_b[0]), w_out1,
                      row(norm_ffn_w[1]), w_up1, ffn_conv_w[1], row(ffn_conv_b[1]), w_dn1,
                      row(final_norm_w), per_seq_mod, grid_rows, ffn_period)

    o_p, new_state, (w_out0, w_up0, w_dn0, w_in1) = mixer0(
        x_prompt, mod_p, None, False, True, N_HEADS,
        [(hgrn_w_out, 0), (ffn_w_up, 0), (ffn_w_down, 0), (sconv_w_in, 0)])
    xp, gb_p, cu_p = post0(x_prompt, o_p, mod_p, False, seq_p)
    o_s, _, (w_out1, w_up1, w_dn1) = mixer0(
        x_sample, mod_s, state_hgrn, True, False, 1,
        [(sconv_w_out, 0), (ffn_w_up, 1), (ffn_w_down, 1)])
    y_prompt = post1(xp, gb_p, cu_p, mod_p, False, False, seq_p)
    xs, gb_s, cu_s = post0(x_sample, o_s, mod_s, True, GRID_W)
    y_sample = post1(xs, gb_s, cu_s, mod_s, True, True, GRID_W)
    return (y_prompt, y_sample, new_state.astype(state_hgrn.dtype))
```

```python
import functools

import jax
import jax.numpy as jnp
from jax import lax
from jax.experimental import pallas as pl
from jax.experimental.pallas import tpu as pltpu

D_MODEL = 1024
N_HEADS = 8
D_HEAD = 128
D_FF = 2816
N_MOD = 6
GRID_W = 64
EPS = 1e-6

LANES = 128
SCAN_CHUNK = 64
SCAN_LANES = 64
ROW_TILE = 256
TOKEN_TILE = 512
FF_CHUNK = 256
ADA_ROWS = 256
VMEM_LIMIT = 56 * 1024 * 1024

F32 = jnp.float32
BF16 = jnp.bfloat16


def _sigmoid(x):
    return 1.0 / (1.0 + jnp.exp(-x))


def _silu(x):
    return x * _sigmoid(x)


def _rms(x, w):
    return x * lax.rsqrt(jnp.mean(x * x, axis=-1, keepdims=True) + EPS) * w


def _modulate(x, w, shift, scale):
    return _rms(x, w) * (1.0 + scale) + shift


def _dot(a, b):
    return jnp.dot(a, b, preferred_element_type=F32)


def _dot_nt(a, b):
    return lax.dot_general(a, b, (((1,), (1,)), ((), ())), preferred_element_type=F32)


def _dot_tn(a, b):
    return lax.dot_general(a, b, (((0,), (0,)), ((), ())), preferred_element_type=F32)


def _params(n_grid):
    return pltpu.CompilerParams(
        dimension_semantics=("arbitrary",) * n_grid,
        vmem_limit_bytes=VMEM_LIMIT,
    )


def _const_spec(shape):
    nd = len(shape)
    return pl.BlockSpec(shape, lambda *_: (0,) * nd, pipeline_mode=pl.Buffered(1))


def _ada_kernel(c_ref, wa_ref, wb_ref, b_ref, o_ref, s_scr):
    k = pl.program_id(1)
    rows, n_out = wa_ref.shape[1], wa_ref.shape[2]

    @pl.when((pl.program_id(0) == 0) & (k == 0))
    def _():
        s_scr[...] = _silu(c_ref[...])

    @pl.when(k == 0)
    def _():
        o_ref[0] = jnp.broadcast_to(b_ref[0], o_ref.shape[1:])

    for half, w_ref in enumerate((wa_ref, wb_ref)):
        r0 = pl.multiple_of((2 * k + half) * rows, rows)
        for r in range(c_ref.shape[0]):
            s = s_scr[r, pl.ds(r0, rows), :]
            parts = [
                jnp.sum(w_ref[0, :, j * LANES:(j + 1) * LANES] * s, axis=0, keepdims=True)
                for j in range(n_out // LANES)
            ]
            o_ref[0, r:r + 1, :] += jnp.concatenate(parts, axis=1)


def _ada(cvecs, ada_w, ada_b):
    depth, d_in, n_out = ada_w.shape
    cb = jnp.broadcast_to(cvecs[:, :, None], cvecs.shape + (LANES,))
    return pl.pallas_call(
        _ada_kernel,
        grid=(depth, d_in // ADA_ROWS),
        in_specs=[
            pl.BlockSpec(cb.shape, lambda l, k: (0, 0, 0)),
            pl.BlockSpec((1, ADA_ROWS // 2, n_out), lambda l, k: (l, 2 * k, 0)),
            pl.BlockSpec((1, ADA_ROWS // 2, n_out), lambda l, k: (l, 2 * k + 1, 0)),
            pl.BlockSpec((1, 1, n_out), lambda l, k: (l, 0, 0)),
        ],
        out_specs=pl.BlockSpec((1, 8, n_out), lambda l, k: (l, 0, 0)),
        out_shape=jax.ShapeDtypeStruct((depth, 8, n_out), F32),
        scratch_shapes=[pltpu.VMEM(cb.shape, F32)],
        compiler_params=_params(2),
        name="ada",
    )(cb, ada_w, ada_w, ada_b.reshape(depth, 1, n_out))


def _inproj0_kernel(x_ref, mod_ref, nw_ref, w_ref, lbp_ref, q_ref, ff_ref, fb_ref, v_ref, g_ref):
    h = _modulate(_rows(x_ref), nw_ref[...], mod_ref[0, 0:1, :], mod_ref[0, 1:2, :]).astype(BF16)
    tb, _, tl, _ = q_ref.shape

    n_lb = lbp_ref.shape[0] // 2

    def lower_bound(d):
        rows = [lbp_ref[d * n_lb + i:d * n_lb + i + 1, :] for i in range(n_lb)]
        top = functools.reduce(jnp.maximum, rows)
        e = [jnp.exp(r - top) for r in rows]
        return e[0] / functools.reduce(jnp.add, e)

    lb = [lower_bound(0), lower_bound(1)]

    def proj(j):
        return _dot(h, w_ref[:, j * D_MODEL:(j + 1) * D_MODEL])

    def put(ref, val):
        for b in range(tb):
            for hh in range(N_HEADS):
                ref[b, hh] = val[b * tl:(b + 1) * tl, hh * D_HEAD:(hh + 1) * D_HEAD].astype(ref.dtype)

    put(q_ref, _silu(proj(0)))
    put(ff_ref, lb[0] + (1.0 - lb[0]) * _sigmoid(proj(1)))
    put(fb_ref, lb[1] + (1.0 - lb[1]) * _sigmoid(proj(2)))
    put(v_ref, proj(3))
    put(g_ref, _silu(proj(4)))


def _inproj0(x, mod, nw, w_in, lbp, per_seq_mod):
    bsz, n, _ = x.shape
    tb, tl = _tile(n)
    mod_map = (lambda b, i: (b, 0, 0)) if per_seq_mod else (lambda b, i: (0, 0, 0))
    head_spec = pl.BlockSpec((tb, N_HEADS, tl, D_HEAD), lambda b, i: (b, 0, i, 0))
    shp = (bsz, N_HEADS, n, D_HEAD)
    return pl.pallas_call(
        _inproj0_kernel,
        grid=(bsz // tb, n // tl),
        in_specs=[
            pl.BlockSpec((tb, tl, D_MODEL), lambda b, i: (b, i, 0)),
            pl.BlockSpec((1, N_MOD, D_MODEL), mod_map),
            _const_spec((1, D_MODEL)),
            _const_spec(w_in.shape),
            _const_spec(lbp.shape),
        ],
        out_specs=[head_spec] * 5,
        out_shape=[
            jax.ShapeDtypeStruct(shp, BF16),
            jax.ShapeDtypeStruct(shp, F32),
            jax.ShapeDtypeStruct(shp, F32),
            jax.ShapeDtypeStruct(shp, BF16),
            jax.ShapeDtypeStruct(shp, BF16),
        ],
        compiler_params=_params(2),
        name="inproj0",
    )(x, mod, nw, w_in, lbp)


def _cumsum_rows(tri, x):
    hi = x.astype(BF16)
    lo = (x - hi.astype(F32)).astype(BF16)
    s = _dot(tri, jnp.concatenate([hi, lo], axis=1))
    return s[:, :D_HEAD] + s[:, D_HEAD:]


def _gla_kernel(*refs, seq_len, heads, head_group, unroll, zero_init, emit_state, n_casts):
    q_ref, ff_ref, fb_ref, v_ref, g_ref = refs[:5]
    pos = 5
    s0_ref = None
    if not zero_init:
        s0_ref = refs[pos]
        pos += 1
    gnw_ref = refs[pos]
    cast_in = refs[pos + 1:pos + 1 + n_casts]
    pos += 1 + n_casts
    o_ref = refs[pos]
    pos += 1
    sf_ref = None
    if emit_state:
        sf_ref = refs[pos]
        pos += 1
    cast_out = refs[pos:pos + n_casts]
    pos += n_casts
    of_scr, ob_scr = refs[pos], refs[pos + 1]

    for src, dst in zip(cast_in, cast_out):
        dst[0] = src[0, 0].astype(dst.dtype)

    c = SCAN_CHUNK
    n_chunks = seq_len // c
    row = lax.broadcasted_iota(jnp.int32, (c, c), 0)
    col = lax.broadcasted_iota(jnp.int32, (c, c), 1)
    lower = col <= row
    upper = col >= row
    tri_lo = lower.astype(BF16)
    tri_up = upper.astype(BF16)
    gnw = gnw_ref[...]

    directions = (
        (ff_ref, tri_lo, lower, c // 2 - 1, c - 1, of_scr),
        (fb_ref, tri_up, upper, c // 2, 0, ob_scr),
    )

    def trip(hs, i, states):
        lanes = []
        for u in range(unroll):
            for slot, h in enumerate(hs):
                for d in range(2):
                    n = i * unroll + u
                    if d == 1:
                        n = n_chunks - 1 - n
                    r0 = n * c if isinstance(n, int) else pl.multiple_of(n * c, c)
                    lanes.append(dict(slot=slot, h=h, d=d, r0=r0, chain=2 * slot + d))
        states = list(states)

        def stage_a(ln):
            f_ref, tri = directions[ln["d"]][:2]
            ln["f"] = f_ref[0, ln["h"], pl.ds(ln["r0"], c), :]
            ln["cum"] = _cumsum_rows(tri, jnp.log2(ln["f"]))

        def stage_b(ln):
            _, _, _, mid, last, _ = directions[ln["d"]]
            cum = ln["cum"]
            cum_mid = cum[mid:mid + 1]
            cum_last = cum[last:last + 1]
            k = 1.0 - ln["f"]
            q = q_ref[0, ln["h"], pl.ds(ln["r0"], c), :].astype(F32)
            ln["v"] = v_ref[0, ln["h"], pl.ds(ln["r0"], c), :]
            q_mid = q * jnp.exp2(cum - cum_mid)
            k_mid = k * jnp.exp2(cum_mid - cum)
            ln["q_dec"] = (q_mid * jnp.exp2(cum_mid)).astype(BF16)
            k_state = (k_mid * jnp.exp2(cum_last - cum_mid)).astype(BF16)
            ln["scores"] = _dot_nt(q_mid.astype(BF16), k_mid.astype(BF16))
            ln["kv"] = _dot_tn(ln["v"], k_state)
            ln["decay"] = jnp.exp2(cum_last)

        def stage_c(ln):
            st = states[ln["chain"]]
            ln["st"] = st.astype(BF16).T
            states[ln["chain"]] = st * ln["decay"] + ln["kv"]

        def stage_d(ln):
            mask, out_scr = directions[ln["d"]][2], directions[ln["d"]][5]
            p = jnp.where(mask, ln["scores"], 0.0).astype(BF16)
            lhs = jnp.concatenate([ln["q_dec"], p], axis=1)
            rhs = jnp.concatenate([ln["st"], ln["v"]], axis=0)
            out_scr[ln["slot"], pl.ds(ln["r0"], c), :] = _dot(lhs, rhs)

        for stage in (stage_a, stage_b, stage_c, stage_d):
            for ln in lanes:
                stage(ln)
        return tuple(states)

    def paired_trip(hs, states):
        lanes = [dict(slot=slot, h=h, n=n, r0=n * c)
                 for n in range(n_chunks) for slot, h in enumerate(hs)]
        states = list(states)
        zeros = jnp.zeros((c, D_HEAD), BF16)
        row2 = lax.broadcasted_iota(jnp.int32, (c, 2 * c), 0)
        col2 = lax.broadcasted_iota(jnp.int32, (c, 2 * c), 1)
        mask2 = ((col2 < c) & (col2 <= row2)) | ((col2 >= c) & (col2 - c >= row2))

        def stage_a(ln):
            rows = pl.ds(ln["r0"], c)
            ln["f"] = [ff_ref[0, ln["h"], rows, :], fb_ref[0, ln["h"], rows, :]]
            ln["cum"] = [_cumsum_rows(tri_lo, jnp.log2(ln["f"][0])),
                         _cumsum_rows(tri_up, jnp.log2(ln["f"][1]))]

        def stage_b(ln):
            rows = pl.ds(ln["r0"], c)
            q = q_ref[0, ln["h"], rows, :].astype(F32)
            ln["v"] = v_ref[0, ln["h"], rows, :]
            q_mid, k_mid, q_dec, k_state, ln["decay"] = [], [], [], [], []
            for d in range(2):
                mid, last = directions[d][3], directions[d][4]
                cum = ln["cum"][d]
                cum_mid = cum[mid:mid + 1]
                cum_last = cum[last:last + 1]
                qm = q * jnp.exp2(cum - cum_mid)
                km = (1.0 - ln["f"][d]) * jnp.exp2(cum_mid - cum)
                q_mid.append(qm.astype(BF16))
                k_mid.append(km.astype(BF16))
                q_dec.append((qm * jnp.exp2(cum_mid)).astype(BF16))
                k_state.append((km * jnp.exp2(cum_last - cum_mid)).astype(BF16))
                ln["decay"].append(jnp.exp2(cum_last))
            k_pair = jnp.concatenate([jnp.concatenate([k_mid[0], zeros], axis=1),
                                      jnp.concatenate([zeros, k_mid[1]], axis=1)], axis=0)
            ln["scores"] = _dot_nt(jnp.concatenate(q_mid, axis=1), k_pair)
            ln["kv"] = _dot_tn(ln["v"], jnp.concatenate(k_state, axis=1))
            ln["q_dec"] = jnp.concatenate(q_dec, axis=1)

        def stage_c(slot):
            mine = [ln for ln in lanes if ln["slot"] == slot]
            for d, order in ((0, mine), (1, mine[::-1])):
                st = states[2 * slot + d]
                for ln in order:
                    ln.setdefault("st", [None, None])[d] = st.astype(BF16).T
                    st = st * ln["decay"][d] + ln["kv"][:, d * D_HEAD:(d + 1) * D_HEAD]
                states[2 * slot + d] = st

        def stage_d(ln):
            rows = pl.ds(ln["r0"], c)
            p = jnp.where(mask2, ln["scores"], 0.0).astype(BF16)
            lhs = jnp.concatenate([ln["q_dec"], p], axis=1)
            rhs = jnp.concatenate([ln["st"][0], ln["st"][1], ln["v"], ln["v"]], axis=0)
            tot = _dot(lhs, rhs)
            gate = g_ref[0, ln["h"], rows, :].astype(F32)
            o_ref[0, ln["h"], rows, :] = (_rms(tot, gnw) * gate).astype(o_ref.dtype)

        for ln in lanes:
            stage_a(ln)
        for ln in lanes:
            stage_b(ln)
        for slot in range(len(hs)):
            stage_c(slot)
        for ln in lanes:
            stage_d(ln)
        return tuple(states)

    def head_group_body(hg, carry):
        hs = [hg * head_group + t for t in range(head_group)]
        states = []
        for h in hs:
            for d in range(2):
                if zero_init:
                    states.append(jnp.zeros((D_HEAD, D_HEAD), F32))
                else:
                    states.append(s0_ref[0, 0, d, h].T)
        states = tuple(states)
        n_trips = n_chunks // unroll
        if n_trips == 1:
            states = paired_trip(hs, states)
        else:
            states = lax.fori_loop(0, n_trips, functools.partial(trip, hs), states)

        eb = min(seq_len, ROW_TILE)
        for slot, h in enumerate(hs):
            if emit_state:
                sf_ref[0, 0, 0, h] = states[2 * slot].T
                sf_ref[0, 0, 1, h] = states[2 * slot + 1].T
            if n_trips == 1:
                continue

            def norm_body(j, carry2, slot=slot, h=h):
                r0 = pl.multiple_of(j * eb, eb)
                tot = of_scr[slot, pl.ds(r0, eb), :] + ob_scr[slot, pl.ds(r0, eb), :]
                gate = g_ref[0, h, pl.ds(r0, eb), :].astype(F32)
                o_ref[0, h, pl.ds(r0, eb), :] = (_rms(tot, gnw) * gate).astype(o_ref.dtype)
                return carry2

            if seq_len == eb:
                norm_body(0, 0)
            else:
                lax.fori_loop(0, seq_len // eb, norm_body, 0, unroll=4)
        return carry

    if heads == head_group:
        head_group_body(0, 0)
    else:
        lax.fori_loop(0, heads // head_group, head_group_body, 0)


def _gla(q, ff, fb, v, g, s0, gnw, emit_state, heads, head_group, unroll, casts):
    bsz, _, n, _ = q.shape
    n_hb = N_HEADS // heads
    steps = bsz * n_hb
    zero_init = s0 is None
    seq_spec = pl.BlockSpec((1, heads, n, D_HEAD), lambda b, h: (b, h, 0, 0))
    state_spec = pl.BlockSpec((1, 1, 2, heads, D_HEAD, D_HEAD), lambda b, h: (b, 0, 0, h, 0, 0))
    in_specs = [seq_spec] * 5
    args = [q, ff, fb, v, g]
    if not zero_init:
        in_specs.append(state_spec)
        args.append(s0)
    in_specs.append(_const_spec((1, D_HEAD)))
    args.append(gnw)
    out_specs = [seq_spec]
    out_shape = [jax.ShapeDtypeStruct(q.shape, BF16)]
    if emit_state:
        out_specs.append(state_spec)
        out_shape.append(jax.ShapeDtypeStruct((bsz, 1, 2, N_HEADS, D_HEAD, D_HEAD), F32))
    for w, layer in casts:
        _, rows, cols = w.shape
        slab = rows // steps
        in_specs.append(pl.BlockSpec((1, 1, slab, cols),
                                     lambda b, h, layer=layer: (layer, b * n_hb + h, 0, 0)))
        args.append(w.reshape(w.shape[0], steps, slab, cols))
        out_specs.append(pl.BlockSpec((1, slab, cols), lambda b, h: (b * n_hb + h, 0, 0)))
        out_shape.append(jax.ShapeDtypeStruct((steps, slab, cols), BF16))
    outs = pl.pallas_call(
        functools.partial(_gla_kernel, seq_len=n, heads=heads, head_group=head_group, unroll=unroll,
                          zero_init=zero_init, emit_state=emit_state, n_casts=len(casts)),
        grid=(bsz, n_hb),
        in_specs=in_specs,
        out_specs=out_specs,
        out_shape=out_shape,
        scratch_shapes=[pltpu.VMEM((head_group, n, D_HEAD), F32)] * 2,
        compiler_params=_params(2),
        name="gla",
    )(*args)
    n_main = 2 if emit_state else 1
    cast_out = [o.reshape(w.shape[1], w.shape[2]) for o, (w, _) in zip(outs[n_main:], casts)]
    return outs[0], (outs[1] if emit_state else None), cast_out


def _conv_ffn(x1, mod_ref, nfw_ref, wup_ref, cw_ref, cb_ref, wdn_ref, hid_scr, period):
    t = x1.shape[0]
    h = _modulate(x1, nfw_ref[...], mod_ref[0, 3:4, :], mod_ref[0, 4:5, :]).astype(BF16)
    pos = lax.broadcasted_iota(jnp.int32, (t, 1), 0) % period
    first = pos == 0
    last = pos == period - 1
    for j in range(D_FF // FF_CHUNK):
        lo = j * FF_CHUNK
        a = _dot(h, wup_ref[:, lo:lo + FF_CHUNK])
        g = _dot(h, wup_ref[:, D_FF + lo:D_FF + lo + FF_CHUNK])
        prev = jnp.where(first, 0.0, pltpu.roll(a, 1, 0))
        nxt = jnp.where(last, 0.0, pltpu.roll(a, t - 1, 0))
        ac = (prev * cw_ref[0:1, lo:lo + FF_CHUNK] + a * cw_ref[1:2, lo:lo + FF_CHUNK]
              + nxt * cw_ref[2:3, lo:lo + FF_CHUNK] + cb_ref[:, lo:lo + FF_CHUNK])
        hid_scr[:, lo:lo + FF_CHUNK] = (_silu(ac) * g).astype(BF16)
    return x1 + mod_ref[0, 5:6, :] * _dot(hid_scr[...], wdn_ref[...])


def _tile(n):
    return (1, TOKEN_TILE) if n >= TOKEN_TILE else (TOKEN_TILE // n, n)


def _rows(ref):
    v = ref[...]
    return v.reshape(v.shape[0] * v.shape[1], v.shape[2])


def _ffn_specs():
    return [
        _const_spec((1, D_MODEL)),
        _const_spec((D_MODEL, 2 * D_FF)),
        _const_spec((3, D_FF)),
        _const_spec((1, D_FF)),
        _const_spec((D_FF, D_MODEL)),
    ]


def _post0_kernel(x_ref, o_ref, mod_ref, wo_ref, nfw_ref, wup_ref, cw_ref, cb_ref, wdn_ref,
                  mod1_ref, nw1_ref, w1_ref, out_ref, gb_ref, cu_ref, hid_scr, *, period):
    o = jnp.concatenate(
        [jnp.concatenate([o_ref[b, hh] for hh in range(N_HEADS)], axis=1)
         for b in range(o_ref.shape[0])], axis=0)
    x1 = _rows(x_ref) + mod_ref[0, 2:3, :] * _dot(o, wo_ref[...])
    x2 = _conv_ffn(x1, mod_ref, nfw_ref, wup_ref, cw_ref, cb_ref, wdn_ref, hid_scr, period)
    out_ref[...] = x2.reshape(out_ref.shape)
    h = _modulate(x2, nw1_ref[...], mod1_ref[0, 0:1, :], mod1_ref[0, 1:2, :]).astype(BF16)
    gb_ref[...] = _dot(h, w1_ref[:, 0:D_MODEL]).astype(gb_ref.dtype).reshape(gb_ref.shape)
    gate_c = _dot(h, w1_ref[:, D_MODEL:2 * D_MODEL])
    u = _dot(h, w1_ref[:, 2 * D_MODEL:3 * D_MODEL])
    cu_ref[...] = (gate_c * u).astype(cu_ref.dtype).reshape(cu_ref.shape)


def _post0(x, o, mod, wo, nfw, wup, cw, cb, wdn, mod1, nw1, w1, per_seq_mod, period):
    bsz, n, _ = x.shape
    tb, tl = _tile(n)
    mod_map = (lambda b, i: (b, 0, 0)) if per_seq_mod else (lambda b, i: (0, 0, 0))
    tok_spec = pl.BlockSpec((tb, tl, D_MODEL), lambda b, i: (b, i, 0))
    return pl.pallas_call(
        functools.partial(_post0_kernel, period=period),
        grid=(bsz // tb, n // tl),
        in_specs=[
            tok_spec,
            pl.BlockSpec((tb, N_HEADS, tl, D_HEAD), lambda b, i: (b, 0, i, 0)),
            pl.BlockSpec((1, N_MOD, D_MODEL), mod_map),
            _const_spec((D_MODEL, D_MODEL)),
        ] + _ffn_specs() + [
            pl.BlockSpec((1, N_MOD, D_MODEL), mod_map),
            _const_spec((1, D_MODEL)),
            _const_spec(w1.shape),
        ],
        out_specs=[tok_spec] * 3,
        out_shape=[jax.ShapeDtypeStruct(x.shape, F32), jax.ShapeDtypeStruct(x.shape, BF16),
                   jax.ShapeDtypeStruct(x.shape, BF16)],
        scratch_shapes=[pltpu.VMEM((tb * tl, D_FF), BF16)],
        compiler_params=_params(2),
        name="post0",
    )(x, o, mod, wo, nfw, wup, cw, cb, wdn, mod1, nw1, w1)


def _post1_kernel(*refs, grid_rows, period):
    if grid_rows:
        (x_ref, gb_ref, cu_ref, cup_ref, cun_ref, mod_ref, scw_ref, scb_ref, wo_ref,
         nfw_ref, wup_ref, cw_ref, cb_ref, wdn_ref, fnw_ref, out_ref, hid_scr) = refs
    else:
        (x_ref, gb_ref, cu_ref, mod_ref, scw_ref, scb_ref, wo_ref,
         nfw_ref, wup_ref, cw_ref, cb_ref, wdn_ref, fnw_ref, out_ref, hid_scr) = refs
    cu = _rows(cu_ref).astype(F32)
    t = cu.shape[0]
    if grid_rows:
        i = pl.program_id(1)
        halo_p = jnp.where(i == 0, 0.0, cup_ref[0].astype(F32))
        halo_n = jnp.where(i == pl.num_programs(1) - 1, 0.0, cun_ref[0].astype(F32))
        prev = jnp.concatenate([halo_p, cu[:t - GRID_W]], axis=0)
        nxt = jnp.concatenate([cu[GRID_W:], halo_n], axis=0)
    else:
        pos = lax.broadcasted_iota(jnp.int32, (t, 1), 0) % period
        prev = jnp.where(pos == 0, 0.0, pltpu.roll(cu, 1, 0))
        nxt = jnp.where(pos == period - 1, 0.0, pltpu.roll(cu, t - 1, 0))
    z = prev * scw_ref[0:1, :] + cu * scw_ref[1:2, :] + nxt * scw_ref[2:3, :] + scb_ref[...]
    m = (_rows(gb_ref).astype(F32) * z).astype(BF16)
    x1 = _rows(x_ref) + mod_ref[0, 2:3, :] * _dot(m, wo_ref[...])
    x2 = _conv_ffn(x1, mod_ref, nfw_ref, wup_ref, cw_ref, cb_ref, wdn_ref, hid_scr, period)
    out_ref[...] = _rms(x2, fnw_ref[...]).reshape(out_ref.shape)


def _post1(x, gb, cu, mod, scw, scb, wo, nfw, wup, cw, cb, wdn, fnw, per_seq_mod, grid_rows, period):
    bsz, n, _ = x.shape
    tb, tl = _tile(n)
    mod_map = (lambda b, i: (b, 0, 0)) if per_seq_mod else (lambda b, i: (0, 0, 0))
    tok_spec = pl.BlockSpec((tb, tl, D_MODEL), lambda b, i: (b, i, 0))
    in_specs = [tok_spec, tok_spec, tok_spec]
    args = [x, gb, cu]
    if grid_rows:
        r = tl // GRID_W
        n_rows = n // GRID_W
        in_specs += [
            pl.BlockSpec((1, GRID_W, D_MODEL), lambda b, i: (b, jnp.maximum(i * r - 1, 0), 0)),
            pl.BlockSpec((1, GRID_W, D_MODEL), lambda b, i: (b, jnp.minimum((i + 1) * r, n_rows - 1), 0)),
        ]
        args += [cu, cu]
    in_specs += [
        pl.BlockSpec((1, N_MOD, D_MODEL), mod_map),
        _const_spec((3, D_MODEL)),
        _const_spec((1, D_MODEL)),
        _const_spec((D_MODEL, D_MODEL)),
    ] + _ffn_specs() + [_const_spec((1, D_MODEL))]
    args += [mod, scw, scb, wo, nfw, wup, cw, cb, wdn, fnw]
    return pl.pallas_call(
        functools.partial(_post1_kernel, grid_rows=grid_rows, period=period),
        grid=(bsz // tb, n // tl),
        in_specs=in_specs,
        out_specs=tok_spec,
        out_shape=jax.ShapeDtypeStruct(x.shape, F32),
        scratch_shapes=[pltpu.VMEM((tb * tl, D_FF), BF16)],
        compiler_params=_params(2),
        name="post1",
    )(*args)


def kernel(x_prompt, x_sample, state_hgrn, c, c_ctx, ada_w, ada_b, norm_mix_w, norm_ffn_w,
           hgrn_w_in, hgrn_lower_bounds, hgrn_gnorm_w, hgrn_w_out,
           sconv_w_in, sconv_conv_w, sconv_conv_b, sconv_w_out,
           ffn_w_up, ffn_conv_w, ffn_conv_b, ffn_w_down, final_norm_w):
    seq_p = x_prompt.shape[1]

    cvecs = jnp.concatenate([c_ctx[None, :], c], axis=0)
    mod = _ada(cvecs, ada_w, ada_b)
    mod = mod.reshape(mod.shape[0], 8, N_MOD, D_MODEL)
    mod_p = mod[:, 0:1]
    mod_s = mod[:, 1:1 + c.shape[0]]

    row = lambda a: a.reshape(1, -1)
    w_in0 = hgrn_w_in[0].astype(BF16)
    gnw = row(hgrn_gnorm_w[0])
    lbp = hgrn_lower_bounds.reshape(-1, hgrn_lower_bounds.shape[-1])

    def mixer0(x, mods, s0, per_seq_mod, emit_state, gla_heads, casts):
        q, ff, fb, v, g = _inproj0(x, mods[0], row(norm_mix_w[0]), w_in0, lbp, per_seq_mod)
        n_chunks = x.shape[1] // SCAN_CHUNK
        unroll = min(n_chunks, SCAN_LANES // 2)
        head_group = min(gla_heads, SCAN_LANES // (2 * unroll))
        return _gla(q, ff, fb, v, g, s0, gnw, emit_state, gla_heads, head_group, unroll, casts)

    def post0(x, o, mods, per_seq_mod, ffn_period):
        return _post0(x, o, mods[0], w_out0, row(norm_ffn_w[0]), w_up0, ffn_conv_w[0],
                      row(ffn_conv_b[0]), w_dn0, mods[1], row(norm_mix_w[1]), w_in1,
                      per_seq_mod, ffn_period)

    def post1(x, gb, cu, mods, per_seq_mod, grid_rows, ffn_period):
        return _post1(x, gb, cu, mods[1], sconv_conv_w[0], row(sconv_conv_b[0]), w_out1,
                      row(norm_ffn_w[1]), w_up1, ffn_conv_w[1], row(ffn_conv_b[1]), w_dn1,
                      row(final_norm_w), per_seq_mod, grid_rows, ffn_period)

    o_p, new_state, (w_out0, w_up0, w_dn0, w_in1) = mixer0(
        x_prompt, mod_p, None, False, True, N_HEADS,
        [(hgrn_w_out, 0), (ffn_w_up, 0), (ffn_w_down, 0), (sconv_w_in, 0)])
    xp, gb_p, cu_p = post0(x_prompt, o_p, mod_p, False, seq_p)
    o_s, _, (w_out1, w_up1, w_dn1) = mixer0(
        x_sample, mod_s, state_hgrn, True, False, 1,
        [(sconv_w_out, 0), (ffn_w_up, 1), (ffn_w_down, 1)])
    y_prompt = post1(xp, gb_p, cu_p, mod_p, False, False, seq_p)
    xs, gb_s, cu_s = post0(x_sample, o_s, mod_s, True, GRID_W)
    y_sample = post1(xs, gb_s, cu_s, mod_s, True, True, GRID_W)
    return (y_prompt, y_sample, new_state.astype(state_hgrn.dtype))
```

```python
import functools

import jax
import jax.numpy as jnp
from jax import lax
from jax.experimental import pallas as pl
from jax.experimental.pallas import tpu as pltpu

D_MODEL = 1024
N_HEADS = 8
D_HEAD = 128
D_FF = 2816
N_MOD = 6
GRID_W = 64
EPS = 1e-6

LANES = 128
SCAN_CHUNK = 64
SCAN_LANES = 64
ROW_TILE = 256
TOKEN_TILE = 512
FF_CHUNK = 256
ADA_ROWS = 256
VMEM_LIMIT = 56 * 1024 * 1024

F32 = jnp.float32
BF16 = jnp.bfloat16


def _sigmoid(x):
    return 1.0 / (1.0 + jnp.exp(-x))


def _silu(x):
    return x * _sigmoid(x)


def _rms(x, w):
    return x * lax.rsqrt(jnp.mean(x * x, axis=-1, keepdims=True) + EPS) * w


def _modulate(x, w, shift, scale):
    return _rms(x, w) * (1.0 + scale) + shift


def _dot(a, b):
    return jnp.dot(a, b, preferred_element_type=F32)


def _dot_nt(a, b):
    return lax.dot_general(a, b, (((1,), (1,)), ((), ())), preferred_element_type=F32)


def _dot_tn(a, b):
    return lax.dot_general(a, b, (((0,), (0,)), ((), ())), preferred_element_type=F32)


def _params(n_grid):
    return pltpu.CompilerParams(
        dimension_semantics=("arbitrary",) * n_grid,
        vmem_limit_bytes=VMEM_LIMIT,
    )


def _const_spec(shape):
    nd = len(shape)
    return pl.BlockSpec(shape, lambda *_: (0,) * nd, pipeline_mode=pl.Buffered(1))


def _ada_kernel(c_ref, wa_ref, wb_ref, b_ref, o_ref, s_scr):
    k = pl.program_id(1)
    rows, n_out = wa_ref.shape[1], wa_ref.shape[2]

    @pl.when((pl.program_id(0) == 0) & (k == 0))
    def _():
        s_scr[...] = _silu(c_ref[...])

    @pl.when(k == 0)
    def _():
        o_ref[0] = jnp.broadcast_to(b_ref[0], o_ref.shape[1:])

    for half, w_ref in enumerate((wa_ref, wb_ref)):
        r0 = pl.multiple_of((2 * k + half) * rows, rows)
        for r in range(c_ref.shape[0]):
            s = s_scr[r, pl.ds(r0, rows), :]
            parts = [
                jnp.sum(w_ref[0, :, j * LANES:(j + 1) * LANES] * s, axis=0, keepdims=True)
                for j in range(n_out // LANES)
            ]
            o_ref[0, r:r + 1, :] += jnp.concatenate(parts, axis=1)


def _ada(cvecs, ada_w, ada_b):
    depth, d_in, n_out = ada_w.shape
    cb = jnp.broadcast_to(cvecs[:, :, None], cvecs.shape + (LANES,))
    return pl.pallas_call(
        _ada_kernel,
        grid=(depth, d_in // ADA_ROWS),
        in_specs=[
            pl.BlockSpec(cb.shape, lambda l, k: (0, 0, 0)),
            pl.BlockSpec((1, ADA_ROWS // 2, n_out), lambda l, k: (l, 2 * k, 0)),
            pl.BlockSpec((1, ADA_ROWS // 2, n_out), lambda l, k: (l, 2 * k + 1, 0)),
            pl.BlockSpec((1, 1, n_out), lambda l, k: (l, 0, 0)),
        ],
        out_specs=pl.BlockSpec((1, 8, n_out), lambda l, k: (l, 0, 0)),
        out_shape=jax.ShapeDtypeStruct((depth, 8, n_out), F32),
        scratch_shapes=[pltpu.VMEM(cb.shape, F32)],
        compiler_params=_params(2),
        name="ada",
    )(cb, ada_w, ada_w, ada_b.reshape(depth, 1, n_out))


def _inproj0_kernel(x_ref, mod_ref, nw_ref, w_ref, lbp_ref, q_ref, ff_ref, fb_ref, v_ref, g_ref):
    x = _rows(x_ref)
    tb, _, tl, _ = q_ref.shape
    half = x.shape[0] // 2
    hs = [_modulate(x[s * half:(s + 1) * half], nw_ref[...], mod_ref[0, 0:1, :],
                    mod_ref[0, 1:2, :]).astype(BF16) for s in range(2)]

    n_lb = lbp_ref.shape[0] // 2

    def lower_bound(d):
        rows = [lbp_ref[d * n_lb + i:d * n_lb + i + 1, :] for i in range(n_lb)]
        top = functools.reduce(jnp.maximum, rows)
        e = [jnp.exp(r - top) for r in rows]
        return e[0] / functools.reduce(jnp.add, e)

    lb = [lower_bound(0), lower_bound(1)]

    def emit(j, ref, act):
        vals = [_dot(h, w_ref[:, j * D_MODEL:(j + 1) * D_MODEL]) for h in hs]
        for s, val in enumerate(vals):
            val = act(val)
            b, r0 = divmod(s * half, tl)
            for hh in range(N_HEADS):
                ref[b, hh, r0:r0 + half, :] = val[:, hh * D_HEAD:(hh + 1) * D_HEAD].astype(ref.dtype)

    emit(0, q_ref, _silu)
    emit(1, ff_ref, lambda p: lb[0] + (1.0 - lb[0]) * _sigmoid(p))
    emit(2, fb_ref, lambda p: lb[1] + (1.0 - lb[1]) * _sigmoid(p))
    emit(3, v_ref, lambda p: p)
    emit(4, g_ref, _silu)


def _inproj0(x, mod, nw, w_in, lbp, per_seq_mod):
    bsz, n, _ = x.shape
    tb, tl = _tile(n)
    mod_map = (lambda b, i: (b, 0, 0)) if per_seq_mod else (lambda b, i: (0, 0, 0))
    head_spec = pl.BlockSpec((tb, N_HEADS, tl, D_HEAD), lambda b, i: (b, 0, i, 0))
    shp = (bsz, N_HEADS, n, D_HEAD)
    return pl.pallas_call(
        _inproj0_kernel,
        grid=(bsz // tb, n // tl),
        in_specs=[
            pl.BlockSpec((tb, tl, D_MODEL), lambda b, i: (b, i, 0)),
            pl.BlockSpec((1, N_MOD, D_MODEL), mod_map),
            _const_spec((1, D_MODEL)),
            _const_spec(w_in.shape),
            _const_spec(lbp.shape),
        ],
        out_specs=[head_spec] * 5,
        out_shape=[
            jax.ShapeDtypeStruct(shp, BF16),
            jax.ShapeDtypeStruct(shp, F32),
            jax.ShapeDtypeStruct(shp, F32),
            jax.ShapeDtypeStruct(shp, BF16),
            jax.ShapeDtypeStruct(shp, BF16),
        ],
        compiler_params=_params(2),
        name="inproj0",
    )(x, mod, nw, w_in, lbp)


def _cumsum_rows(tri, x):
    hi = x.astype(BF16)
    lo = (x - hi.astype(F32)).astype(BF16)
    s = _dot(tri, jnp.concatenate([hi, lo], axis=1))
    return s[:, :D_HEAD] + s[:, D_HEAD:]


def _gla_kernel(*refs, seq_len, heads, head_group, unroll, zero_init, emit_state, n_casts):
    q_ref, ff_ref, fb_ref, v_ref, g_ref = refs[:5]
    pos = 5
    s0_ref = None
    if not zero_init:
        s0_ref = refs[pos]
        pos += 1
    gnw_ref = refs[pos]
    cast_in = refs[pos + 1:pos + 1 + n_casts]
    pos += 1 + n_casts
    o_ref = refs[pos]
    pos += 1
    sf_ref = None
    if emit_state:
        sf_ref = refs[pos]
        pos += 1
    cast_out = refs[pos:pos + n_casts]
    pos += n_casts
    of_scr, ob_scr = refs[pos], refs[pos + 1]

    for src, dst in zip(cast_in, cast_out):
        dst[0] = src[0, 0].astype(dst.dtype)

    c = SCAN_CHUNK
    n_chunks = seq_len // c
    row = lax.broadcasted_iota(jnp.int32, (c, c), 0)
    col = lax.broadcasted_iota(jnp.int32, (c, c), 1)
    lower = col <= row
    upper = col >= row
    tri_lo = lower.astype(BF16)
    tri_up = upper.astype(BF16)
    gnw = gnw_ref[...]

    directions = (
        (ff_ref, tri_lo, lower, c // 2 - 1, c - 1, of_scr),
        (fb_ref, tri_up, upper, c // 2, 0, ob_scr),
    )

    def trip(hs, i, states):
        lanes = []
        for u in range(unroll):
            for slot, h in enumerate(hs):
                for d in range(2):
                    n = i * unroll + u
                    if d == 1:
                        n = n_chunks - 1 - n
                    r0 = n * c if isinstance(n, int) else pl.multiple_of(n * c, c)
                    lanes.append(dict(slot=slot, h=h, d=d, r0=r0, chain=2 * slot + d))
        states = list(states)

        def stage_a(ln):
            f_ref, tri = directions[ln["d"]][:2]
            ln["f"] = f_ref[0, ln["h"], pl.ds(ln["r0"], c), :]
            ln["cum"] = _cumsum_rows(tri, jnp.log2(ln["f"]))

        def stage_b(ln):
            _, _, _, mid, last, _ = directions[ln["d"]]
            cum = ln["cum"]
            cum_mid = cum[mid:mid + 1]
            cum_last = cum[last:last + 1]
            k = 1.0 - ln["f"]
            q = q_ref[0, ln["h"], pl.ds(ln["r0"], c), :].astype(F32)
            ln["v"] = v_ref[0, ln["h"], pl.ds(ln["r0"], c), :]
            q_mid = q * jnp.exp2(cum - cum_mid)
            k_mid = k * jnp.exp2(cum_mid - cum)
            ln["q_dec"] = (q_mid * jnp.exp2(cum_mid)).astype(BF16)
            k_state = (k_mid * jnp.exp2(cum_last - cum_mid)).astype(BF16)
            ln["scores"] = _dot_nt(q_mid.astype(BF16), k_mid.astype(BF16))
            ln["kv"] = _dot_tn(ln["v"], k_state)
            ln["decay"] = jnp.exp2(cum_last)

        def stage_c(ln):
            st = states[ln["chain"]]
            ln["st"] = st.astype(BF16).T
            states[ln["chain"]] = st * ln["decay"] + ln["kv"]

        def stage_d(ln):
            mask, out_scr = directions[ln["d"]][2], directions[ln["d"]][5]
            p = jnp.where(mask, ln["scores"], 0.0).astype(BF16)
            lhs = jnp.concatenate([ln["q_dec"], p], axis=1)
            rhs = jnp.concatenate([ln["st"], ln["v"]], axis=0)
            out_scr[ln["slot"], pl.ds(ln["r0"], c), :] = _dot(lhs, rhs)

        for stage in (stage_a, stage_b, stage_c, stage_d):
            for ln in lanes:
                stage(ln)
        return tuple(states)

    def paired_trip(hs, states):
        lanes = [dict(slot=slot, h=h, n=n, r0=n * c)
                 for n in range(n_chunks) for slot, h in enumerate(hs)]
        states = list(states)
        zeros = jnp.zeros((c, D_HEAD), BF16)
        row2 = lax.broadcasted_iota(jnp.int32, (c, 2 * c), 0)
        col2 = lax.broadcasted_iota(jnp.int32, (c, 2 * c), 1)
        mask2 = ((col2 < c) & (col2 <= row2)) | ((col2 >= c) & (col2 - c >= row2))

        def stage_a(ln):
            rows = pl.ds(ln["r0"], c)
            ln["f"] = [ff_ref[0, ln["h"], rows, :], fb_ref[0, ln["h"], rows, :]]
            ln["cum"] = [_cumsum_rows(tri_lo, jnp.log2(ln["f"][0])),
                         _cumsum_rows(tri_up, jnp.log2(ln["f"][1]))]

        def stage_b(ln):
            rows = pl.ds(ln["r0"], c)
            q = q_ref[0, ln["h"], rows, :].astype(F32)
            ln["v"] = v_ref[0, ln["h"], rows, :]
            q_mid, k_mid, q_dec, k_state, ln["decay"] = [], [], [], [], []
            for d in range(2):
                mid, last = directions[d][3], directions[d][4]
                cum = ln["cum"][d]
                cum_mid = cum[mid:mid + 1]
                cum_last = cum[last:last + 1]
                qm = q * jnp.exp2(cum - cum_mid)
                km = (1.0 - ln["f"][d]) * jnp.exp2(cum_mid - cum)
                q_mid.append(qm.astype(BF16))
                k_mid.append(km.astype(BF16))
                q_dec.append((qm * jnp.exp2(cum_mid)).astype(BF16))
                k_state.append((km * jnp.exp2(cum_last - cum_mid)).astype(BF16))
                ln["decay"].append(jnp.exp2(cum_last))
            k_pair = jnp.concatenate([jnp.concatenate([k_mid[0], zeros], axis=1),
                                      jnp.concatenate([zeros, k_mid[1]], axis=1)], axis=0)
            ln["scores"] = _dot_nt(jnp.concatenate(q_mid, axis=1), k_pair)
            ln["kv"] = _dot_tn(ln["v"], jnp.concatenate(k_state, axis=1))
            ln["q_dec"] = jnp.concatenate(q_dec, axis=1)

        def stage_c(slot):
            mine = [ln for ln in lanes if ln["slot"] == slot]
            for d, order in ((0, mine), (1, mine[::-1])):
                st = states[2 * slot + d]
                for ln in order:
                    ln.setdefault("st", [None, None])[d] = st.astype(BF16).T
                    st = st * ln["decay"][d] + ln["kv"][:, d * D_HEAD:(d + 1) * D_HEAD]
                states[2 * slot + d] = st

        def stage_d(ln):
            rows = pl.ds(ln["r0"], c)
            p = jnp.where(mask2, ln["scores"], 0.0).astype(BF16)
            lhs = jnp.concatenate([ln["q_dec"], p], axis=1)
            rhs = jnp.concatenate([ln["st"][0], ln["st"][1], ln["v"], ln["v"]], axis=0)
            tot = _dot(lhs, rhs)
            gate = g_ref[0, ln["h"], rows, :].astype(F32)
            o_ref[0, ln["h"], rows, :] = (_rms(tot, gnw) * gate).astype(o_ref.dtype)

        for ln in lanes:
            stage_a(ln)
        for ln in lanes:
            stage_b(ln)
        for slot in range(len(hs)):
            stage_c(slot)
        for ln in lanes:
            stage_d(ln)
        return tuple(states)

    def head_group_body(hg, carry):
        hs = [hg * head_group + t for t in range(head_group)]
        states = []
        for h in hs:
            for d in range(2):
                if zero_init:
                    states.append(jnp.zeros((D_HEAD, D_HEAD), F32))
                else:
                    states.append(s0_ref[0, 0, d, h].T)
        states = tuple(states)
        n_trips = n_chunks // unroll
        if n_trips == 1:
            states = paired_trip(hs, states)
        else:
            states = lax.fori_loop(0, n_trips, functools.partial(trip, hs), states)

        eb = min(seq_len, ROW_TILE)
        for slot, h in enumerate(hs):
            if emit_state:
                sf_ref[0, 0, 0, h] = states[2 * slot].T
                sf_ref[0, 0, 1, h] = states[2 * slot + 1].T
            if n_trips == 1:
                continue

            def norm_body(j, carry2, slot=slot, h=h):
                r0 = pl.multiple_of(j * eb, eb)
                tot = of_scr[slot, pl.ds(r0, eb), :] + ob_scr[slot, pl.ds(r0, eb), :]
                gate = g_ref[0, h, pl.ds(r0, eb), :].astype(F32)
                o_ref[0, h, pl.ds(r0, eb), :] = (_rms(tot, gnw) * gate).astype(o_ref.dtype)
                return carry2

            if seq_len == eb:
                norm_body(0, 0)
            else:
                lax.fori_loop(0, seq_len // eb, norm_body, 0, unroll=4)
        return carry

    if heads == head_group:
        head_group_body(0, 0)
    else:
        lax.fori_loop(0, heads // head_group, head_group_body, 0)


def _gla(q, ff, fb, v, g, s0, gnw, emit_state, heads, head_group, unroll, casts):
    bsz, _, n, _ = q.shape
    n_hb = N_HEADS // heads
    steps = bsz * n_hb
    zero_init = s0 is None
    seq_spec = pl.BlockSpec((1, heads, n, D_HEAD), lambda b, h: (b, h, 0, 0))
    state_spec = pl.BlockSpec((1, 1, 2, heads, D_HEAD, D_HEAD), lambda b, h: (b, 0, 0, h, 0, 0))
    in_specs = [seq_spec] * 5
    args = [q, ff, fb, v, g]
    if not zero_init:
        in_specs.append(state_spec)
        args.append(s0)
    in_specs.append(_const_spec((1, D_HEAD)))
    args.append(gnw)
    out_specs = [seq_spec]
    out_shape = [jax.ShapeDtypeStruct(q.shape, BF16)]
    if emit_state:
        out_specs.append(state_spec)
        out_shape.append(jax.ShapeDtypeStruct((bsz, 1, 2, N_HEADS, D_HEAD, D_HEAD), F32))
    for w, layer in casts:
        _, rows, cols = w.shape
        slab = rows // steps
        in_specs.append(pl.BlockSpec((1, 1, slab, cols),
                                     lambda b, h, layer=layer: (layer, b * n_hb + h, 0, 0)))
        args.append(w.reshape(w.shape[0], steps, slab, cols))
        out_specs.append(pl.BlockSpec((1, slab, cols), lambda b, h: (b * n_hb + h, 0, 0)))
        out_shape.append(jax.ShapeDtypeStruct((steps, slab, cols), BF16))
    outs = pl.pallas_call(
        functools.partial(_gla_kernel, seq_len=n, heads=heads, head_group=head_group, unroll=unroll,
                          zero_init=zero_init, emit_state=emit_state, n_casts=len(casts)),
        grid=(bsz, n_hb),
        in_specs=in_specs,
        out_specs=out_specs,
        out_shape=out_shape,
        scratch_shapes=[pltpu.VMEM((head_group, n, D_HEAD), F32)] * 2,
        compiler_params=_params(2),
        name="gla",
    )(*args)
    n_main = 2 if emit_state else 1
    cast_out = [o.reshape(w.shape[1], w.shape[2]) for o, (w, _) in zip(outs[n_main:], casts)]
    return outs[0], (outs[1] if emit_state else None), cast_out


def _conv_ffn(x1s, mod_ref, nfw_ref, wup_ref, cw_ref, cb_ref, wdn_ref, hid_scr, period):
    t = x1s[0].shape[0]
    hs = [_modulate(x1, nfw_ref[...], mod_ref[0, 3:4, :], mod_ref[0, 4:5, :]).astype(BF16)
          for x1 in x1s]
    pos = lax.broadcasted_iota(jnp.int32, (t, 1), 0) % period
    first = pos == 0
    last = pos == period - 1
    for j in range(D_FF // FF_CHUNK):
        lo = j * FF_CHUNK
        for s, h in enumerate(hs):
            a = _dot(h, wup_ref[:, lo:lo + FF_CHUNK])
            g = _dot(h, wup_ref[:, D_FF + lo:D_FF + lo + FF_CHUNK])
            prev = jnp.where(first, 0.0, pltpu.roll(a, 1, 0))
            nxt = jnp.where(last, 0.0, pltpu.roll(a, t - 1, 0))
            ac = (prev * cw_ref[0:1, lo:lo + FF_CHUNK] + a * cw_ref[1:2, lo:lo + FF_CHUNK]
                  + nxt * cw_ref[2:3, lo:lo + FF_CHUNK] + cb_ref[:, lo:lo + FF_CHUNK])
            hid_scr[s * t:(s + 1) * t, lo:lo + FF_CHUNK] = (_silu(ac) * g).astype(BF16)
    return [x1 + mod_ref[0, 5:6, :] * _dot(hid_scr[s * t:(s + 1) * t, :], wdn_ref[...])
            for s, x1 in enumerate(x1s)]


def _tile(n):
    return (1, TOKEN_TILE) if n >= TOKEN_TILE else (TOKEN_TILE // n, n)


def _rows(ref):
    v = ref[...]
    return v.reshape(v.shape[0] * v.shape[1], v.shape[2])


def _halves(ref):
    tb, tl, _ = ref.shape
    half = tb * tl // 2
    out = []
    for s in range(2):
        bi, r0 = divmod(s * half, tl)
        out.append((bi, r0, ref[bi, r0:r0 + half, :]))
    return out


def _ffn_specs():
    return [
        _const_spec((1, D_MODEL)),
        _const_spec((D_MODEL, 2 * D_FF)),
        _const_spec((3, D_FF)),
        _const_spec((1, D_FF)),
        _const_spec((D_FF, D_MODEL)),
    ]


def _post0_kernel(x_ref, o_ref, mod_ref, wo_ref, nfw_ref, wup_ref, cw_ref, cb_ref, wdn_ref,
                  mod1_ref, nw1_ref, w1_ref, out_ref, gb_ref, cu_ref, hid_scr, *, period):
    x1s = []
    for bi, r0, x in _halves(x_ref):
        half = x.shape[0]
        o = jnp.concatenate([o_ref[bi, hh, r0:r0 + half, :] for hh in range(N_HEADS)], axis=1)
        x1s.append(x + mod_ref[0, 2:3, :] * _dot(o, wo_ref[...]))
    x2s = _conv_ffn(x1s, mod_ref, nfw_ref, wup_ref, cw_ref, cb_ref, wdn_ref, hid_scr, period)
    hs = [_modulate(x2, nw1_ref[...], mod1_ref[0, 0:1, :], mod1_ref[0, 1:2, :]).astype(BF16)
          for x2 in x2s]
    where = [(bi, r0) for bi, r0, _ in _halves(x_ref)]
    for (bi, r0), x2, h in zip(where, x2s, hs):
        half = x2.shape[0]
        out_ref[bi, r0:r0 + half, :] = x2
        gb_ref[bi, r0:r0 + half, :] = _dot(h, w1_ref[:, 0:D_MODEL]).astype(gb_ref.dtype)
    for (bi, r0), h in zip(where, hs):
        half = h.shape[0]
        gate_c = _dot(h, w1_ref[:, D_MODEL:2 * D_MODEL])
        u = _dot(h, w1_ref[:, 2 * D_MODEL:3 * D_MODEL])
        cu_ref[bi, r0:r0 + half, :] = (gate_c * u).astype(cu_ref.dtype)


def _post0(x, o, mod, wo, nfw, wup, cw, cb, wdn, mod1, nw1, w1, per_seq_mod, period):
    bsz, n, _ = x.shape
    tb, tl = _tile(n)
    mod_map = (lambda b, i: (b, 0, 0)) if per_seq_mod else (lambda b, i: (0, 0, 0))
    tok_spec = pl.BlockSpec((tb, tl, D_MODEL), lambda b, i: (b, i, 0))
    return pl.pallas_call(
        functools.partial(_post0_kernel, period=period),
        grid=(bsz // tb, n // tl),
        in_specs=[
            tok_spec,
            pl.BlockSpec((tb, N_HEADS, tl, D_HEAD), lambda b, i: (b, 0, i, 0)),
            pl.BlockSpec((1, N_MOD, D_MODEL), mod_map),
            _const_spec((D_MODEL, D_MODEL)),
        ] + _ffn_specs() + [
            pl.BlockSpec((1, N_MOD, D_MODEL), mod_map),
            _const_spec((1, D_MODEL)),
            _const_spec(w1.shape),
        ],
        out_specs=[tok_spec] * 3,
        out_shape=[jax.ShapeDtypeStruct(x.shape, F32), jax.ShapeDtypeStruct(x.shape, BF16),
                   jax.ShapeDtypeStruct(x.shape, BF16)],
        scratch_shapes=[pltpu.VMEM((tb * tl, D_FF), BF16)],
        compiler_params=_params(2),
        name="post0",
    )(x, o, mod, wo, nfw, wup, cw, cb, wdn, mod1, nw1, w1)


def _post1_kernel(*refs, grid_rows, period):
    if grid_rows:
        (x_ref, gb_ref, cu_ref, cup_ref, cun_ref, mod_ref, scw_ref, scb_ref, wo_ref,
         nfw_ref, wup_ref, cw_ref, cb_ref, wdn_ref, fnw_ref, out_ref, hid_scr) = refs
    else:
        (x_ref, gb_ref, cu_ref, mod_ref, scw_ref, scb_ref, wo_ref,
         nfw_ref, wup_ref, cw_ref, cb_ref, wdn_ref, fnw_ref, out_ref, hid_scr) = refs
    cu = _rows(cu_ref).astype(F32)
    t = cu.shape[0]
    if grid_rows:
        i = pl.program_id(1)
        halo_p = jnp.where(i == 0, 0.0, cup_ref[0].astype(F32))
        halo_n = jnp.where(i == pl.num_programs(1) - 1, 0.0, cun_ref[0].astype(F32))
        prev = jnp.concatenate([halo_p, cu[:t - GRID_W]], axis=0)
        nxt = jnp.concatenate([cu[GRID_W:], halo_n], axis=0)
    else:
        pos = lax.broadcasted_iota(jnp.int32, (t, 1), 0) % period
        prev = jnp.where(pos == 0, 0.0, pltpu.roll(cu, 1, 0))
        nxt = jnp.where(pos == period - 1, 0.0, pltpu.roll(cu, t - 1, 0))
    z = prev * scw_ref[0:1, :] + cu * scw_ref[1:2, :] + nxt * scw_ref[2:3, :] + scb_ref[...]
    m = (_rows(gb_ref).astype(F32) * z).astype(BF16)
    halves = _halves(x_ref)
    half = t // 2
    x1s = [x + mod_ref[0, 2:3, :] * _dot(m[s * half:(s + 1) * half], wo_ref[...])
           for s, (_, _, x) in enumerate(halves)]
    x2s = _conv_ffn(x1s, mod_ref, nfw_ref, wup_ref, cw_ref, cb_ref, wdn_ref, hid_scr, period)
    for (bi, r0, _), x2 in zip(halves, x2s):
        out_ref[bi, r0:r0 + half, :] = _rms(x2, fnw_ref[...])


def _post1(x, gb, cu, mod, scw, scb, wo, nfw, wup, cw, cb, wdn, fnw, per_seq_mod, grid_rows, period):
    bsz, n, _ = x.shape
    tb, tl = _tile(n)
    mod_map = (lambda b, i: (b, 0, 0)) if per_seq_mod else (lambda b, i: (0, 0, 0))
    tok_spec = pl.BlockSpec((tb, tl, D_MODEL), lambda b, i: (b, i, 0))
    in_specs = [tok_spec, tok_spec, tok_spec]
    args = [x, gb, cu]
    if grid_rows:
        r = tl // GRID_W
        n_rows = n // GRID_W
        in_specs += [
            pl.BlockSpec((1, GRID_W, D_MODEL), lambda b, i: (b, jnp.maximum(i * r - 1, 0), 0)),
            pl.BlockSpec((1, GRID_W, D_MODEL), lambda b, i: (b, jnp.minimum((i + 1) * r, n_rows - 1), 0)),
        ]
        args += [cu, cu]
    in_specs += [
        pl.BlockSpec((1, N_MOD, D_MODEL), mod_map),
        _const_spec((3, D_MODEL)),
        _const_spec((1, D_MODEL)),
        _const_spec((D_MODEL, D_MODEL)),
    ] + _ffn_specs() + [_const_spec((1, D_MODEL))]
    args += [mod, scw, scb, wo, nfw, wup, cw, cb, wdn, fnw]
    return pl.pallas_call(
        functools.partial(_post1_kernel, grid_rows=grid_rows, period=period),
        grid=(bsz // tb, n // tl),
        in_specs=in_specs,
        out_specs=tok_spec,
        out_shape=jax.ShapeDtypeStruct(x.shape, F32),
        scratch_shapes=[pltpu.VMEM((tb * tl, D_FF), BF16)],
        compiler_params=_params(2),
        name="post1",
    )(*args)


def kernel(x_prompt, x_sample, state_hgrn, c, c_ctx, ada_w, ada_b, norm_mix_w, norm_ffn_w,
           hgrn_w_in, hgrn_lower_bounds, hgrn_gnorm_w, hgrn_w_out,
           sconv_w_in, sconv_conv_w, sconv_conv_b, sconv_w_out,
           ffn_w_up, ffn_conv_w, ffn_conv_b, ffn_w_down, final_norm_w):
    seq_p = x_prompt.shape[1]

    cvecs = jnp.concatenate([c_ctx[None, :], c], axis=0)
    mod = _ada(cvecs, ada_w, ada_b)
    mod = mod.reshape(mod.shape[0], 8, N_MOD, D_MODEL)
    mod_p = mod[:, 0:1]
    mod_s = mod[:, 1:1 + c.shape[0]]

    row = lambda a: a.reshape(1, -1)
    w_in0 = hgrn_w_in[0].astype(BF16)
    gnw = row(hgrn_gnorm_w[0])
    lbp = hgrn_lower_bounds.reshape(-1, hgrn_lower_bounds.shape[-1])

    def mixer0(x, mods, s0, per_seq_mod, emit_state, gla_heads, casts):
        q, ff, fb, v, g = _inproj0(x, mods[0], row(norm_mix_w[0]), w_in0, lbp, per_seq_mod)
        n_chunks = x.shape[1] // SCAN_CHUNK
        unroll = min(n_chunks, SCAN_LANES // 2)
        head_group = min(gla_heads, SCAN_LANES // (2 * unroll))
        return _gla(q, ff, fb, v, g, s0, gnw, emit_state, gla_heads, head_group, unroll, casts)

    def post0(x, o, mods, per_seq_mod, ffn_period):
        return _post0(x, o, mods[0], w_out0, row(norm_ffn_w[0]), w_up0, ffn_conv_w[0],
                      row(ffn_conv_b[0]), w_dn0, mods[1], row(norm_mix_w[1]), w_in1,
                      per_seq_mod, ffn_period)

    def post1(x, gb, cu, mods, per_seq_mod, grid_rows, ffn_period):
        return _post1(x, gb, cu, mods[1], sconv_conv_w[0], row(sconv_conv_b[0]), w_out1,
                      row(norm_ffn_w[1]), w_up1, ffn_conv_w[1], row(ffn_conv_b[1]), w_dn1,
                      row(final_norm_w), per_seq_mod, grid_rows, ffn_period)

    o_p, new_state, (w_out0, w_up0, w_dn0, w_in1) = mixer0(
        x_prompt, mod_p, None, False, True, N_HEADS,
        [(hgrn_w_out, 0), (ffn_w_up, 0), (ffn_w_down, 0), (sconv_w_in, 0)])
    xp, gb_p, cu_p = post0(x_prompt, o_p, mod_p, False, seq_p)
    o_s, _, (w_out1, w_up1, w_dn1) = mixer0(
        x_sample, mod_s, state_hgrn, True, False, 1,
        [(sconv_w_out, 0), (ffn_w_up, 1), (ffn_w_down, 1)])
    y_prompt = post1(xp, gb_p, cu_p, mod_p, False, False, seq_p)
    xs, gb_s, cu_s = post0(x_sample, o_s, mod_s, True, GRID_W)
    y_sample = post1(xs, gb_s, cu_s, mod_s, True, True, GRID_W)
    return (y_prompt, y_sample, new_state.astype(state_hgrn.dtype))
```

```python
import functools

import jax
import jax.numpy as jnp
from jax import lax
from jax.experimental import pallas as pl
from jax.experimental.pallas import tpu as pltpu

D_MODEL = 1024
N_HEADS = 8
D_HEAD = 128
D_FF = 2816
N_MOD = 6
GRID_W = 64
EPS = 1e-6

LANES = 128
SCAN_CHUNK = 64
SCAN_LANES = 128
ROW_TILE = 256
TOKEN_TILE = 512
FF_CHUNK = 256
ADA_ROWS = 256
VMEM_LIMIT = 56 * 1024 * 1024

F32 = jnp.float32
BF16 = jnp.bfloat16


def _sigmoid(x):
    return 1.0 / (1.0 + jnp.exp(-x))


def _silu(x):
    return x * _sigmoid(x)


def _rms(x, w):
    return x * lax.rsqrt(jnp.mean(x * x, axis=-1, keepdims=True) + EPS) * w


def _modulate(x, w, shift, scale):
    return _rms(x, w) * (1.0 + scale) + shift


def _dot(a, b):
    return jnp.dot(a, b, preferred_element_type=F32)


def _dot_nt(a, b):
    return lax.dot_general(a, b, (((1,), (1,)), ((), ())), preferred_element_type=F32)


def _dot_tn(a, b):
    return lax.dot_general(a, b, (((0,), (0,)), ((), ())), preferred_element_type=F32)


def _params(n_grid):
    return pltpu.CompilerParams(
        dimension_semantics=("arbitrary",) * n_grid,
        vmem_limit_bytes=VMEM_LIMIT,
    )


def _const_spec(shape):
    nd = len(shape)
    return pl.BlockSpec(shape, lambda *_: (0,) * nd, pipeline_mode=pl.Buffered(1))


def _cast_riders(casts, steps, step_of):
    in_specs, args, out_specs, out_shape = [], [], [], []
    for w, layer in casts:
        _, rows, cols = w.shape
        slab = rows // steps
        in_specs.append(pl.BlockSpec((1, 1, slab, cols),
                                     lambda *g, layer=layer: (layer, step_of(*g), 0, 0)))
        args.append(w.reshape(w.shape[0], steps, slab, cols))
        out_specs.append(pl.BlockSpec((1, slab, cols), lambda *g: (step_of(*g), 0, 0)))
        out_shape.append(jax.ShapeDtypeStruct((steps, slab, cols), BF16))
    return in_specs, args, out_specs, out_shape


def _cast_slabs(cast_in, cast_out):
    for src, dst in zip(cast_in, cast_out):
        dst[0] = src[0, 0].astype(dst.dtype)


def _cast_results(outs, casts):
    return [o.reshape(w.shape[1], w.shape[2]) for o, (w, _) in zip(outs, casts)]


def _ada_kernel(c_ref, wa_ref, wb_ref, b_ref, o_ref, s_scr):
    k = pl.program_id(1)
    rows, n_out = wa_ref.shape[1], wa_ref.shape[2]

    @pl.when((pl.program_id(0) == 0) & (k == 0))
    def _():
        s_scr[...] = _silu(c_ref[...])

    @pl.when(k == 0)
    def _():
        o_ref[0] = jnp.broadcast_to(b_ref[0], o_ref.shape[1:])

    for half, w_ref in enumerate((wa_ref, wb_ref)):
        r0 = pl.multiple_of((2 * k + half) * rows, rows)
        for r in range(c_ref.shape[0]):
            s = s_scr[r, pl.ds(r0, rows), :]
            parts = [
                jnp.sum(w_ref[0, :, j * LANES:(j + 1) * LANES] * s, axis=0, keepdims=True)
                for j in range(n_out // LANES)
            ]
            o_ref[0, r:r + 1, :] += jnp.concatenate(parts, axis=1)


def _ada(cvecs, ada_w, ada_b):
    depth, d_in, n_out = ada_w.shape
    cb = jnp.broadcast_to(cvecs[:, :, None], cvecs.shape + (LANES,))
    return pl.pallas_call(
        _ada_kernel,
        grid=(depth, d_in // ADA_ROWS),
        in_specs=[
            pl.BlockSpec(cb.shape, lambda l, k: (0, 0, 0)),
            pl.BlockSpec((1, ADA_ROWS // 2, n_out), lambda l, k: (l, 2 * k, 0)),
            pl.BlockSpec((1, ADA_ROWS // 2, n_out), lambda l, k: (l, 2 * k + 1, 0)),
            pl.BlockSpec((1, 1, n_out), lambda l, k: (l, 0, 0)),
        ],
        out_specs=pl.BlockSpec((1, 8, n_out), lambda l, k: (l, 0, 0)),
        out_shape=jax.ShapeDtypeStruct((depth, 8, n_out), F32),
        scratch_shapes=[pltpu.VMEM(cb.shape, F32)],
        compiler_params=_params(2),
        name="ada",
    )(cb, ada_w, ada_w, ada_b.reshape(depth, 1, n_out))


def _inproj0_kernel(x_ref, mod_ref, nw_ref, w_ref, lbp_ref, *refs, n_casts):
    cast_in, (q_ref, ff_ref, fb_ref, v_ref, g_ref) = refs[:n_casts], refs[n_casts:n_casts + 5]
    _cast_slabs(cast_in, refs[n_casts + 5:])
    x = _rows(x_ref)
    tb, _, tl, _ = q_ref.shape
    half = x.shape[0] // 2
    hs = [_modulate(x[s * half:(s + 1) * half], nw_ref[...], mod_ref[0, 0:1, :],
                    mod_ref[0, 1:2, :]).astype(BF16) for s in range(2)]

    n_lb = lbp_ref.shape[0] // 2

    def lower_bound(d):
        rows = [lbp_ref[d * n_lb + i:d * n_lb + i + 1, :] for i in range(n_lb)]
        top = functools.reduce(jnp.maximum, rows)
        e = [jnp.exp(r - top) for r in rows]
        return e[0] / functools.reduce(jnp.add, e)

    lb = [lower_bound(0), lower_bound(1)]

    def emit(j, ref, act):
        vals = [_dot(h, w_ref[:, j * D_MODEL:(j + 1) * D_MODEL]) for h in hs]
        for s, val in enumerate(vals):
            val = act(val)
            b, r0 = divmod(s * half, tl)
            for hh in range(N_HEADS):
                ref[b, hh, r0:r0 + half, :] = val[:, hh * D_HEAD:(hh + 1) * D_HEAD].astype(ref.dtype)

    emit(0, q_ref, _silu)
    emit(1, ff_ref, lambda p: lb[0] + (1.0 - lb[0]) * _sigmoid(p))
    emit(2, fb_ref, lambda p: lb[1] + (1.0 - lb[1]) * _sigmoid(p))
    emit(3, v_ref, lambda p: p)
    emit(4, g_ref, _silu)


def _inproj0(x, mod, nw, w_in, lbp, per_seq_mod, casts):
    bsz, n, _ = x.shape
    tb, tl = _tile(n)
    n_i = n // tl
    mod_map = (lambda b, i: (b, 0, 0)) if per_seq_mod else (lambda b, i: (0, 0, 0))
    head_spec = pl.BlockSpec((tb, N_HEADS, tl, D_HEAD), lambda b, i: (b, 0, i, 0))
    shp = (bsz, N_HEADS, n, D_HEAD)
    c_in, c_args, c_out, c_shape = _cast_riders(casts, (bsz // tb) * n_i, lambda b, i: b * n_i + i)
    outs = pl.pallas_call(
        functools.partial(_inproj0_kernel, n_casts=len(casts)),
        grid=(bsz // tb, n_i),
        in_specs=[
            pl.BlockSpec((tb, tl, D_MODEL), lambda b, i: (b, i, 0)),
            pl.BlockSpec((1, N_MOD, D_MODEL), mod_map),
            _const_spec((1, D_MODEL)),
            _const_spec(w_in.shape),
            _const_spec(lbp.shape),
        ] + c_in,
        out_specs=[head_spec] * 5 + c_out,
        out_shape=[
            jax.ShapeDtypeStruct(shp, BF16),
            jax.ShapeDtypeStruct(shp, F32),
            jax.ShapeDtypeStruct(shp, F32),
            jax.ShapeDtypeStruct(shp, BF16),
            jax.ShapeDtypeStruct(shp, BF16),
        ] + c_shape,
        compiler_params=_params(2),
        name="inproj0",
    )(x, mod, nw, w_in, lbp, *c_args)
    return outs[:5], _cast_results(outs[5:], casts)


def _cumsum_rows(tri, x):
    hi = x.astype(BF16)
    lo = (x - hi.astype(F32)).astype(BF16)
    s = _dot(tri, jnp.concatenate([hi, lo], axis=1))
    return s[:, :D_HEAD] + s[:, D_HEAD:]


def _gla_kernel(*refs, seq_len, heads, head_group, unroll, zero_init, emit_state, n_casts):
    q_ref, ff_ref, fb_ref, v_ref, g_ref = refs[:5]
    pos = 5
    s0_ref = None
    if not zero_init:
        s0_ref = refs[pos]
        pos += 1
    gnw_ref = refs[pos]
    cast_in = refs[pos + 1:pos + 1 + n_casts]
    pos += 1 + n_casts
    o_ref = refs[pos]
    pos += 1
    sf_ref = None
    if emit_state:
        sf_ref = refs[pos]
        pos += 1
    cast_out = refs[pos:pos + n_casts]
    pos += n_casts
    of_scr, ob_scr = refs[pos], refs[pos + 1]

    _cast_slabs(cast_in, cast_out)

    c = SCAN_CHUNK
    n_chunks = seq_len // c
    row = lax.broadcasted_iota(jnp.int32, (c, c), 0)
    col = lax.broadcasted_iota(jnp.int32, (c, c), 1)
    lower = col <= row
    upper = col >= row
    tri_lo = lower.astype(BF16)
    tri_up = upper.astype(BF16)
    gnw = gnw_ref[...]

    directions = (
        (ff_ref, tri_lo, lower, c // 2 - 1, c - 1, of_scr),
        (fb_ref, tri_up, upper, c // 2, 0, ob_scr),
    )

    def trip(hs, i, states):
        lanes = []
        for u in range(unroll):
            for slot, h in enumerate(hs):
                for d in range(2):
                    n = i * unroll + u
                    if d == 1:
                        n = n_chunks - 1 - n
                    r0 = n * c if isinstance(n, int) else pl.multiple_of(n * c, c)
                    lanes.append(dict(slot=slot, h=h, d=d, r0=r0, chain=2 * slot + d))
        states = list(states)

        def stage_a(ln):
            f_ref, tri = directions[ln["d"]][:2]
            ln["f"] = f_ref[0, ln["h"], pl.ds(ln["r0"], c), :]
            ln["cum"] = _cumsum_rows(tri, jnp.log2(ln["f"]))

        def stage_b(ln):
            _, _, _, mid, last, _ = directions[ln["d"]]
            cum = ln["cum"]
            cum_mid = cum[mid:mid + 1]
            cum_last = cum[last:last + 1]
            k = 1.0 - ln["f"]
            q = q_ref[0, ln["h"], pl.ds(ln["r0"], c), :].astype(F32)
            ln["v"] = v_ref[0, ln["h"], pl.ds(ln["r0"], c), :]
            q_mid = q * jnp.exp2(cum - cum_mid)
            k_mid = k * jnp.exp2(cum_mid - cum)
            ln["q_dec"] = (q_mid * jnp.exp2(cum_mid)).astype(BF16)
            k_state = (k_mid * jnp.exp2(cum_last - cum_mid)).astype(BF16)
            ln["scores"] = _dot_nt(q_mid.astype(BF16), k_mid.astype(BF16))
            ln["kv"] = _dot_tn(ln["v"], k_state)
            ln["decay"] = jnp.exp2(cum_last)

        def stage_c(ln):
            st = states[ln["chain"]]
            ln["st"] = st.astype(BF16).T
            states[ln["chain"]] = st * ln["decay"] + ln["kv"]

        def stage_d(ln):
            mask, out_scr = directions[ln["d"]][2], directions[ln["d"]][5]
            p = jnp.where(mask, ln["scores"], 0.0).astype(BF16)
            lhs = jnp.concatenate([ln["q_dec"], p], axis=1)
            rhs = jnp.concatenate([ln["st"], ln["v"]], axis=0)
            out_scr[ln["slot"], pl.ds(ln["r0"], c), :] = _dot(lhs, rhs)

        for stage in (stage_a, stage_b, stage_c, stage_d):
            for ln in lanes:
                stage(ln)
        return tuple(states)

    def paired_trip(hs, states):
        lanes = [dict(slot=slot, h=h, n=n, r0=n * c)
                 for n in range(n_chunks) for slot, h in enumerate(hs)]
        states = list(states)
        zeros = jnp.zeros((c, D_HEAD), BF16)
        row2 = lax.broadcasted_iota(jnp.int32, (c, 2 * c), 0)
        col2 = lax.broadcasted_iota(jnp.int32, (c, 2 * c), 1)
        mask2 = ((col2 < c) & (col2 <= row2)) | ((col2 >= c) & (col2 - c >= row2))

        def stage_a(ln):
            rows = pl.ds(ln["r0"], c)
            ln["f"] = [ff_ref[0, ln["h"], rows, :], fb_ref[0, ln["h"], rows, :]]
            ln["cum"] = [_cumsum_rows(tri_lo, jnp.log2(ln["f"][0])),
                         _cumsum_rows(tri_up, jnp.log2(ln["f"][1]))]

        def stage_b(ln):
            rows = pl.ds(ln["r0"], c)
            q = q_ref[0, ln["h"], rows, :].astype(F32)
            ln["v"] = v_ref[0, ln["h"], rows, :]
            q_mid, k_mid, q_dec, k_state, ln["decay"] = [], [], [], [], []
            for d in range(2):
                mid, last = directions[d][3], directions[d][4]
                cum = ln["cum"][d]
                cum_mid = cum[mid:mid + 1]
                cum_last = cum[last:last + 1]
                qm = q * jnp.exp2(cum - cum_mid)
                km = (1.0 - ln["f"][d]) * jnp.exp2(cum_mid - cum)
                q_mid.append(qm.astype(BF16))
                k_mid.append(km.astype(BF16))
                q_dec.append((qm * jnp.exp2(cum_mid)).astype(BF16))
                k_state.append((km * jnp.exp2(cum_last - cum_mid)).astype(BF16))
                ln["decay"].append(jnp.exp2(cum_last))
            k_pair = jnp.concatenate([jnp.concatenate([k_mid[0], zeros], axis=1),
                                      jnp.concatenate([zeros, k_mid[1]], axis=1)], axis=0)
            ln["scores"] = _dot_nt(jnp.concatenate(q_mid, axis=1), k_pair)
            ln["kv"] = _dot_tn(ln["v"], jnp.concatenate(k_state, axis=1))
            ln["q_dec"] = jnp.concatenate(q_dec, axis=1)

        def stage_c(slot):
            mine = [ln for ln in lanes if ln["slot"] == slot]
            for d, order in ((0, mine), (1, mine[::-1])):
                st = states[2 * slot + d]
                for ln in order:
                    ln.setdefault("st", [None, None])[d] = st.astype(BF16).T
                    st = st * ln["decay"][d] + ln["kv"][:, d * D_HEAD:(d + 1) * D_HEAD]
                states[2 * slot + d] = st

        def stage_d(ln):
            rows = pl.ds(ln["r0"], c)
            p = jnp.where(mask2, ln["scores"], 0.0).astype(BF16)
            lhs = jnp.concatenate([ln["q_dec"], p], axis=1)
            rhs = jnp.concatenate([ln["st"][0], ln["st"][1], ln["v"], ln["v"]], axis=0)
            tot = _dot(lhs, rhs)
            gate = g_ref[0, ln["h"], rows, :].astype(F32)
            o_ref[0, ln["h"], rows, :] = (_rms(tot, gnw) * gate).astype(o_ref.dtype)

        for ln in lanes:
            stage_a(ln)
        for ln in lanes:
            stage_b(ln)
        for slot in range(len(hs)):
            stage_c(slot)
        for ln in lanes:
            stage_d(ln)
        return tuple(states)

    def head_group_body(hg, carry):
        hs = [hg * head_group + t for t in range(head_group)]
        states = []
        for h in hs:
            for d in range(2):
                if zero_init:
                    states.append(jnp.zeros((D_HEAD, D_HEAD), F32))
                else:
                    states.append(s0_ref[0, 0, d, h].T)
        states = tuple(states)
        n_trips = n_chunks // unroll
        if n_trips == 1:
            states = paired_trip(hs, states)
        else:
            states = lax.fori_loop(0, n_trips, functools.partial(trip, hs), states)

        eb = min(seq_len, ROW_TILE)
        for slot, h in enumerate(hs):
            if emit_state:
                sf_ref[0, 0, 0, h] = states[2 * slot].T
                sf_ref[0, 0, 1, h] = states[2 * slot + 1].T
            if n_trips == 1:
                continue

            def norm_body(j, carry2, slot=slot, h=h):
                r0 = pl.multiple_of(j * eb, eb)
                tot = of_scr[slot, pl.ds(r0, eb), :] + ob_scr[slot, pl.ds(r0, eb), :]
                gate = g_ref[0, h, pl.ds(r0, eb), :].astype(F32)
                o_ref[0, h, pl.ds(r0, eb), :] = (_rms(tot, gnw) * gate).astype(o_ref.dtype)
                return carry2

            if seq_len == eb:
                norm_body(0, 0)
            else:
                lax.fori_loop(0, seq_len // eb, norm_body, 0, unroll=4)
        return carry

    if heads == head_group:
        head_group_body(0, 0)
    else:
        lax.fori_loop(0, heads // head_group, head_group_body, 0)


def _gla(q, ff, fb, v, g, s0, gnw, emit_state, heads, head_group, unroll, casts):
    bsz, _, n, _ = q.shape
    n_hb = N_HEADS // heads
    steps = bsz * n_hb
    zero_init = s0 is None
    seq_spec = pl.BlockSpec((1, heads, n, D_HEAD), lambda b, h: (b, h, 0, 0))
    state_spec = pl.BlockSpec((1, 1, 2, heads, D_HEAD, D_HEAD), lambda b, h: (b, 0, 0, h, 0, 0))
    in_specs = [seq_spec] * 5
    args = [q, ff, fb, v, g]
    if not zero_init:
        in_specs.append(state_spec)
        args.append(s0)
    in_specs.append(_const_spec((1, D_HEAD)))
    args.append(gnw)
    out_specs = [seq_spec]
    out_shape = [jax.ShapeDtypeStruct(q.shape, BF16)]
    if emit_state:
        out_specs.append(state_spec)
        out_shape.append(jax.ShapeDtypeStruct((bsz, 1, 2, N_HEADS, D_HEAD, D_HEAD), F32))
    c_in, c_args, c_out, c_shape = _cast_riders(casts, steps, lambda b, h: b * n_hb + h)
    in_specs += c_in
    args += c_args
    out_specs += c_out
    out_shape += c_shape
    outs = pl.pallas_call(
        functools.partial(_gla_kernel, seq_len=n, heads=heads, head_group=head_group, unroll=unroll,
                          zero_init=zero_init, emit_state=emit_state, n_casts=len(casts)),
        grid=(bsz, n_hb),
        in_specs=in_specs,
        out_specs=out_specs,
        out_shape=out_shape,
        scratch_shapes=[pltpu.VMEM((head_group, n, D_HEAD), F32)] * 2,
        compiler_params=_params(2),
        name="gla",
    )(*args)
    n_main = 2 if emit_state else 1
    return outs[0], (outs[1] if emit_state else None), _cast_results(outs[n_main:], casts)


def _conv_ffn(x1s, mod_ref, nfw_ref, wup_ref, cw_ref, cb_ref, wdn_ref, hid_scr, period):
    t = x1s[0].shape[0]
    hs = [_modulate(x1, nfw_ref[...], mod_ref[0, 3:4, :], mod_ref[0, 4:5, :]).astype(BF16)
          for x1 in x1s]
    pos = lax.broadcasted_iota(jnp.int32, (t, 1), 0) % period
    first = pos == 0
    last = pos == period - 1
    for j in range(D_FF // FF_CHUNK):
        lo = j * FF_CHUNK
        for s, h in enumerate(hs):
            a = _dot(h, wup_ref[:, lo:lo + FF_CHUNK])
            g = _dot(h, wup_ref[:, D_FF + lo:D_FF + lo + FF_CHUNK])
            prev = jnp.where(first, 0.0, pltpu.roll(a, 1, 0))
            nxt = jnp.where(last, 0.0, pltpu.roll(a, t - 1, 0))
            ac = (prev * cw_ref[0:1, lo:lo + FF_CHUNK] + a * cw_ref[1:2, lo:lo + FF_CHUNK]
                  + nxt * cw_ref[2:3, lo:lo + FF_CHUNK] + cb_ref[:, lo:lo + FF_CHUNK])
            hid_scr[s * t:(s + 1) * t, lo:lo + FF_CHUNK] = (_silu(ac) * g).astype(BF16)
    return [x1 + mod_ref[0, 5:6, :] * _dot(hid_scr[s * t:(s + 1) * t, :], wdn_ref[...])
            for s, x1 in enumerate(x1s)]


def _tile(n):
    return (1, TOKEN_TILE) if n >= TOKEN_TILE else (TOKEN_TILE // n, n)


def _rows(ref):
    v = ref[...]
    return v.reshape(v.shape[0] * v.shape[1], v.shape[2])


def _halves(ref):
    tb, tl, _ = ref.shape
    half = tb * tl // 2
    out = []
    for s in range(2):
        bi, r0 = divmod(s * half, tl)
        out.append((bi, r0, ref[bi, r0:r0 + half, :]))
    return out


def _ffn_specs():
    return [
        _const_spec((1, D_MODEL)),
        _const_spec((D_MODEL, 2 * D_FF)),
        _const_spec((3, D_FF)),
        _const_spec((1, D_FF)),
        _const_spec((D_FF, D_MODEL)),
    ]


def _post0_kernel(x_ref, o_ref, mod_ref, wo_ref, nfw_ref, wup_ref, cw_ref, cb_ref, wdn_ref,
                  mod1_ref, nw1_ref, w1_ref, out_ref, gb_ref, cu_ref, hid_scr, *, period):
    x1s = []
    for bi, r0, x in _halves(x_ref):
        half = x.shape[0]
        o = jnp.concatenate([o_ref[bi, hh, r0:r0 + half, :] for hh in range(N_HEADS)], axis=1)
        x1s.append(x + mod_ref[0, 2:3, :] * _dot(o, wo_ref[...]))
    x2s = _conv_ffn(x1s, mod_ref, nfw_ref, wup_ref, cw_ref, cb_ref, wdn_ref, hid_scr, period)
    hs = [_modulate(x2, nw1_ref[...], mod1_ref[0, 0:1, :], mod1_ref[0, 1:2, :]).astype(BF16)
          for x2 in x2s]
    where = [(bi, r0) for bi, r0, _ in _halves(x_ref)]
    for (bi, r0), x2, h in zip(where, x2s, hs):
        half = x2.shape[0]
        out_ref[bi, r0:r0 + half, :] = x2
        gb_ref[bi, r0:r0 + half, :] = _dot(h, w1_ref[:, 0:D_MODEL]).astype(gb_ref.dtype)
    for (bi, r0), h in zip(where, hs):
        half = h.shape[0]
        gate_c = _dot(h, w1_ref[:, D_MODEL:2 * D_MODEL])
        u = _dot(h, w1_ref[:, 2 * D_MODEL:3 * D_MODEL])
        cu_ref[bi, r0:r0 + half, :] = (gate_c * u).astype(cu_ref.dtype)


def _post0(x, o, mod, wo, nfw, wup, cw, cb, wdn, mod1, nw1, w1, per_seq_mod, period):
    bsz, n, _ = x.shape
    tb, tl = _tile(n)
    mod_map = (lambda b, i: (b, 0, 0)) if per_seq_mod else (lambda b, i: (0, 0, 0))
    tok_spec = pl.BlockSpec((tb, tl, D_MODEL), lambda b, i: (b, i, 0))
    return pl.pallas_call(
        functools.partial(_post0_kernel, period=period),
        grid=(bsz // tb, n // tl),
        in_specs=[
            tok_spec,
            pl.BlockSpec((tb, N_HEADS, tl, D_HEAD), lambda b, i: (b, 0, i, 0)),
            pl.BlockSpec((1, N_MOD, D_MODEL), mod_map),
            _const_spec((D_MODEL, D_MODEL)),
        ] + _ffn_specs() + [
            pl.BlockSpec((1, N_MOD, D_MODEL), mod_map),
            _const_spec((1, D_MODEL)),
            _const_spec(w1.shape),
        ],
        out_specs=[tok_spec] * 3,
        out_shape=[jax.ShapeDtypeStruct(x.shape, F32), jax.ShapeDtypeStruct(x.shape, BF16),
                   jax.ShapeDtypeStruct(x.shape, BF16)],
        scratch_shapes=[pltpu.VMEM((tb * tl, D_FF), BF16)],
        compiler_params=_params(2),
        name="post0",
    )(x, o, mod, wo, nfw, wup, cw, cb, wdn, mod1, nw1, w1)


def _post1_kernel(*refs, grid_rows, period):
    if grid_rows:
        (x_ref, gb_ref, cu_ref, cup_ref, cun_ref, mod_ref, scw_ref, scb_ref, wo_ref,
         nfw_ref, wup_ref, cw_ref, cb_ref, wdn_ref, fnw_ref, out_ref, hid_scr) = refs
    else:
        (x_ref, gb_ref, cu_ref, mod_ref, scw_ref, scb_ref, wo_ref,
         nfw_ref, wup_ref, cw_ref, cb_ref, wdn_ref, fnw_ref, out_ref, hid_scr) = refs
    cu = _rows(cu_ref).astype(F32)
    t = cu.shape[0]
    if grid_rows:
        i = pl.program_id(1)
        halo_p = jnp.where(i == 0, 0.0, cup_ref[0].astype(F32))
        halo_n = jnp.where(i == pl.num_programs(1) - 1, 0.0, cun_ref[0].astype(F32))
        prev = jnp.concatenate([halo_p, cu[:t - GRID_W]], axis=0)
        nxt = jnp.concatenate([cu[GRID_W:], halo_n], axis=0)
    else:
        pos = lax.broadcasted_iota(jnp.int32, (t, 1), 0) % period
        prev = jnp.where(pos == 0, 0.0, pltpu.roll(cu, 1, 0))
        nxt = jnp.where(pos == period - 1, 0.0, pltpu.roll(cu, t - 1, 0))
    z = prev * scw_ref[0:1, :] + cu * scw_ref[1:2, :] + nxt * scw_ref[2:3, :] + scb_ref[...]
    m = (_rows(gb_ref).astype(F32) * z).astype(BF16)
    halves = _halves(x_ref)
    half = t // 2
    x1s = [x + mod_ref[0, 2:3, :] * _dot(m[s * half:(s + 1) * half], wo_ref[...])
           for s, (_, _, x) in enumerate(halves)]
    x2s = _conv_ffn(x1s, mod_ref, nfw_ref, wup_ref, cw_ref, cb_ref, wdn_ref, hid_scr, period)
    for (bi, r0, _), x2 in zip(halves, x2s):
        out_ref[bi, r0:r0 + half, :] = _rms(x2, fnw_ref[...])


def _post1(x, gb, cu, mod, scw, scb, wo, nfw, wup, cw, cb, wdn, fnw, per_seq_mod, grid_rows, period):
    bsz, n, _ = x.shape
    tb, tl = _tile(n)
    mod_map = (lambda b, i: (b, 0, 0)) if per_seq_mod else (lambda b, i: (0, 0, 0))
    tok_spec = pl.BlockSpec((tb, tl, D_MODEL), lambda b, i: (b, i, 0))
    in_specs = [tok_spec, tok_spec, tok_spec]
    args = [x, gb, cu]
    if grid_rows:
        r = tl // GRID_W
        n_rows = n // GRID_W
        in_specs += [
            pl.BlockSpec((1, GRID_W, D_MODEL), lambda b, i: (b, jnp.maximum(i * r - 1, 0), 0)),
            pl.BlockSpec((1, GRID_W, D_MODEL), lambda b, i: (b, jnp.minimum((i + 1) * r, n_rows - 1), 0)),
        ]
        args += [cu, cu]
    in_specs += [
        pl.BlockSpec((1, N_MOD, D_MODEL), mod_map),
        _const_spec((3, D_MODEL)),
        _const_spec((1, D_MODEL)),
        _const_spec((D_MODEL, D_MODEL)),
    ] + _ffn_specs() + [_const_spec((1, D_MODEL))]
    args += [mod, scw, scb, wo, nfw, wup, cw, cb, wdn, fnw]
    return pl.pallas_call(
        functools.partial(_post1_kernel, grid_rows=grid_rows, period=period),
        grid=(bsz // tb, n // tl),
        in_specs=in_specs,
        out_specs=tok_spec,
        out_shape=jax.ShapeDtypeStruct(x.shape, F32),
        scratch_shapes=[pltpu.VMEM((tb * tl, D_FF), BF16)],
        compiler_params=_params(2),
        name="post1",
    )(*args)


def kernel(x_prompt, x_sample, state_hgrn, c, c_ctx, ada_w, ada_b, norm_mix_w, norm_ffn_w,
           hgrn_w_in, hgrn_lower_bounds, hgrn_gnorm_w, hgrn_w_out,
           sconv_w_in, sconv_conv_w, sconv_conv_b, sconv_w_out,
           ffn_w_up, ffn_conv_w, ffn_conv_b, ffn_w_down, final_norm_w):
    seq_p = x_prompt.shape[1]

    cvecs = jnp.concatenate([c_ctx[None, :], c], axis=0)
    mod = _ada(cvecs, ada_w, ada_b)
    mod = mod.reshape(mod.shape[0], 8, N_MOD, D_MODEL)
    mod_p = mod[:, 0:1]
    mod_s = mod[:, 1:1 + c.shape[0]]

    row = lambda a: a.reshape(1, -1)
    w_in0 = hgrn_w_in[0].astype(BF16)
    gnw = row(hgrn_gnorm_w[0])
    lbp = hgrn_lower_bounds.reshape(-1, hgrn_lower_bounds.shape[-1])

    def mixer0(x, mods, s0, per_seq_mod, emit_state, gla_heads, proj_casts, casts):
        (q, ff, fb, v, g), proj_cast = _inproj0(x, mods[0], row(norm_mix_w[0]), w_in0, lbp,
                                                per_seq_mod, proj_casts)
        n_chunks = x.shape[1] // SCAN_CHUNK
        unroll = min(n_chunks, SCAN_LANES // 2)
        head_group = min(gla_heads, SCAN_LANES // (2 * unroll))
        o, s_fin, scan_cast = _gla(q, ff, fb, v, g, s0, gnw, emit_state, gla_heads, head_group,
                                   unroll, casts)
        return o, s_fin, proj_cast + scan_cast

    def post0(x, o, mods, per_seq_mod, ffn_period):
        return _post0(x, o, mods[0], w_out0, row(norm_ffn_w[0]), w_up0, ffn_conv_w[0],
                      row(ffn_conv_b[0]), w_dn0, mods[1], row(norm_mix_w[1]), w_in1,
                      per_seq_mod, ffn_period)

    def post1(x, gb, cu, mods, per_seq_mod, grid_rows, ffn_period):
        return _post1(x, gb, cu, mods[1], sconv_conv_w[0], row(sconv_conv_b[0]), w_out1,
                      row(norm_ffn_w[1]), w_up1, ffn_conv_w[1], row(ffn_conv_b[1]), w_dn1,
                      row(final_norm_w), per_seq_mod, grid_rows, ffn_period)

    o_p, new_state, (w_up0, w_out0, w_dn0, w_in1) = mixer0(
        x_prompt, mod_p, None, False, True, N_HEADS,
        [(ffn_w_up, 0)], [(hgrn_w_out, 0), (ffn_w_down, 0), (sconv_w_in, 0)])
    xp, gb_p, cu_p = post0(x_prompt, o_p, mod_p, False, seq_p)
    o_s, _, (w_out1, w_up1, w_dn1) = mixer0(
        x_sample, mod_s, state_hgrn, True, False, 1,
        [], [(sconv_w_out, 0), (ffn_w_up, 1), (ffn_w_down, 1)])
    y_prompt = post1(xp, gb_p, cu_p, mod_p, False, False, seq_p)
    xs, gb_s, cu_s = post0(x_sample, o_s, mod_s, True, GRID_W)
    y_sample = post1(xs, gb_s, cu_s, mod_s, True, True, GRID_W)
    return (y_prompt, y_sample, new_state.astype(state_hgrn.dtype))
```

```python
import functools

import jax
import jax.numpy as jnp
from jax import lax
from jax.experimental import pallas as pl
from jax.experimental.pallas import tpu as pltpu

D_MODEL = 1024
N_HEADS = 8
D_HEAD = 128
D_FF = 2816
N_MOD = 6
GRID_W = 64
EPS = 1e-6

LANES = 128
SCAN_CHUNK = 64
SCAN_LANES = 64
TOKEN_TILE = 512
FF_CHUNK = 256
ADA_ROWS = 256
VMEM_LIMIT = 56 * 1024 * 1024

F32 = jnp.float32
BF16 = jnp.bfloat16


def _sigmoid(x):
    return 1.0 / (1.0 + jnp.exp(-x))


def _silu(x):
    return x * _sigmoid(x)


def _rms(x, w):
    return x * lax.rsqrt(jnp.mean(x * x, axis=-1, keepdims=True) + EPS) * w


def _modulate(x, w, shift, scale):
    return _rms(x, w) * (1.0 + scale) + shift


def _dot(a, b):
    return jnp.dot(a, b, preferred_element_type=F32)


def _dot_nt(a, b):
    return lax.dot_general(a, b, (((1,), (1,)), ((), ())), preferred_element_type=F32)


def _dot_tn(a, b):
    return lax.dot_general(a, b, (((0,), (0,)), ((), ())), preferred_element_type=F32)


def _params(n_grid):
    return pltpu.CompilerParams(
        dimension_semantics=("arbitrary",) * n_grid,
        vmem_limit_bytes=VMEM_LIMIT,
    )


def _const_spec(shape):
    nd = len(shape)
    return pl.BlockSpec(shape, lambda *_: (0,) * nd, pipeline_mode=pl.Buffered(1))


def _cast_riders(casts, steps, step_of):
    in_specs, args, out_specs, out_shape = [], [], [], []
    for w, layer in casts:
        _, rows, cols = w.shape
        slab = rows // steps
        in_specs.append(pl.BlockSpec((1, 1, slab, cols),
                                     lambda *g, layer=layer: (layer, step_of(*g), 0, 0)))
        args.append(w.reshape(w.shape[0], steps, slab, cols))
        out_specs.append(pl.BlockSpec((1, slab, cols), lambda *g: (step_of(*g), 0, 0)))
        out_shape.append(jax.ShapeDtypeStruct((steps, slab, cols), BF16))
    return in_specs, args, out_specs, out_shape


def _cast_slabs(cast_in, cast_out):
    for src, dst in zip(cast_in, cast_out):
        dst[0] = src[0, 0].astype(dst.dtype)


def _cast_results(outs, casts):
    return [o.reshape(w.shape[1], w.shape[2]) for o, (w, _) in zip(outs, casts)]


def _ada_kernel(c_ref, wa_ref, wb_ref, b_ref, o_ref, s_scr):
    k = pl.program_id(1)
    rows, n_out = wa_ref.shape[1], wa_ref.shape[2]

    @pl.when((pl.program_id(0) == 0) & (k == 0))
    def _():
        s_scr[...] = _silu(c_ref[...])

    @pl.when(k == 0)
    def _():
        o_ref[0] = jnp.broadcast_to(b_ref[0], o_ref.shape[1:])

    for half, w_ref in enumerate((wa_ref, wb_ref)):
        r0 = pl.multiple_of((2 * k + half) * rows, rows)
        for r in range(c_ref.shape[0]):
            s = s_scr[r, pl.ds(r0, rows), :]
            parts = [
                jnp.sum(w_ref[0, :, j * LANES:(j + 1) * LANES] * s, axis=0, keepdims=True)
                for j in range(n_out // LANES)
            ]
            o_ref[0, r:r + 1, :] += jnp.concatenate(parts, axis=1)


def _ada(cvecs, ada_w, ada_b):
    depth, d_in, n_out = ada_w.shape
    cb = jnp.broadcast_to(cvecs[:, :, None], cvecs.shape + (LANES,))
    return pl.pallas_call(
        _ada_kernel,
        grid=(depth, d_in // ADA_ROWS),
        in_specs=[
            pl.BlockSpec(cb.shape, lambda l, k: (0, 0, 0)),
            pl.BlockSpec((1, ADA_ROWS // 2, n_out), lambda l, k: (l, 2 * k, 0)),
            pl.BlockSpec((1, ADA_ROWS // 2, n_out), lambda l, k: (l, 2 * k + 1, 0)),
            pl.BlockSpec((1, 1, n_out), lambda l, k: (l, 0, 0)),
        ],
        out_specs=pl.BlockSpec((1, 8, n_out), lambda l, k: (l, 0, 0)),
        out_shape=jax.ShapeDtypeStruct((depth, 8, n_out), F32),
        scratch_shapes=[pltpu.VMEM(cb.shape, F32)],
        compiler_params=_params(2),
        name="ada",
    )(cb, ada_w, ada_w, ada_b.reshape(depth, 1, n_out))


def _inproj0_kernel(x_ref, mod_ref, nw_ref, w_ref, lbp_ref, *refs, n_casts):
    cast_in, (q_ref, ff_ref, fb_ref, v_ref, g_ref) = refs[:n_casts], refs[n_casts:n_casts + 5]
    _cast_slabs(cast_in, refs[n_casts + 5:])
    x = _rows(x_ref)
    tb, _, tl, _ = q_ref.shape
    half = x.shape[0] // 2
    hs = [_modulate(x[s * half:(s + 1) * half], nw_ref[...], mod_ref[0, 0:1, :],
                    mod_ref[0, 1:2, :]).astype(BF16) for s in range(2)]

    n_lb = lbp_ref.shape[0] // 2

    def lower_bound(d):
        rows = [lbp_ref[d * n_lb + i:d * n_lb + i + 1, :] for i in range(n_lb)]
        top = functools.reduce(jnp.maximum, rows)
        e = [jnp.exp(r - top) for r in rows]
        return e[0] / functools.reduce(jnp.add, e)

    lb = [lower_bound(0), lower_bound(1)]

    def emit(j, ref, act):
        vals = [_dot(h, w_ref[:, j * D_MODEL:(j + 1) * D_MODEL]) for h in hs]
        for s, val in enumerate(vals):
            val = act(val)
            b, r0 = divmod(s * half, tl)
            for hh in range(N_HEADS):
                ref[b, hh, r0:r0 + half, :] = val[:, hh * D_HEAD:(hh + 1) * D_HEAD].astype(ref.dtype)

    emit(0, q_ref, _silu)
    emit(1, ff_ref, lambda p: lb[0] + (1.0 - lb[0]) * _sigmoid(p))
    emit(2, fb_ref, lambda p: lb[1] + (1.0 - lb[1]) * _sigmoid(p))
    emit(3, v_ref, lambda p: p)
    emit(4, g_ref, _silu)


def _inproj0(x, mod, nw, w_in, lbp, per_seq_mod, casts):
    bsz, n, _ = x.shape
    tb, tl = _tile(n)
    n_i = n // tl
    mod_map = (lambda b, i: (b, 0, 0)) if per_seq_mod else (lambda b, i: (0, 0, 0))
    head_spec = pl.BlockSpec((tb, N_HEADS, tl, D_HEAD), lambda b, i: (b, 0, i, 0))
    shp = (bsz, N_HEADS, n, D_HEAD)
    c_in, c_args, c_out, c_shape = _cast_riders(casts, (bsz // tb) * n_i, lambda b, i: b * n_i + i)
    outs = pl.pallas_call(
        functools.partial(_inproj0_kernel, n_casts=len(casts)),
        grid=(bsz // tb, n_i),
        in_specs=[
            pl.BlockSpec((tb, tl, D_MODEL), lambda b, i: (b, i, 0)),
            pl.BlockSpec((1, N_MOD, D_MODEL), mod_map),
            _const_spec((1, D_MODEL)),
            _const_spec(w_in.shape),
            _const_spec(lbp.shape),
        ] + c_in,
        out_specs=[head_spec] * 5 + c_out,
        out_shape=[
            jax.ShapeDtypeStruct(shp, BF16),
            jax.ShapeDtypeStruct(shp, F32),
            jax.ShapeDtypeStruct(shp, F32),
            jax.ShapeDtypeStruct(shp, BF16),
            jax.ShapeDtypeStruct(shp, BF16),
        ] + c_shape,
        compiler_params=_params(2),
        name="inproj0",
    )(x, mod, nw, w_in, lbp, *c_args)
    return outs[:5], _cast_results(outs[5:], casts)


def _cumsum_rows(tri, x):
    hi = x.astype(BF16)
    lo = (x - hi.astype(F32)).astype(BF16)
    s = _dot(tri, jnp.concatenate([hi, lo], axis=1))
    return s[:, :D_HEAD] + s[:, D_HEAD:]


def _gla_kernel(*refs, seq_len, seqs, heads, zero_init, emit_state, n_casts):
    q_ref, ff_ref, fb_ref, v_ref, g_ref = refs[:5]
    pos = 5
    s0_ref = None
    if not zero_init:
        s0_ref = refs[pos]
        pos += 1
    gnw_ref = refs[pos]
    cast_in = refs[pos + 1:pos + 1 + n_casts]
    pos += 1 + n_casts
    o_ref = refs[pos]
    pos += 1
    sf_ref = None
    if emit_state:
        sf_ref = refs[pos]
        pos += 1
    cast_out = refs[pos:pos + n_casts]

    _cast_slabs(cast_in, cast_out)

    c = SCAN_CHUNK
    n_chunks = seq_len // c
    row = lax.broadcasted_iota(jnp.int32, (c, c), 0)
    col = lax.broadcasted_iota(jnp.int32, (c, c), 1)
    tri_lo = (col <= row).astype(BF16)
    tri_up = (col >= row).astype(BF16)
    row2 = lax.broadcasted_iota(jnp.int32, (c, 2 * c), 0)
    col2 = lax.broadcasted_iota(jnp.int32, (c, 2 * c), 1)
    mask2 = ((col2 < c) & (col2 <= row2)) | ((col2 >= c) & (col2 - c >= row2))
    zeros = jnp.zeros((c, D_HEAD), BF16)
    gnw = gnw_ref[...]
    directions = ((tri_lo, c // 2 - 1, c - 1), (tri_up, c // 2, 0))

    def scan_sequence(b, carry):
        lanes = [dict(h=h, r0=n * c) for n in range(n_chunks) for h in range(heads)]
        states = []
        for h in range(heads):
            for d in range(2):
                if zero_init:
                    states.append(jnp.zeros((D_HEAD, D_HEAD), F32))
                else:
                    states.append(s0_ref[b, 0, d, h].T)

        def stage_a(ln):
            rows = pl.ds(ln["r0"], c)
            ln["f"] = [ff_ref[b, ln["h"], rows, :], fb_ref[b, ln["h"], rows, :]]
            ln["cum"] = [_cumsum_rows(tri, jnp.log2(f))
                         for (tri, _, _), f in zip(directions, ln["f"])]

        def stage_b(ln):
            rows = pl.ds(ln["r0"], c)
            q = q_ref[b, ln["h"], rows, :].astype(F32)
            ln["v"] = v_ref[b, ln["h"], rows, :]
            q_mid, k_mid, q_dec, k_state, ln["decay"] = [], [], [], [], []
            for d, (_, mid, last) in enumerate(directions):
                cum = ln["cum"][d]
                cum_mid = cum[mid:mid + 1]
                cum_last = cum[last:last + 1]
                qm = q * jnp.exp2(cum - cum_mid)
                km = (1.0 - ln["f"][d]) * jnp.exp2(cum_mid - cum)
                q_mid.append(qm.astype(BF16))
                k_mid.append(km.astype(BF16))
                q_dec.append((qm * jnp.exp2(cum_mid)).astype(BF16))
                k_state.append((km * jnp.exp2(cum_last - cum_mid)).astype(BF16))
                ln["decay"].append(jnp.exp2(cum_last))
            k_pair = jnp.concatenate([jnp.concatenate([k_mid[0], zeros], axis=1),
                                      jnp.concatenate([zeros, k_mid[1]], axis=1)], axis=0)
            ln["scores"] = _dot_nt(jnp.concatenate(q_mid, axis=1), k_pair)
            ln["kv"] = _dot_tn(ln["v"], jnp.concatenate(k_state, axis=1))
            ln["q_dec"] = jnp.concatenate(q_dec, axis=1)

        def stage_c(h):
            mine = [ln for ln in lanes if ln["h"] == h]
            for d, order in ((0, mine), (1, mine[::-1])):
                st = states[2 * h + d]
                for ln in order:
                    ln.setdefault("st", [None, None])[d] = st.astype(BF16).T
                    st = st * ln["decay"][d] + ln["kv"][:, d * D_HEAD:(d + 1) * D_HEAD]
                states[2 * h + d] = st

        def stage_d(ln):
            rows = pl.ds(ln["r0"], c)
            p = jnp.where(mask2, ln["scores"], 0.0).astype(BF16)
            lhs = jnp.concatenate([ln["q_dec"], p], axis=1)
            rhs = jnp.concatenate([ln["st"][0], ln["st"][1], ln["v"], ln["v"]], axis=0)
            tot = _dot(lhs, rhs)
            gate = g_ref[b, ln["h"], rows, :].astype(F32)
            o_ref[b, ln["h"], rows, :] = (_rms(tot, gnw) * gate).astype(o_ref.dtype)

        for ln in lanes:
            stage_a(ln)
        for ln in lanes:
            stage_b(ln)
        for h in range(heads):
            stage_c(h)
        for ln in lanes:
            stage_d(ln)
        if emit_state:
            for h in range(heads):
                sf_ref[b, 0, 0, h] = states[2 * h].T
                sf_ref[b, 0, 1, h] = states[2 * h + 1].T
        return carry

    if seqs == 1:
        scan_sequence(0, 0)
    else:
        lax.fori_loop(0, seqs, scan_sequence, 0)


def _gla(q, ff, fb, v, g, s0, gnw, emit_state, seqs, heads, casts):
    bsz, _, n, _ = q.shape
    n_hb = N_HEADS // heads
    steps = (bsz // seqs) * n_hb
    zero_init = s0 is None
    seq_spec = pl.BlockSpec((seqs, heads, n, D_HEAD), lambda b, h: (b, h, 0, 0))
    state_spec = pl.BlockSpec((seqs, 1, 2, heads, D_HEAD, D_HEAD), lambda b, h: (b, 0, 0, h, 0, 0))
    in_specs = [seq_spec] * 5
    args = [q, ff, fb, v, g]
    if not zero_init:
        in_specs.append(state_spec)
        args.append(s0)
    in_specs.append(_const_spec((1, D_HEAD)))
    args.append(gnw)
    out_specs = [seq_spec]
    out_shape = [jax.ShapeDtypeStruct(q.shape, BF16)]
    if emit_state:
        out_specs.append(state_spec)
        out_shape.append(jax.ShapeDtypeStruct((bsz, 1, 2, N_HEADS, D_HEAD, D_HEAD), F32))
    c_in, c_args, c_out, c_shape = _cast_riders(casts, steps, lambda b, h: b * n_hb + h)
    in_specs += c_in
    args += c_args
    out_specs += c_out
    out_shape += c_shape
    outs = pl.pallas_call(
        functools.partial(_gla_kernel, seq_len=n, seqs=seqs, heads=heads, zero_init=zero_init,
                          emit_state=emit_state, n_casts=len(casts)),
        grid=(bsz // seqs, n_hb),
        in_specs=in_specs,
        out_specs=out_specs,
        out_shape=out_shape,
        compiler_params=_params(2),
        name="gla",
    )(*args)
    n_main = 2 if emit_state else 1
    return outs[0], (outs[1] if emit_state else None), _cast_results(outs[n_main:], casts)


def _conv_ffn(x1s, mod_ref, nfw_ref, wup_ref, cw_ref, cb_ref, wdn_ref, hid_scr, period):
    t = x1s[0].shape[0]
    hs = [_modulate(x1, nfw_ref[...], mod_ref[0, 3:4, :], mod_ref[0, 4:5, :]).astype(BF16)
          for x1 in x1s]
    pos = lax.broadcasted_iota(jnp.int32, (t, 1), 0) % period
    first = pos == 0
    last = pos == period - 1
    for j in range(D_FF // FF_CHUNK):
        lo = j * FF_CHUNK
        for s, h in enumerate(hs):
            a = _dot(h, wup_ref[:, lo:lo + FF_CHUNK])
            g = _dot(h, wup_ref[:, D_FF + lo:D_FF + lo + FF_CHUNK])
            prev = jnp.where(first, 0.0, pltpu.roll(a, 1, 0))
            nxt = jnp.where(last, 0.0, pltpu.roll(a, t - 1, 0))
            ac = (prev * cw_ref[0:1, lo:lo + FF_CHUNK] + a * cw_ref[1:2, lo:lo + FF_CHUNK]
                  + nxt * cw_ref[2:3, lo:lo + FF_CHUNK] + cb_ref[:, lo:lo + FF_CHUNK])
            hid_scr[s * t:(s + 1) * t, lo:lo + FF_CHUNK] = (_silu(ac) * g).astype(BF16)
    return [x1 + mod_ref[0, 5:6, :] * _dot(hid_scr[s * t:(s + 1) * t, :], wdn_ref[...])
            for s, x1 in enumerate(x1s)]


def _tile(n):
    return (1, TOKEN_TILE) if n >= TOKEN_TILE else (TOKEN_TILE // n, n)


def _rows(ref):
    v = ref[...]
    return v.reshape(v.shape[0] * v.shape[1], v.shape[2])


def _halves(ref):
    tb, tl, _ = ref.shape
    half = tb * tl // 2
    out = []
    for s in range(2):
        bi, r0 = divmod(s * half, tl)
        out.append((bi, r0, ref[bi, r0:r0 + half, :]))
    return out


def _ffn_specs():
    return [
        _const_spec((1, D_MODEL)),
        _const_spec((D_MODEL, 2 * D_FF)),
        _const_spec((3, D_FF)),
        _const_spec((1, D_FF)),
        _const_spec((D_FF, D_MODEL)),
    ]


def _post0_kernel(x_ref, o_ref, mod_ref, wo_ref, nfw_ref, wup_ref, cw_ref, cb_ref, wdn_ref,
                  mod1_ref, nw1_ref, w1_ref, out_ref, gb_ref, cu_ref, hid_scr, *, period):
    x1s = []
    for bi, r0, x in _halves(x_ref):
        half = x.shape[0]
        o = jnp.concatenate([o_ref[bi, hh, r0:r0 + half, :] for hh in range(N_HEADS)], axis=1)
        x1s.append(x + mod_ref[0, 2:3, :] * _dot(o, wo_ref[...]))
    x2s = _conv_ffn(x1s, mod_ref, nfw_ref, wup_ref, cw_ref, cb_ref, wdn_ref, hid_scr, period)
    hs = [_modulate(x2, nw1_ref[...], mod1_ref[0, 0:1, :], mod1_ref[0, 1:2, :]).astype(BF16)
          for x2 in x2s]
    where = [(bi, r0) for bi, r0, _ in _halves(x_ref)]
    for (bi, r0), x2, h in zip(where, x2s, hs):
        half = x2.shape[0]
        out_ref[bi, r0:r0 + half, :] = x2
        gb_ref[bi, r0:r0 + half, :] = _dot(h, w1_ref[:, 0:D_MODEL]).astype(gb_ref.dtype)
    for (bi, r0), h in zip(where, hs):
        half = h.shape[0]
        gate_c = _dot(h, w1_ref[:, D_MODEL:2 * D_MODEL])
        u = _dot(h, w1_ref[:, 2 * D_MODEL:3 * D_MODEL])
        cu_ref[bi, r0:r0 + half, :] = (gate_c * u).astype(cu_ref.dtype)


def _post0(x, o, mod, wo, nfw, wup, cw, cb, wdn, mod1, nw1, w1, per_seq_mod, period):
    bsz, n, _ = x.shape
    tb, tl = _tile(n)
    mod_map = (lambda b, i: (b, 0, 0)) if per_seq_mod else (lambda b, i: (0, 0, 0))
    tok_spec = pl.BlockSpec((tb, tl, D_MODEL), lambda b, i: (b, i, 0))
    return pl.pallas_call(
        functools.partial(_post0_kernel, period=period),
        grid=(bsz // tb, n // tl),
        in_specs=[
            tok_spec,
            pl.BlockSpec((tb, N_HEADS, tl, D_HEAD), lambda b, i: (b, 0, i, 0)),
            pl.BlockSpec((1, N_MOD, D_MODEL), mod_map),
            _const_spec((D_MODEL, D_MODEL)),
        ] + _ffn_specs() + [
            pl.BlockSpec((1, N_MOD, D_MODEL), mod_map),
            _const_spec((1, D_MODEL)),
            _const_spec(w1.shape),
        ],
        out_specs=[tok_spec] * 3,
        out_shape=[jax.ShapeDtypeStruct(x.shape, F32), jax.ShapeDtypeStruct(x.shape, BF16),
                   jax.ShapeDtypeStruct(x.shape, BF16)],
        scratch_shapes=[pltpu.VMEM((tb * tl, D_FF), BF16)],
        compiler_params=_params(2),
        name="post0",
    )(x, o, mod, wo, nfw, wup, cw, cb, wdn, mod1, nw1, w1)


def _post1_kernel(*refs, grid_rows, period):
    if grid_rows:
        (x_ref, gb_ref, cu_ref, cup_ref, cun_ref, mod_ref, scw_ref, scb_ref, wo_ref,
         nfw_ref, wup_ref, cw_ref, cb_ref, wdn_ref, fnw_ref, out_ref, hid_scr) = refs
    else:
        (x_ref, gb_ref, cu_ref, mod_ref, scw_ref, scb_ref, wo_ref,
         nfw_ref, wup_ref, cw_ref, cb_ref, wdn_ref, fnw_ref, out_ref, hid_scr) = refs
    cu = _rows(cu_ref).astype(F32)
    t = cu.shape[0]
    if grid_rows:
        i = pl.program_id(1)
        halo_p = jnp.where(i == 0, 0.0, cup_ref[0].astype(F32))
        halo_n = jnp.where(i == pl.num_programs(1) - 1, 0.0, cun_ref[0].astype(F32))
        prev = jnp.concatenate([halo_p, cu[:t - GRID_W]], axis=0)
        nxt = jnp.concatenate([cu[GRID_W:], halo_n], axis=0)
    else:
        pos = lax.broadcasted_iota(jnp.int32, (t, 1), 0) % period
        prev = jnp.where(pos == 0, 0.0, pltpu.roll(cu, 1, 0))
        nxt = jnp.where(pos == period - 1, 0.0, pltpu.roll(cu, t - 1, 0))
    z = prev * scw_ref[0:1, :] + cu * scw_ref[1:2, :] + nxt * scw_ref[2:3, :] + scb_ref[...]
    m = (_rows(gb_ref).astype(F32) * z).astype(BF16)
    halves = _halves(x_ref)
    half = t // 2
    x1s = [x + mod_ref[0, 2:3, :] * _dot(m[s * half:(s + 1) * half], wo_ref[...])
           for s, (_, _, x) in enumerate(halves)]
    x2s = _conv_ffn(x1s, mod_ref, nfw_ref, wup_ref, cw_ref, cb_ref, wdn_ref, hid_scr, period)
    for (bi, r0, _), x2 in zip(halves, x2s):
        out_ref[bi, r0:r0 + half, :] = _rms(x2, fnw_ref[...])


def _post1(x, gb, cu, mod, scw, scb, wo, nfw, wup, cw, cb, wdn, fnw, per_seq_mod, grid_rows, period):
    bsz, n, _ = x.shape
    tb, tl = _tile(n)
    mod_map = (lambda b, i: (b, 0, 0)) if per_seq_mod else (lambda b, i: (0, 0, 0))
    tok_spec = pl.BlockSpec((tb, tl, D_MODEL), lambda b, i: (b, i, 0))
    in_specs = [tok_spec, tok_spec, tok_spec]
    args = [x, gb, cu]
    if grid_rows:
        r = tl // GRID_W
        n_rows = n // GRID_W
        in_specs += [
            pl.BlockSpec((1, GRID_W, D_MODEL), lambda b, i: (b, jnp.maximum(i * r - 1, 0), 0)),
            pl.BlockSpec((1, GRID_W, D_MODEL), lambda b, i: (b, jnp.minimum((i + 1) * r, n_rows - 1), 0)),
        ]
        args += [cu, cu]
    in_specs += [
        pl.BlockSpec((1, N_MOD, D_MODEL), mod_map),
        _const_spec((3, D_MODEL)),
        _const_spec((1, D_MODEL)),
        _const_spec((D_MODEL, D_MODEL)),
    ] + _ffn_specs() + [_const_spec((1, D_MODEL))]
    args += [mod, scw, scb, wo, nfw, wup, cw, cb, wdn, fnw]
    return pl.pallas_call(
        functools.partial(_post1_kernel, grid_rows=grid_rows, period=period),
        grid=(bsz // tb, n // tl),
        in_specs=in_specs,
        out_specs=tok_spec,
        out_shape=jax.ShapeDtypeStruct(x.shape, F32),
        scratch_shapes=[pltpu.VMEM((tb * tl, D_FF), BF16)],
        compiler_params=_params(2),
        name="post1",
    )(*args)


def kernel(x_prompt, x_sample, state_hgrn, c, c_ctx, ada_w, ada_b, norm_mix_w, norm_ffn_w,
           hgrn_w_in, hgrn_lower_bounds, hgrn_gnorm_w, hgrn_w_out,
           sconv_w_in, sconv_conv_w, sconv_conv_b, sconv_w_out,
           ffn_w_up, ffn_conv_w, ffn_conv_b, ffn_w_down, final_norm_w):
    seq_p = x_prompt.shape[1]

    cvecs = jnp.concatenate([c_ctx[None, :], c], axis=0)
    mod = _ada(cvecs, ada_w, ada_b)
    mod = mod.reshape(mod.shape[0], 8, N_MOD, D_MODEL)
    mod_p = mod[:, 0:1]
    mod_s = mod[:, 1:1 + c.shape[0]]

    row = lambda a: a.reshape(1, -1)
    w_in0 = hgrn_w_in[0].astype(BF16)
    gnw = row(hgrn_gnorm_w[0])
    lbp = hgrn_lower_bounds.reshape(-1, hgrn_lower_bounds.shape[-1])

    def mixer0(x, mods, s0, per_seq_mod, emit_state, proj_casts, casts):
        (q, ff, fb, v, g), proj_cast = _inproj0(x, mods[0], row(norm_mix_w[0]), w_in0, lbp,
                                                per_seq_mod, proj_casts)
        heads = min(N_HEADS, max(1, SCAN_LANES // (x.shape[1] // SCAN_CHUNK)))
        seqs = 2 if heads == N_HEADS else 1
        o, s_fin, scan_cast = _gla(q, ff, fb, v, g, s0, gnw, emit_state, seqs, heads, casts)
        return o, s_fin, proj_cast + scan_cast

    def post0(x, o, mods, per_seq_mod, ffn_period):
        return _post0(x, o, mods[0], w_out0, row(norm_ffn_w[0]), w_up0, ffn_conv_w[0],
                      row(ffn_conv_b[0]), w_dn0, mods[1], row(norm_mix_w[1]), w_in1,
                      per_seq_mod, ffn_period)

    def post1(x, gb, cu, mods, per_seq_mod, grid_rows, ffn_period):
        return _post1(x, gb, cu, mods[1], sconv_conv_w[0], row(sconv_conv_b[0]), w_out1,
                      row(norm_ffn_w[1]), w_up1, ffn_conv_w[1], row(ffn_conv_b[1]), w_dn1,
                      row(final_norm_w), per_seq_mod, grid_rows, ffn_period)

    o_p, new_state, (w_up0, w_out0, w_dn0, w_in1) = mixer0(
        x_prompt, mod_p, None, False, True,
        [(ffn_w_up, 0)], [(hgrn_w_out, 0), (ffn_w_down, 0), (sconv_w_in, 0)])
    xp, gb_p, cu_p = post0(x_prompt, o_p, mod_p, False, seq_p)
    o_s, _, (w_out1, w_up1, w_dn1) = mixer0(
        x_sample, mod_s, state_hgrn, True, False,
        [], [(sconv_w_out, 0), (ffn_w_up, 1), (ffn_w_down, 1)])
    y_prompt = post1(xp, gb_p, cu_p, mod_p, False, False, seq_p)
    xs, gb_s, cu_s = post0(x_sample, o_s, mod_s, True, GRID_W)
    y_sample = post1(xs, gb_s, cu_s, mod_s, True, True, GRID_W)
    return (y_prompt, y_sample, new_state.astype(state_hgrn.dtype))
```

```python
import functools

import jax
import jax.numpy as jnp
from jax import lax
from jax.experimental import pallas as pl
from jax.experimental.pallas import tpu as pltpu

D_MODEL = 1024
N_HEADS = 8
D_HEAD = 128
D_FF = 2816
N_MOD = 6
GRID_W = 64
EPS = 1e-6

LANES = 128
SCAN_CHUNK = 64
SCAN_LANES = 64
TOKEN_TILE = 512
FF_CHUNK = 256
ADA_ROWS = 256
VMEM_LIMIT = 56 * 1024 * 1024

F32 = jnp.float32
BF16 = jnp.bfloat16


def _sigmoid(x):
    return 1.0 / (1.0 + jnp.exp(-x))


def _silu(x):
    return x * _sigmoid(x)


def _rms(x, w):
    return x * lax.rsqrt(jnp.mean(x * x, axis=-1, keepdims=True) + EPS) * w


def _modulate(x, w, shift, scale):
    return _rms(x, w) * (1.0 + scale) + shift


def _dot(a, b):
    return jnp.dot(a, b, preferred_element_type=F32)


def _dot_nt(a, b):
    return lax.dot_general(a, b, (((1,), (1,)), ((), ())), preferred_element_type=F32)


def _dot_tn(a, b):
    return lax.dot_general(a, b, (((0,), (0,)), ((), ())), preferred_element_type=F32)


def _params(n_grid):
    return pltpu.CompilerParams(
        dimension_semantics=("arbitrary",) * n_grid,
        vmem_limit_bytes=VMEM_LIMIT,
    )


def _const_spec(shape):
    nd = len(shape)
    return pl.BlockSpec(shape, lambda *_: (0,) * nd, pipeline_mode=pl.Buffered(1))


def _cast_riders(casts, steps, step_of):
    in_specs, args, out_specs, out_shape = [], [], [], []
    for w, layer in casts:
        _, rows, cols = w.shape
        slab = rows // steps
        in_specs.append(pl.BlockSpec((1, 1, slab, cols),
                                     lambda *g, layer=layer: (layer, step_of(*g), 0, 0)))
        args.append(w.reshape(w.shape[0], steps, slab, cols))
        out_specs.append(pl.BlockSpec((1, slab, cols), lambda *g: (step_of(*g), 0, 0)))
        out_shape.append(jax.ShapeDtypeStruct((steps, slab, cols), BF16))
    return in_specs, args, out_specs, out_shape


def _cast_slabs(cast_in, cast_out):
    for src, dst in zip(cast_in, cast_out):
        dst[0] = src[0, 0].astype(dst.dtype)


def _cast_results(outs, casts):
    return [o.reshape(w.shape[1], w.shape[2]) for o, (w, _) in zip(outs, casts)]


def _ada_kernel(c_ref, wa_ref, wb_ref, b_ref, o_ref, s_scr):
    k = pl.program_id(1)
    rows, n_out = wa_ref.shape[1], wa_ref.shape[2]

    @pl.when((pl.program_id(0) == 0) & (k == 0))
    def _():
        s_scr[...] = _silu(c_ref[...])

    @pl.when(k == 0)
    def _():
        o_ref[0] = jnp.broadcast_to(b_ref[0], o_ref.shape[1:])

    for half, w_ref in enumerate((wa_ref, wb_ref)):
        r0 = pl.multiple_of((2 * k + half) * rows, rows)
        for r in range(c_ref.shape[0]):
            s = s_scr[r, pl.ds(r0, rows), :]
            parts = [
                jnp.sum(w_ref[0, :, j * LANES:(j + 1) * LANES] * s, axis=0, keepdims=True)
                for j in range(n_out // LANES)
            ]
            o_ref[0, r:r + 1, :] += jnp.concatenate(parts, axis=1)


def _ada(cvecs, ada_w, ada_b):
    depth, d_in, n_out = ada_w.shape
    cb = jnp.broadcast_to(cvecs[:, :, None], cvecs.shape + (LANES,))
    return pl.pallas_call(
        _ada_kernel,
        grid=(depth, d_in // ADA_ROWS),
        in_specs=[
            pl.BlockSpec(cb.shape, lambda l, k: (0, 0, 0)),
            pl.BlockSpec((1, ADA_ROWS // 2, n_out), lambda l, k: (l, 2 * k, 0)),
            pl.BlockSpec((1, ADA_ROWS // 2, n_out), lambda l, k: (l, 2 * k + 1, 0)),
            pl.BlockSpec((1, 1, n_out), lambda l, k: (l, 0, 0)),
        ],
        out_specs=pl.BlockSpec((1, 8, n_out), lambda l, k: (l, 0, 0)),
        out_shape=jax.ShapeDtypeStruct((depth, 8, n_out), F32),
        scratch_shapes=[pltpu.VMEM(cb.shape, F32)],
        compiler_params=_params(2),
        name="ada",
    )(cb, ada_w, ada_w, ada_b.reshape(depth, 1, n_out))


def _inproj0_kernel(x_ref, mod_ref, nw_ref, w_ref, lbp_ref, *refs, n_casts):
    cast_in, (q_ref, ff_ref, fb_ref, v_ref, g_ref) = refs[:n_casts], refs[n_casts:n_casts + 5]
    _cast_slabs(cast_in, refs[n_casts + 5:])
    x = _rows(x_ref)
    tb, _, tl, _ = q_ref.shape
    half = x.shape[0] // 2
    hs = [_modulate(x[s * half:(s + 1) * half], nw_ref[...], mod_ref[0, 0:1, :],
                    mod_ref[0, 1:2, :]).astype(BF16) for s in range(2)]

    n_lb = lbp_ref.shape[0] // 2

    def lower_bound(d):
        rows = [lbp_ref[d * n_lb + i:d * n_lb + i + 1, :] for i in range(n_lb)]
        top = functools.reduce(jnp.maximum, rows)
        e = [jnp.exp(r - top) for r in rows]
        return e[0] / functools.reduce(jnp.add, e)

    lb = [lower_bound(0), lower_bound(1)]

    def emit(j, ref, act):
        vals = [_dot(h, w_ref[:, j * D_MODEL:(j + 1) * D_MODEL]) for h in hs]
        for s, val in enumerate(vals):
            val = act(val)
            b, r0 = divmod(s * half, tl)
            for hh in range(N_HEADS):
                ref[b, hh, r0:r0 + half, :] = val[:, hh * D_HEAD:(hh + 1) * D_HEAD].astype(ref.dtype)

    emit(0, q_ref, _silu)
    emit(1, ff_ref, lambda p: lb[0] + (1.0 - lb[0]) * _sigmoid(p))
    emit(2, fb_ref, lambda p: lb[1] + (1.0 - lb[1]) * _sigmoid(p))
    emit(3, v_ref, lambda p: p)
    emit(4, g_ref, _silu)


def _inproj0(x, mod, nw, w_in, lbp, per_seq_mod, casts):
    bsz, n, _ = x.shape
    tb, tl = _tile(n)
    n_i = n // tl
    mod_map = (lambda b, i: (b, 0, 0)) if per_seq_mod else (lambda b, i: (0, 0, 0))
    head_spec = pl.BlockSpec((tb, N_HEADS, tl, D_HEAD), lambda b, i: (b, 0, i, 0))
    shp = (bsz, N_HEADS, n, D_HEAD)
    c_in, c_args, c_out, c_shape = _cast_riders(casts, (bsz // tb) * n_i, lambda b, i: b * n_i + i)
    outs = pl.pallas_call(
        functools.partial(_inproj0_kernel, n_casts=len(casts)),
        grid=(bsz // tb, n_i),
        in_specs=[
            pl.BlockSpec((tb, tl, D_MODEL), lambda b, i: (b, i, 0)),
            pl.BlockSpec((1, N_MOD, D_MODEL), mod_map),
            _const_spec((1, D_MODEL)),
            _const_spec(w_in.shape),
            _const_spec(lbp.shape),
        ] + c_in,
        out_specs=[head_spec] * 5 + c_out,
        out_shape=[
            jax.ShapeDtypeStruct(shp, BF16),
            jax.ShapeDtypeStruct(shp, F32),
            jax.ShapeDtypeStruct(shp, F32),
            jax.ShapeDtypeStruct(shp, BF16),
            jax.ShapeDtypeStruct(shp, BF16),
        ] + c_shape,
        compiler_params=_params(2),
        name="inproj0",
    )(x, mod, nw, w_in, lbp, *c_args)
    return outs[:5], _cast_results(outs[5:], casts)


def _cumsum_rows(tri, x):
    hi = x.astype(BF16)
    lo = (x - hi.astype(F32)).astype(BF16)
    s = _dot(tri, jnp.concatenate([hi, lo], axis=1))
    return s[:, :D_HEAD] + s[:, D_HEAD:]


def _gla_kernel(*refs, seq_len, heads, zero_init, emit_state, n_casts):
    q_ref, ff_ref, fb_ref, v_ref, g_ref = refs[:5]
    pos = 5
    s0_ref = None
    if not zero_init:
        s0_ref = refs[pos]
        pos += 1
    gnw_ref = refs[pos]
    cast_in = refs[pos + 1:pos + 1 + n_casts]
    pos += 1 + n_casts
    o_ref = refs[pos]
    pos += 1
    sf_ref = None
    if emit_state:
        sf_ref = refs[pos]
        pos += 1
    cast_out = refs[pos:pos + n_casts]

    _cast_slabs(cast_in, cast_out)

    c = SCAN_CHUNK
    n_chunks = seq_len // c
    row = lax.broadcasted_iota(jnp.int32, (c, c), 0)
    col = lax.broadcasted_iota(jnp.int32, (c, c), 1)
    tri_lo = (col <= row).astype(BF16)
    tri_up = (col >= row).astype(BF16)
    row2 = lax.broadcasted_iota(jnp.int32, (c, 2 * c), 0)
    col2 = lax.broadcasted_iota(jnp.int32, (c, 2 * c), 1)
    mask2 = ((col2 < c) & (col2 <= row2)) | ((col2 >= c) & (col2 - c >= row2))
    zeros = jnp.zeros((c, D_HEAD), BF16)
    gnw = gnw_ref[...]
    directions = ((tri_lo, c // 2 - 1, c - 1), (tri_up, c // 2, 0))

    lanes = [dict(h=h, r0=n * c) for h in range(heads) for n in range(n_chunks)]
    states = []
    for h in range(heads):
        for d in range(2):
            if zero_init:
                states.append(jnp.zeros((D_HEAD, D_HEAD), F32))
            else:
                states.append(s0_ref[0, 0, d, h].T)

    def stage_a(ln):
        rows = pl.ds(ln["r0"], c)
        ln["f"] = [ff_ref[0, ln["h"], rows, :], fb_ref[0, ln["h"], rows, :]]
        ln["cum"] = [_cumsum_rows(tri, jnp.log2(f))
                     for (tri, _, _), f in zip(directions, ln["f"])]

    def stage_b(ln):
        rows = pl.ds(ln["r0"], c)
        q = q_ref[0, ln["h"], rows, :].astype(F32)
        ln["v"] = v_ref[0, ln["h"], rows, :]
        q_mid, k_mid, q_dec, k_state, ln["decay"] = [], [], [], [], []
        for d, (_, mid, last) in enumerate(directions):
            cum = ln["cum"][d]
            cum_mid = cum[mid:mid + 1]
            cum_last = cum[last:last + 1]
            qm = q * jnp.exp2(cum - cum_mid)
            km = (1.0 - ln["f"][d]) * jnp.exp2(cum_mid - cum)
            q_mid.append(qm.astype(BF16))
            k_mid.append(km.astype(BF16))
            q_dec.append((qm * jnp.exp2(cum_mid)).astype(BF16))
            k_state.append((km * jnp.exp2(cum_last - cum_mid)).astype(BF16))
            ln["decay"].append(jnp.exp2(cum_last))
        k_pair = jnp.concatenate([jnp.concatenate([k_mid[0], zeros], axis=1),
                                  jnp.concatenate([zeros, k_mid[1]], axis=1)], axis=0)
        ln["scores"] = _dot_nt(jnp.concatenate(q_mid, axis=1), k_pair)
        ln["kv"] = _dot_tn(ln["v"], jnp.concatenate(k_state, axis=1))
        ln["q_dec"] = jnp.concatenate(q_dec, axis=1)

    def stage_c(h):
        mine = [ln for ln in lanes if ln["h"] == h]
        for d, order in ((0, mine), (1, mine[::-1])):
            st = states[2 * h + d]
            for ln in order:
                ln.setdefault("st", [None, None])[d] = st.astype(BF16).T
                st = st * ln["decay"][d] + ln["kv"][:, d * D_HEAD:(d + 1) * D_HEAD]
            states[2 * h + d] = st

    def stage_d(ln):
        rows = pl.ds(ln["r0"], c)
        p = jnp.where(mask2, ln["scores"], 0.0).astype(BF16)
        lhs = jnp.concatenate([ln["q_dec"], p], axis=1)
        rhs = jnp.concatenate([ln["st"][0], ln["st"][1], ln["v"], ln["v"]], axis=0)
        tot = _dot(lhs, rhs)
        gate = g_ref[0, ln["h"], rows, :].astype(F32)
        o_ref[0, ln["h"], rows, :] = (_rms(tot, gnw) * gate).astype(o_ref.dtype)

    for ln in lanes:
        stage_a(ln)
    for ln in lanes:
        stage_b(ln)
    for h in range(heads):
        stage_c(h)
    for ln in lanes:
        stage_d(ln)
    if emit_state:
        for h in range(heads):
            sf_ref[0, 0, 0, h] = states[2 * h].T
            sf_ref[0, 0, 1, h] = states[2 * h + 1].T


def _gla(q, ff, fb, v, g, s0, gnw, emit_state, heads, casts):
    bsz, _, n, _ = q.shape
    n_hb = N_HEADS // heads
    steps = bsz * n_hb
    zero_init = s0 is None
    seq_spec = pl.BlockSpec((1, heads, n, D_HEAD), lambda b, h: (b, h, 0, 0))
    state_spec = pl.BlockSpec((1, 1, 2, heads, D_HEAD, D_HEAD), lambda b, h: (b, 0, 0, h, 0, 0))
    in_specs = [seq_spec] * 5
    args = [q, ff, fb, v, g]
    if not zero_init:
        in_specs.append(state_spec)
        args.append(s0)
    in_specs.append(_const_spec((1, D_HEAD)))
    args.append(gnw)
    out_specs = [seq_spec]
    out_shape = [jax.ShapeDtypeStruct(q.shape, BF16)]
    if emit_state:
        out_specs.append(state_spec)
        out_shape.append(jax.ShapeDtypeStruct((bsz, 1, 2, N_HEADS, D_HEAD, D_HEAD), F32))
    c_in, c_args, c_out, c_shape = _cast_riders(casts, steps, lambda b, h: b * n_hb + h)
    in_specs += c_in
    args += c_args
    out_specs += c_out
    out_shape += c_shape
    outs = pl.pallas_call(
        functools.partial(_gla_kernel, seq_len=n, heads=heads, zero_init=zero_init,
                          emit_state=emit_state, n_casts=len(casts)),
        grid=(bsz, n_hb),
        in_specs=in_specs,
        out_specs=out_specs,
        out_shape=out_shape,
        compiler_params=_params(2),
        name="gla",
    )(*args)
    n_main = 2 if emit_state else 1
    return outs[0], (outs[1] if emit_state else None), _cast_results(outs[n_main:], casts)


def _conv_ffn(x1s, mod_ref, nfw_ref, wup_ref, cw_ref, cb_ref, wdn_ref, hid_scr, period):
    t = x1s[0].shape[0]
    hs = [_modulate(x1, nfw_ref[...], mod_ref[0, 3:4, :], mod_ref[0, 4:5, :]).astype(BF16)
          for x1 in x1s]
    pos = lax.broadcasted_iota(jnp.int32, (t, 1), 0) % period
    first = pos == 0
    last = pos == period - 1
    for j in range(D_FF // FF_CHUNK):
        lo = j * FF_CHUNK
        for s, h in enumerate(hs):
            a = _dot(h, wup_ref[:, lo:lo + FF_CHUNK])
            g = _dot(h, wup_ref[:, D_FF + lo:D_FF + lo + FF_CHUNK])
            prev = jnp.where(first, 0.0, pltpu.roll(a, 1, 0))
            nxt = jnp.where(last, 0.0, pltpu.roll(a, t - 1, 0))
            ac = (prev * cw_ref[0:1, lo:lo + FF_CHUNK] + a * cw_ref[1:2, lo:lo + FF_CHUNK]
                  + nxt * cw_ref[2:3, lo:lo + FF_CHUNK] + cb_ref[:, lo:lo + FF_CHUNK])
            hid_scr[s * t:(s + 1) * t, lo:lo + FF_CHUNK] = (_silu(ac) * g).astype(BF16)
    return [x1 + mod_ref[0, 5:6, :] * _dot(hid_scr[s * t:(s + 1) * t, :], wdn_ref[...])
            for s, x1 in enumerate(x1s)]


def _tile(n):
    return (1, TOKEN_TILE) if n >= TOKEN_TILE else (TOKEN_TILE // n, n)


def _rows(ref):
    v = ref[...]
    return v.reshape(v.shape[0] * v.shape[1], v.shape[2])


def _halves(ref):
    tb, tl, _ = ref.shape
    half = tb * tl // 2
    out = []
    for s in range(2):
        bi, r0 = divmod(s * half, tl)
        out.append((bi, r0, ref[bi, r0:r0 + half, :]))
    return out


def _ffn_specs():
    return [
        _const_spec((1, D_MODEL)),
        _const_spec((D_MODEL, 2 * D_FF)),
        _const_spec((3, D_FF)),
        _const_spec((1, D_FF)),
        _const_spec((D_FF, D_MODEL)),
    ]


def _post0_kernel(x_ref, o_ref, mod_ref, wo_ref, nfw_ref, wup_ref, cw_ref, cb_ref, wdn_ref,
                  mod1_ref, nw1_ref, w1_ref, out_ref, gb_ref, cu_ref, hid_scr, *, period):
    x1s = []
    for bi, r0, x in _halves(x_ref):
        half = x.shape[0]
        o = jnp.concatenate([o_ref[bi, hh, r0:r0 + half, :] for hh in range(N_HEADS)], axis=1)
        x1s.append(x + mod_ref[0, 2:3, :] * _dot(o, wo_ref[...]))
    x2s = _conv_ffn(x1s, mod_ref, nfw_ref, wup_ref, cw_ref, cb_ref, wdn_ref, hid_scr, period)
    hs = [_modulate(x2, nw1_ref[...], mod1_ref[0, 0:1, :], mod1_ref[0, 1:2, :]).astype(BF16)
          for x2 in x2s]
    where = [(bi, r0) for bi, r0, _ in _halves(x_ref)]
    for (bi, r0), x2, h in zip(where, x2s, hs):
        half = x2.shape[0]
        out_ref[bi, r0:r0 + half, :] = x2
        gb_ref[bi, r0:r0 + half, :] = _dot(h, w1_ref[:, 0:D_MODEL]).astype(gb_ref.dtype)
    for (bi, r0), h in zip(where, hs):
        half = h.shape[0]
        gate_c = _dot(h, w1_ref[:, D_MODEL:2 * D_MODEL])
        u = _dot(h, w1_ref[:, 2 * D_MODEL:3 * D_MODEL])
        cu_ref[bi, r0:r0 + half, :] = (gate_c * u).astype(cu_ref.dtype)


def _post0(x, o, mod, wo, nfw, wup, cw, cb, wdn, mod1, nw1, w1, per_seq_mod, period):
    bsz, n, _ = x.shape
    tb, tl = _tile(n)
    mod_map = (lambda b, i: (b, 0, 0)) if per_seq_mod else (lambda b, i: (0, 0, 0))
    tok_spec = pl.BlockSpec((tb, tl, D_MODEL), lambda b, i: (b, i, 0))
    return pl.pallas_call(
        functools.partial(_post0_kernel, period=period),
        grid=(bsz // tb, n // tl),
        in_specs=[
            tok_spec,
            pl.BlockSpec((tb, N_HEADS, tl, D_HEAD), lambda b, i: (b, 0, i, 0)),
            pl.BlockSpec((1, N_MOD, D_MODEL), mod_map),
            _const_spec((D_MODEL, D_MODEL)),
        ] + _ffn_specs() + [
            pl.BlockSpec((1, N_MOD, D_MODEL), mod_map),
            _const_spec((1, D_MODEL)),
            _const_spec(w1.shape),
        ],
        out_specs=[tok_spec] * 3,
        out_shape=[jax.ShapeDtypeStruct(x.shape, F32), jax.ShapeDtypeStruct(x.shape, BF16),
                   jax.ShapeDtypeStruct(x.shape, BF16)],
        scratch_shapes=[pltpu.VMEM((tb * tl, D_FF), BF16)],
        compiler_params=_params(2),
        name="post0",
    )(x, o, mod, wo, nfw, wup, cw, cb, wdn, mod1, nw1, w1)


def _post1_kernel(*refs, grid_rows, period):
    if grid_rows:
        (x_ref, gb_ref, cu_ref, cup_ref, cun_ref, mod_ref, scw_ref, scb_ref, wo_ref,
         nfw_ref, wup_ref, cw_ref, cb_ref, wdn_ref, fnw_ref, out_ref, hid_scr) = refs
    else:
        (x_ref, gb_ref, cu_ref, mod_ref, scw_ref, scb_ref, wo_ref,
         nfw_ref, wup_ref, cw_ref, cb_ref, wdn_ref, fnw_ref, out_ref, hid_scr) = refs
    cu = _rows(cu_ref).astype(F32)
    t = cu.shape[0]
    if grid_rows:
        i = pl.program_id(1)
        halo_p = jnp.where(i == 0, 0.0, cup_ref[0].astype(F32))
        halo_n = jnp.where(i == pl.num_programs(1) - 1, 0.0, cun_ref[0].astype(F32))
        prev = jnp.concatenate([halo_p, cu[:t - GRID_W]], axis=0)
        nxt = jnp.concatenate([cu[GRID_W:], halo_n], axis=0)
    else:
        pos = lax.broadcasted_iota(jnp.int32, (t, 1), 0) % period
        prev = jnp.where(pos == 0, 0.0, pltpu.roll(cu, 1, 0))
        nxt = jnp.where(pos == period - 1, 0.0, pltpu.roll(cu, t - 1, 0))
    z = prev * scw_ref[0:1, :] + cu * scw_ref[1:2, :] + nxt * scw_ref[2:3, :] + scb_ref[...]
    m = (_rows(gb_ref).astype(F32) * z).astype(BF16)
    halves = _halves(x_ref)
    half = t // 2
    x1s = [x + mod_ref[0, 2:3, :] * _dot(m[s * half:(s + 1) * half], wo_ref[...])
           for s, (_, _, x) in enumerate(halves)]
    x2s = _conv_ffn(x1s, mod_ref, nfw_ref, wup_ref, cw_ref, cb_ref, wdn_ref, hid_scr, period)
    for (bi, r0, _), x2 in zip(halves, x2s):
        out_ref[bi, r0:r0 + half, :] = _rms(x2, fnw_ref[...])


def _post1(x, gb, cu, mod, scw, scb, wo, nfw, wup, cw, cb, wdn, fnw, per_seq_mod, grid_rows, period):
    bsz, n, _ = x.shape
    tb, tl = _tile(n)
    mod_map = (lambda b, i: (b, 0, 0)) if per_seq_mod else (lambda b, i: (0, 0, 0))
    tok_spec = pl.BlockSpec((tb, tl, D_MODEL), lambda b, i: (b, i, 0))
    in_specs = [tok_spec, tok_spec, tok_spec]
    args = [x, gb, cu]
    if grid_rows:
        r = tl // GRID_W
        n_rows = n // GRID_W
        in_specs += [
            pl.BlockSpec((1, GRID_W, D_MODEL), lambda b, i: (b, jnp.maximum(i * r - 1, 0), 0)),
            pl.BlockSpec((1, GRID_W, D_MODEL), lambda b, i: (b, jnp.minimum((i + 1) * r, n_rows - 1), 0)),
        ]
        args += [cu, cu]
    in_specs += [
        pl.BlockSpec((1, N_MOD, D_MODEL), mod_map),
        _const_spec((3, D_MODEL)),
        _const_spec((1, D_MODEL)),
        _const_spec((D_MODEL, D_MODEL)),
    ] + _ffn_specs() + [_const_spec((1, D_MODEL))]
    args += [mod, scw, scb, wo, nfw, wup, cw, cb, wdn, fnw]
    return pl.pallas_call(
        functools.partial(_post1_kernel, grid_rows=grid_rows, period=period),
        grid=(bsz // tb, n // tl),
        in_specs=in_specs,
        out_specs=tok_spec,
        out_shape=jax.ShapeDtypeStruct(x.shape, F32),
        scratch_shapes=[pltpu.VMEM((tb * tl, D_FF), BF16)],
        compiler_params=_params(2),
        name="post1",
    )(*args)


def kernel(x_prompt, x_sample, state_hgrn, c, c_ctx, ada_w, ada_b, norm_mix_w, norm_ffn_w,
           hgrn_w_in, hgrn_lower_bounds, hgrn_gnorm_w, hgrn_w_out,
           sconv_w_in, sconv_conv_w, sconv_conv_b, sconv_w_out,
           ffn_w_up, ffn_conv_w, ffn_conv_b, ffn_w_down, final_norm_w):
    seq_p = x_prompt.shape[1]

    cvecs = jnp.concatenate([c_ctx[None, :], c], axis=0)
    mod = _ada(cvecs, ada_w, ada_b)
    mod = mod.reshape(mod.shape[0], 8, N_MOD, D_MODEL)
    mod_p = mod[:, 0:1]
    mod_s = mod[:, 1:1 + c.shape[0]]

    row = lambda a: a.reshape(1, -1)
    w_in0 = hgrn_w_in[0].astype(BF16)
    gnw = row(hgrn_gnorm_w[0])
    lbp = hgrn_lower_bounds.reshape(-1, hgrn_lower_bounds.shape[-1])

    def mixer0(x, mods, s0, per_seq_mod, emit_state, proj_casts, casts):
        (q, ff, fb, v, g), proj_cast = _inproj0(x, mods[0], row(norm_mix_w[0]), w_in0, lbp,
                                                per_seq_mod, proj_casts)
        heads = min(N_HEADS, max(1, SCAN_LANES // (x.shape[1] // SCAN_CHUNK)))
        o, s_fin, scan_cast = _gla(q, ff, fb, v, g, s0, gnw, emit_state, heads, casts)
        return o, s_fin, proj_cast + scan_cast

    def post0(x, o, mods, per_seq_mod, ffn_period):
        return _post0(x, o, mods[0], w_out0, row(norm_ffn_w[0]), w_up0, ffn_conv_w[0],
                      row(ffn_conv_b[0]), w_dn0, mods[1], row(norm_mix_w[1]), w_in1,
                      per_seq_mod, ffn_period)

    def post1(x, gb, cu, mods, per_seq_mod, grid_rows, ffn_period):
        return _post1(x, gb, cu, mods[1], sconv_conv_w[0], row(sconv_conv_b[0]), w_out1,
                      row(norm_ffn_w[1]), w_up1, ffn_conv_w[1], row(ffn_conv_b[1]), w_dn1,
                      row(final_norm_w), per_seq_mod, grid_rows, ffn_period)

    o_p, new_state, (w_up0, w_out0, w_dn0, w_in1) = mixer0(
        x_prompt, mod_p, None, False, True,
        [(ffn_w_up, 0)], [(hgrn_w_out, 0), (ffn_w_down, 0), (sconv_w_in, 0)])
    xp, gb_p, cu_p = post0(x_prompt, o_p, mod_p, False, seq_p)
    o_s, _, (w_out1, w_up1, w_dn1) = mixer0(
        x_sample, mod_s, state_hgrn, True, False,
        [], [(sconv_w_out, 0), (ffn_w_up, 1), (ffn_w_down, 1)])
    y_prompt = post1(xp, gb_p, cu_p, mod_p, False, False, seq_p)
    xs, gb_s, cu_s = post0(x_sample, o_s, mod_s, True, GRID_W)
    y_sample = post1(xs, gb_s, cu_s, mod_s, True, True, GRID_W)
    return (y_prompt, y_sample, new_state.astype(state_hgrn.dtype))
```

```python
import functools

import jax
import jax.numpy as jnp
from jax import lax
from jax.experimental import pallas as pl
from jax.experimental.pallas import tpu as pltpu

D_MODEL = 1024
N_HEADS = 8
D_HEAD = 128
D_FF = 2816
N_MOD = 6
GRID_W = 64
EPS = 1e-6

LANES = 128
SCAN_CHUNK = 64
SCAN_LANES = 64
SCAN_GROUP = 8
TOKEN_TILE = 512
FF_CHUNK = 256
ADA_ROWS = 256
VMEM_LIMIT = 56 * 1024 * 1024

F32 = jnp.float32
BF16 = jnp.bfloat16


def _sigmoid(x):
    return 1.0 / (1.0 + jnp.exp(-x))


def _silu(x):
    return x * _sigmoid(x)


def _rms(x, w):
    return x * lax.rsqrt(jnp.mean(x * x, axis=-1, keepdims=True) + EPS) * w


def _modulate(x, w, shift, scale):
    return _rms(x, w) * (1.0 + scale) + shift


def _dot(a, b):
    return jnp.dot(a, b, preferred_element_type=F32)


def _dot_nt(a, b):
    return lax.dot_general(a, b, (((1,), (1,)), ((), ())), preferred_element_type=F32)


def _dot_tn(a, b):
    return lax.dot_general(a, b, (((0,), (0,)), ((), ())), preferred_element_type=F32)


def _params(n_grid):
    return pltpu.CompilerParams(
        dimension_semantics=("arbitrary",) * n_grid,
        vmem_limit_bytes=VMEM_LIMIT,
    )


def _const_spec(shape):
    nd = len(shape)
    return pl.BlockSpec(shape, lambda *_: (0,) * nd, pipeline_mode=pl.Buffered(1))


def _cast_riders(casts, steps, step_of):
    in_specs, args, out_specs, out_shape = [], [], [], []
    for w, layer in casts:
        _, rows, cols = w.shape
        slab = rows // steps
        in_specs.append(pl.BlockSpec((1, 1, slab, cols),
                                     lambda *g, layer=layer: (layer, step_of(*g), 0, 0)))
        args.append(w.reshape(w.shape[0], steps, slab, cols))
        out_specs.append(pl.BlockSpec((1, slab, cols), lambda *g: (step_of(*g), 0, 0)))
        out_shape.append(jax.ShapeDtypeStruct((steps, slab, cols), BF16))
    return in_specs, args, out_specs, out_shape


def _cast_slabs(cast_in, cast_out):
    for src, dst in zip(cast_in, cast_out):
        dst[0] = src[0, 0].astype(dst.dtype)


def _cast_results(outs, casts):
    return [o.reshape(w.shape[1], w.shape[2]) for o, (w, _) in zip(outs, casts)]


def _ada_kernel(c_ref, wa_ref, wb_ref, b_ref, o_ref, s_scr):
    k = pl.program_id(1)
    rows, n_out = wa_ref.shape[1], wa_ref.shape[2]

    @pl.when((pl.program_id(0) == 0) & (k == 0))
    def _():
        s_scr[...] = _silu(c_ref[...])

    @pl.when(k == 0)
    def _():
        o_ref[0] = jnp.broadcast_to(b_ref[0], o_ref.shape[1:])

    for half, w_ref in enumerate((wa_ref, wb_ref)):
        r0 = pl.multiple_of((2 * k + half) * rows, rows)
        for r in range(c_ref.shape[0]):
            s = s_scr[r, pl.ds(r0, rows), :]
            parts = [
                jnp.sum(w_ref[0, :, j * LANES:(j + 1) * LANES] * s, axis=0, keepdims=True)
                for j in range(n_out // LANES)
            ]
            o_ref[0, r:r + 1, :] += jnp.concatenate(parts, axis=1)


def _ada(cvecs, ada_w, ada_b):
    depth, d_in, n_out = ada_w.shape
    cb = jnp.broadcast_to(cvecs[:, :, None], cvecs.shape + (LANES,))
    return pl.pallas_call(
        _ada_kernel,
        grid=(depth, d_in // ADA_ROWS),
        in_specs=[
            pl.BlockSpec(cb.shape, lambda l, k: (0, 0, 0)),
            pl.BlockSpec((1, ADA_ROWS // 2, n_out), lambda l, k: (l, 2 * k, 0)),
            pl.BlockSpec((1, ADA_ROWS // 2, n_out), lambda l, k: (l, 2 * k + 1, 0)),
            pl.BlockSpec((1, 1, n_out), lambda l, k: (l, 0, 0)),
        ],
        out_specs=pl.BlockSpec((1, 8, n_out), lambda l, k: (l, 0, 0)),
        out_shape=jax.ShapeDtypeStruct((depth, 8, n_out), F32),
        scratch_shapes=[pltpu.VMEM(cb.shape, F32)],
        compiler_params=_params(2),
        name="ada",
    )(cb, ada_w, ada_w, ada_b.reshape(depth, 1, n_out))


def _inproj0_kernel(x_ref, mod_ref, nw_ref, w_ref, lbp_ref, *refs, n_casts):
    cast_in, (q_ref, ff_ref, fb_ref, v_ref, g_ref) = refs[:n_casts], refs[n_casts:n_casts + 5]
    _cast_slabs(cast_in, refs[n_casts + 5:])
    x = _rows(x_ref)
    tb, _, tl, _ = q_ref.shape
    half = x.shape[0] // 2
    hs = [_modulate(x[s * half:(s + 1) * half], nw_ref[...], mod_ref[0, 0:1, :],
                    mod_ref[0, 1:2, :]).astype(BF16) for s in range(2)]

    n_lb = lbp_ref.shape[0] // 2

    def lower_bound(d):
        rows = [lbp_ref[d * n_lb + i:d * n_lb + i + 1, :] for i in range(n_lb)]
        top = functools.reduce(jnp.maximum, rows)
        e = [jnp.exp(r - top) for r in rows]
        return e[0] / functools.reduce(jnp.add, e)

    lb = [lower_bound(0), lower_bound(1)]

    def emit(j, ref, act):
        vals = [_dot(h, w_ref[:, j * D_MODEL:(j + 1) * D_MODEL]) for h in hs]
        for s, val in enumerate(vals):
            val = act(val)
            b, r0 = divmod(s * half, tl)
            for hh in range(N_HEADS):
                ref[b, hh, r0:r0 + half, :] = val[:, hh * D_HEAD:(hh + 1) * D_HEAD].astype(ref.dtype)

    emit(0, q_ref, _silu)
    emit(1, ff_ref, lambda p: lb[0] + (1.0 - lb[0]) * _sigmoid(p))
    emit(2, fb_ref, lambda p: lb[1] + (1.0 - lb[1]) * _sigmoid(p))
    emit(3, v_ref, lambda p: p)
    emit(4, g_ref, _silu)


def _inproj0(x, mod, nw, w_in, lbp, per_seq_mod, casts):
    bsz, n, _ = x.shape
    tb, tl = _tile(n)
    n_i = n // tl
    mod_map = (lambda b, i: (b, 0, 0)) if per_seq_mod else (lambda b, i: (0, 0, 0))
    head_spec = pl.BlockSpec((tb, N_HEADS, tl, D_HEAD), lambda b, i: (b, 0, i, 0))
    shp = (bsz, N_HEADS, n, D_HEAD)
    c_in, c_args, c_out, c_shape = _cast_riders(casts, (bsz // tb) * n_i, lambda b, i: b * n_i + i)
    outs = pl.pallas_call(
        functools.partial(_inproj0_kernel, n_casts=len(casts)),
        grid=(bsz // tb, n_i),
        in_specs=[
            pl.BlockSpec((tb, tl, D_MODEL), lambda b, i: (b, i, 0)),
            pl.BlockSpec((1, N_MOD, D_MODEL), mod_map),
            _const_spec((1, D_MODEL)),
            _const_spec(w_in.shape),
            _const_spec(lbp.shape),
        ] + c_in,
        out_specs=[head_spec] * 5 + c_out,
        out_shape=[
            jax.ShapeDtypeStruct(shp, BF16),
            jax.ShapeDtypeStruct(shp, F32),
            jax.ShapeDtypeStruct(shp, F32),
            jax.ShapeDtypeStruct(shp, BF16),
            jax.ShapeDtypeStruct(shp, BF16),
        ] + c_shape,
        compiler_params=_params(2),
        name="inproj0",
    )(x, mod, nw, w_in, lbp, *c_args)
    return outs[:5], _cast_results(outs[5:], casts)


def _cumsum_rows(tri, x):
    hi = x.astype(BF16)
    lo = (x - hi.astype(F32)).astype(BF16)
    s = _dot(tri, jnp.concatenate([hi, lo], axis=1))
    return s[:, :D_HEAD] + s[:, D_HEAD:]


def _gla_kernel(*refs, seq_len, heads, zero_init, emit_state, n_casts):
    q_ref, ff_ref, fb_ref, v_ref, g_ref = refs[:5]
    pos = 5
    s0_ref = None
    if not zero_init:
        s0_ref = refs[pos]
        pos += 1
    gnw_ref = refs[pos]
    cast_in = refs[pos + 1:pos + 1 + n_casts]
    pos += 1 + n_casts
    o_ref = refs[pos]
    pos += 1
    sf_ref = None
    if emit_state:
        sf_ref = refs[pos]
        pos += 1
    cast_out = refs[pos:pos + n_casts]

    _cast_slabs(cast_in, cast_out)

    c = SCAN_CHUNK
    n_chunks = seq_len // c
    row = lax.broadcasted_iota(jnp.int32, (c, c), 0)
    col = lax.broadcasted_iota(jnp.int32, (c, c), 1)
    tri_lo = (col <= row).astype(BF16)
    tri_up = (col >= row).astype(BF16)
    row2 = lax.broadcasted_iota(jnp.int32, (c, 2 * c), 0)
    col2 = lax.broadcasted_iota(jnp.int32, (c, 2 * c), 1)
    mask2 = ((col2 < c) & (col2 <= row2)) | ((col2 >= c) & (col2 - c >= row2))
    zeros = jnp.zeros((c, D_HEAD), BF16)
    gnw = gnw_ref[...]
    directions = ((tri_lo, c // 2 - 1, c - 1), (tri_up, c // 2, 0))

    lanes = [dict(h=h, r0=n * c) for h in range(heads) for n in range(n_chunks)]
    states = []
    for h in range(heads):
        for d in range(2):
            if zero_init:
                states.append(jnp.zeros((D_HEAD, D_HEAD), F32))
            else:
                states.append(s0_ref[0, 0, d, h].T)

    def stage_a(ln):
        rows = pl.ds(ln["r0"], c)
        ln["f"] = [ff_ref[0, ln["h"], rows, :], fb_ref[0, ln["h"], rows, :]]
        ln["cum"] = [_cumsum_rows(tri, jnp.log2(f))
                     for (tri, _, _), f in zip(directions, ln["f"])]

    def stage_b(ln):
        rows = pl.ds(ln["r0"], c)
        q = q_ref[0, ln["h"], rows, :].astype(F32)
        ln["v"] = v_ref[0, ln["h"], rows, :]
        q_mid, k_mid, q_dec, k_state, ln["decay"] = [], [], [], [], []
        for d, (_, mid, last) in enumerate(directions):
            cum = ln["cum"][d]
            cum_mid = cum[mid:mid + 1]
            cum_last = cum[last:last + 1]
            qm = q * jnp.exp2(cum - cum_mid)
            km = (1.0 - ln["f"][d]) * jnp.exp2(cum_mid - cum)
            q_mid.append(qm.astype(BF16))
            k_mid.append(km.astype(BF16))
            q_dec.append((qm * jnp.exp2(cum_mid)).astype(BF16))
            k_state.append((km * jnp.exp2(cum_last - cum_mid)).astype(BF16))
            ln["decay"].append(jnp.exp2(cum_last))
        k_pair = jnp.concatenate([jnp.concatenate([k_mid[0], zeros], axis=1),
                                  jnp.concatenate([zeros, k_mid[1]], axis=1)], axis=0)
        ln["scores"] = _dot_nt(jnp.concatenate(q_mid, axis=1), k_pair)
        ln["kv"] = _dot_tn(ln["v"], jnp.concatenate(k_state, axis=1))
        ln["q_dec"] = jnp.concatenate(q_dec, axis=1)

    def stage_c(h):
        mine = [ln for ln in lanes if ln["h"] == h]
        for d, order in ((0, mine), (1, mine[::-1])):
            st = states[2 * h + d]
            for ln in order:
                ln.setdefault("st", [None, None])[d] = st.astype(BF16).T
                st = st * ln["decay"][d] + ln["kv"][:, d * D_HEAD:(d + 1) * D_HEAD]
            states[2 * h + d] = st

    def stage_d(ln):
        rows = pl.ds(ln["r0"], c)
        p = jnp.where(mask2, ln["scores"], 0.0).astype(BF16)
        lhs = jnp.concatenate([ln["q_dec"], p], axis=1)
        rhs = jnp.concatenate([ln["st"][0], ln["st"][1], ln["v"], ln["v"]], axis=0)
        tot = _dot(lhs, rhs)
        gate = g_ref[0, ln["h"], rows, :].astype(F32)
        o_ref[0, ln["h"], rows, :] = (_rms(tot, gnw) * gate).astype(o_ref.dtype)

    for g0 in range(0, len(lanes), SCAN_GROUP):
        for ln in lanes[g0:g0 + SCAN_GROUP]:
            stage_a(ln)
        for ln in lanes[g0:g0 + SCAN_GROUP]:
            stage_b(ln)
    for h in range(heads):
        stage_c(h)
    for ln in lanes:
        stage_d(ln)
    if emit_state:
        for h in range(heads):
            sf_ref[0, 0, 0, h] = states[2 * h].T
            sf_ref[0, 0, 1, h] = states[2 * h + 1].T


def _gla(q, ff, fb, v, g, s0, gnw, emit_state, heads, casts):
    bsz, _, n, _ = q.shape
    n_hb = N_HEADS // heads
    steps = bsz * n_hb
    zero_init = s0 is None
    seq_spec = pl.BlockSpec((1, heads, n, D_HEAD), lambda b, h: (b, h, 0, 0))
    state_spec = pl.BlockSpec((1, 1, 2, heads, D_HEAD, D_HEAD), lambda b, h: (b, 0, 0, h, 0, 0))
    in_specs = [seq_spec] * 5
    args = [q, ff, fb, v, g]
    if not zero_init:
        in_specs.append(state_spec)
        args.append(s0)
    in_specs.append(_const_spec((1, D_HEAD)))
    args.append(gnw)
    out_specs = [seq_spec]
    out_shape = [jax.ShapeDtypeStruct(q.shape, BF16)]
    if emit_state:
        out_specs.append(state_spec)
        out_shape.append(jax.ShapeDtypeStruct((bsz, 1, 2, N_HEADS, D_HEAD, D_HEAD), F32))
    c_in, c_args, c_out, c_shape = _cast_riders(casts, steps, lambda b, h: b * n_hb + h)
    in_specs += c_in
    args += c_args
    out_specs += c_out
    out_shape += c_shape
    outs = pl.pallas_call(
        functools.partial(_gla_kernel, seq_len=n, heads=heads, zero_init=zero_init,
                          emit_state=emit_state, n_casts=len(casts)),
        grid=(bsz, n_hb),
        in_specs=in_specs,
        out_specs=out_specs,
        out_shape=out_shape,
        compiler_params=_params(2),
        name="gla",
    )(*args)
    n_main = 2 if emit_state else 1
    return outs[0], (outs[1] if emit_state else None), _cast_results(outs[n_main:], casts)


def _conv_ffn(x1s, mod_ref, nfw_ref, wup_ref, cw_ref, cb_ref, wdn_ref, hid_scr, period):
    t = x1s[0].shape[0]
    hs = [_modulate(x1, nfw_ref[...], mod_ref[0, 3:4, :], mod_ref[0, 4:5, :]).astype(BF16)
          for x1 in x1s]
    pos = lax.broadcasted_iota(jnp.int32, (t, 1), 0) % period
    first = pos == 0
    last = pos == period - 1
    for j in range(D_FF // FF_CHUNK):
        lo = j * FF_CHUNK
        for s, h in enumerate(hs):
            a = _dot(h, wup_ref[:, lo:lo + FF_CHUNK])
            g = _dot(h, wup_ref[:, D_FF + lo:D_FF + lo + FF_CHUNK])
            prev = jnp.where(first, 0.0, pltpu.roll(a, 1, 0))
            nxt = jnp.where(last, 0.0, pltpu.roll(a, t - 1, 0))
            ac = (prev * cw_ref[0:1, lo:lo + FF_CHUNK] + a * cw_ref[1:2, lo:lo + FF_CHUNK]
                  + nxt * cw_ref[2:3, lo:lo + FF_CHUNK] + cb_ref[:, lo:lo + FF_CHUNK])
            hid_scr[s * t:(s + 1) * t, lo:lo + FF_CHUNK] = (_silu(ac) * g).astype(BF16)
    return [x1 + mod_ref[0, 5:6, :] * _dot(hid_scr[s * t:(s + 1) * t, :], wdn_ref[...])
            for s, x1 in enumerate(x1s)]


def _tile(n):
    return (1, TOKEN_TILE) if n >= TOKEN_TILE else (TOKEN_TILE // n, n)


def _rows(ref):
    v = ref[...]
    return v.reshape(v.shape[0] * v.shape[1], v.shape[2])


def _halves(ref):
    tb, tl, _ = ref.shape
    half = tb * tl // 2
    out = []
    for s in range(2):
        bi, r0 = divmod(s * half, tl)
        out.append((bi, r0, ref[bi, r0:r0 + half, :]))
    return out


def _ffn_specs():
    return [
        _const_spec((1, D_MODEL)),
        _const_spec((D_MODEL, 2 * D_FF)),
        _const_spec((3, D_FF)),
        _const_spec((1, D_FF)),
        _const_spec((D_FF, D_MODEL)),
    ]


def _post0_kernel(x_ref, o_ref, mod_ref, wo_ref, nfw_ref, wup_ref, cw_ref, cb_ref, wdn_ref,
                  mod1_ref, nw1_ref, w1_ref, out_ref, gb_ref, cu_ref, hid_scr, *, period):
    x1s = []
    for bi, r0, x in _halves(x_ref):
        half = x.shape[0]
        o = jnp.concatenate([o_ref[bi, hh, r0:r0 + half, :] for hh in range(N_HEADS)], axis=1)
        x1s.append(x + mod_ref[0, 2:3, :] * _dot(o, wo_ref[...]))
    x2s = _conv_ffn(x1s, mod_ref, nfw_ref, wup_ref, cw_ref, cb_ref, wdn_ref, hid_scr, period)
    hs = [_modulate(x2, nw1_ref[...], mod1_ref[0, 0:1, :], mod1_ref[0, 1:2, :]).astype(BF16)
          for x2 in x2s]
    where = [(bi, r0) for bi, r0, _ in _halves(x_ref)]
    for (bi, r0), x2, h in zip(where, x2s, hs):
        half = x2.shape[0]
        out_ref[bi, r0:r0 + half, :] = x2
        gb_ref[bi, r0:r0 + half, :] = _dot(h, w1_ref[:, 0:D_MODEL]).astype(gb_ref.dtype)
    for (bi, r0), h in zip(where, hs):
        half = h.shape[0]
        gate_c = _dot(h, w1_ref[:, D_MODEL:2 * D_MODEL])
        u = _dot(h, w1_ref[:, 2 * D_MODEL:3 * D_MODEL])
        cu_ref[bi, r0:r0 + half, :] = (gate_c * u).astype(cu_ref.dtype)


def _post0(x, o, mod, wo, nfw, wup, cw, cb, wdn, mod1, nw1, w1, per_seq_mod, period):
    bsz, n, _ = x.shape
    tb, tl = _tile(n)
    mod_map = (lambda b, i: (b, 0, 0)) if per_seq_mod else (lambda b, i: (0, 0, 0))
    tok_spec = pl.BlockSpec((tb, tl, D_MODEL), lambda b, i: (b, i, 0))
    return pl.pallas_call(
        functools.partial(_post0_kernel, period=period),
        grid=(bsz // tb, n // tl),
        in_specs=[
            tok_spec,
            pl.BlockSpec((tb, N_HEADS, tl, D_HEAD), lambda b, i: (b, 0, i, 0)),
            pl.BlockSpec((1, N_MOD, D_MODEL), mod_map),
            _const_spec((D_MODEL, D_MODEL)),
        ] + _ffn_specs() + [
            pl.BlockSpec((1, N_MOD, D_MODEL), mod_map),
            _const_spec((1, D_MODEL)),
            _const_spec(w1.shape),
        ],
        out_specs=[tok_spec] * 3,
        out_shape=[jax.ShapeDtypeStruct(x.shape, F32), jax.ShapeDtypeStruct(x.shape, BF16),
                   jax.ShapeDtypeStruct(x.shape, BF16)],
        scratch_shapes=[pltpu.VMEM((tb * tl, D_FF), BF16)],
        compiler_params=_params(2),
        name="post0",
    )(x, o, mod, wo, nfw, wup, cw, cb, wdn, mod1, nw1, w1)


def _post1_kernel(*refs, grid_rows, period):
    if grid_rows:
        (x_ref, gb_ref, cu_ref, cup_ref, cun_ref, mod_ref, scw_ref, scb_ref, wo_ref,
         nfw_ref, wup_ref, cw_ref, cb_ref, wdn_ref, fnw_ref, out_ref, hid_scr) = refs
    else:
        (x_ref, gb_ref, cu_ref, mod_ref, scw_ref, scb_ref, wo_ref,
         nfw_ref, wup_ref, cw_ref, cb_ref, wdn_ref, fnw_ref, out_ref, hid_scr) = refs
    cu = _rows(cu_ref).astype(F32)
    t = cu.shape[0]
    if grid_rows:
        i = pl.program_id(1)
        halo_p = jnp.where(i == 0, 0.0, cup_ref[0].astype(F32))
        halo_n = jnp.where(i == pl.num_programs(1) - 1, 0.0, cun_ref[0].astype(F32))
        prev = jnp.concatenate([halo_p, cu[:t - GRID_W]], axis=0)
        nxt = jnp.concatenate([cu[GRID_W:], halo_n], axis=0)
    else:
        pos = lax.broadcasted_iota(jnp.int32, (t, 1), 0) % period
        prev = jnp.where(pos == 0, 0.0, pltpu.roll(cu, 1, 0))
        nxt = jnp.where(pos == period - 1, 0.0, pltpu.roll(cu, t - 1, 0))
    z = prev * scw_ref[0:1, :] + cu * scw_ref[1:2, :] + nxt * scw_ref[2:3, :] + scb_ref[...]
    m = (_rows(gb_ref).astype(F32) * z).astype(BF16)
    halves = _halves(x_ref)
    half = t // 2
    x1s = [x + mod_ref[0, 2:3, :] * _dot(m[s * half:(s + 1) * half], wo_ref[...])
           for s, (_, _, x) in enumerate(halves)]
    x2s = _conv_ffn(x1s, mod_ref, nfw_ref, wup_ref, cw_ref, cb_ref, wdn_ref, hid_scr, period)
    for (bi, r0, _), x2 in zip(halves, x2s):
        out_ref[bi, r0:r0 + half, :] = _rms(x2, fnw_ref[...])


def _post1(x, gb, cu, mod, scw, scb, wo, nfw, wup, cw, cb, wdn, fnw, per_seq_mod, grid_rows, period):
    bsz, n, _ = x.shape
    tb, tl = _tile(n)
    mod_map = (lambda b, i: (b, 0, 0)) if per_seq_mod else (lambda b, i: (0, 0, 0))
    tok_spec = pl.BlockSpec((tb, tl, D_MODEL), lambda b, i: (b, i, 0))
    in_specs = [tok_spec, tok_spec, tok_spec]
    args = [x, gb, cu]
    if grid_rows:
        r = tl // GRID_W
        n_rows = n // GRID_W
        in_specs += [
            pl.BlockSpec((1, GRID_W, D_MODEL), lambda b, i: (b, jnp.maximum(i * r - 1, 0), 0)),
            pl.BlockSpec((1, GRID_W, D_MODEL), lambda b, i: (b, jnp.minimum((i + 1) * r, n_rows - 1), 0)),
        ]
        args += [cu, cu]
    in_specs += [
        pl.BlockSpec((1, N_MOD, D_MODEL), mod_map),
        _const_spec((3, D_MODEL)),
        _const_spec((1, D_MODEL)),
        _const_spec((D_MODEL, D_MODEL)),
    ] + _ffn_specs() + [_const_spec((1, D_MODEL))]
    args += [mod, scw, scb, wo, nfw, wup, cw, cb, wdn, fnw]
    return pl.pallas_call(
        functools.partial(_post1_kernel, grid_rows=grid_rows, period=period),
        grid=(bsz // tb, n // tl),
        in_specs=in_specs,
        out_specs=tok_spec,
        out_shape=jax.ShapeDtypeStruct(x.shape, F32),
        scratch_shapes=[pltpu.VMEM((tb * tl, D_FF), BF16)],
        compiler_params=_params(2),
        name="post1",
    )(*args)


def kernel(x_prompt, x_sample, state_hgrn, c, c_ctx, ada_w, ada_b, norm_mix_w, norm_ffn_w,
           hgrn_w_in, hgrn_lower_bounds, hgrn_gnorm_w, hgrn_w_out,
           sconv_w_in, sconv_conv_w, sconv_conv_b, sconv_w_out,
           ffn_w_up, ffn_conv_w, ffn_conv_b, ffn_w_down, final_norm_w):
    seq_p = x_prompt.shape[1]

    cvecs = jnp.concatenate([c_ctx[None, :], c], axis=0)
    mod = _ada(cvecs, ada_w, ada_b)
    mod = mod.reshape(mod.shape[0], 8, N_MOD, D_MODEL)
    mod_p = mod[:, 0:1]
    mod_s = mod[:, 1:1 + c.shape[0]]

    row = lambda a: a.reshape(1, -1)
    w_in0 = hgrn_w_in[0].astype(BF16)
    gnw = row(hgrn_gnorm_w[0])
    lbp = hgrn_lower_bounds.reshape(-1, hgrn_lower_bounds.shape[-1])

    def mixer0(x, mods, s0, per_seq_mod, emit_state, proj_casts, casts):
        (q, ff, fb, v, g), proj_cast = _inproj0(x, mods[0], row(norm_mix_w[0]), w_in0, lbp,
                                                per_seq_mod, proj_casts)
        heads = min(N_HEADS, max(1, SCAN_LANES // (x.shape[1] // SCAN_CHUNK)))
        o, s_fin, scan_cast = _gla(q, ff, fb, v, g, s0, gnw, emit_state, heads, casts)
        return o, s_fin, proj_cast + scan_cast

    def post0(x, o, mods, per_seq_mod, ffn_period):
        return _post0(x, o, mods[0], w_out0, row(norm_ffn_w[0]), w_up0, ffn_conv_w[0],
                      row(ffn_conv_b[0]), w_dn0, mods[1], row(norm_mix_w[1]), w_in1,
                      per_seq_mod, ffn_period)

    def post1(x, gb, cu, mods, per_seq_mod, grid_rows, ffn_period):
        return _post1(x, gb, cu, mods[1], sconv_conv_w[0], row(sconv_conv_b[0]), w_out1,
                      row(norm_ffn_w[1]), w_up1, ffn_conv_w[1], row(ffn_conv_b[1]), w_dn1,
                      row(final_norm_w), per_seq_mod, grid_rows, ffn_period)

    o_p, new_state, (w_up0, w_out0, w_dn0, w_in1) = mixer0(
        x_prompt, mod_p, None, False, True,
        [(ffn_w_up, 0)], [(hgrn_w_out, 0), (ffn_w_down, 0), (sconv_w_in, 0)])
    xp, gb_p, cu_p = post0(x_prompt, o_p, mod_p, False, seq_p)
    o_s, _, (w_out1, w_up1, w_dn1) = mixer0(
        x_sample, mod_s, state_hgrn, True, False,
        [], [(sconv_w_out, 0), (ffn_w_up, 1), (ffn_w_down, 1)])
    y_prompt = post1(xp, gb_p, cu_p, mod_p, False, False, seq_p)
    xs, gb_s, cu_s = post0(x_sample, o_s, mod_s, True, GRID_W)
    y_sample = post1(xs, gb_s, cu_s, mod_s, True, True, GRID_W)
    return (y_prompt, y_sample, new_state.astype(state_hgrn.dtype))
```

```python
import functools

import jax
import jax.numpy as jnp
from jax import lax
from jax.experimental import pallas as pl
from jax.experimental.pallas import tpu as pltpu

D_MODEL = 1024
N_HEADS = 8
D_HEAD = 128
D_FF = 2816
N_MOD = 6
GRID_W = 64
EPS = 1e-6

LANES = 128
SCAN_CHUNK = 64
SCAN_LANES = 64
SCAN_GROUP = 8
TOKEN_TILE = 512
FF_CHUNK = 512
ADA_ROWS = 256
VMEM_LIMIT = 56 * 1024 * 1024

F32 = jnp.float32
BF16 = jnp.bfloat16


def _sigmoid(x):
    return 1.0 / (1.0 + jnp.exp(-x))


def _silu(x):
    return x * _sigmoid(x)


def _rms(x, w):
    return x * lax.rsqrt(jnp.mean(x * x, axis=-1, keepdims=True) + EPS) * w


def _modulate(x, w, shift, scale):
    return _rms(x, w) * (1.0 + scale) + shift


def _dot(a, b):
    return jnp.dot(a, b, preferred_element_type=F32)


def _dot_nt(a, b):
    return lax.dot_general(a, b, (((1,), (1,)), ((), ())), preferred_element_type=F32)


def _dot_tn(a, b):
    return lax.dot_general(a, b, (((0,), (0,)), ((), ())), preferred_element_type=F32)


def _params(n_grid):
    return pltpu.CompilerParams(
        dimension_semantics=("arbitrary",) * n_grid,
        vmem_limit_bytes=VMEM_LIMIT,
    )


def _const_spec(shape):
    nd = len(shape)
    return pl.BlockSpec(shape, lambda *_: (0,) * nd, pipeline_mode=pl.Buffered(1))


def _cast_riders(casts, steps, step_of):
    in_specs, args, out_specs, out_shape = [], [], [], []
    for w, layer in casts:
        _, rows, cols = w.shape
        slab = rows // steps
        in_specs.append(pl.BlockSpec((1, 1, slab, cols),
                                     lambda *g, layer=layer: (layer, step_of(*g), 0, 0)))
        args.append(w.reshape(w.shape[0], steps, slab, cols))
        out_specs.append(pl.BlockSpec((1, slab, cols), lambda *g: (step_of(*g), 0, 0)))
        out_shape.append(jax.ShapeDtypeStruct((steps, slab, cols), BF16))
    return in_specs, args, out_specs, out_shape


def _cast_slabs(cast_in, cast_out):
    for src, dst in zip(cast_in, cast_out):
        dst[0] = src[0, 0].astype(dst.dtype)


def _cast_results(outs, casts):
    return [o.reshape(w.shape[1], w.shape[2]) for o, (w, _) in zip(outs, casts)]


def _ada_kernel(c_ref, wa_ref, wb_ref, b_ref, o_ref, s_scr):
    k = pl.program_id(1)
    rows, n_out = wa_ref.shape[1], wa_ref.shape[2]

    @pl.when((pl.program_id(0) == 0) & (k == 0))
    def _():
        s_scr[...] = _silu(c_ref[...])

    @pl.when(k == 0)
    def _():
        o_ref[0] = jnp.broadcast_to(b_ref[0], o_ref.shape[1:])

    for half, w_ref in enumerate((wa_ref, wb_ref)):
        r0 = pl.multiple_of((2 * k + half) * rows, rows)
        for r in range(c_ref.shape[0]):
            s = s_scr[r, pl.ds(r0, rows), :]
            parts = [
                jnp.sum(w_ref[0, :, j * LANES:(j + 1) * LANES] * s, axis=0, keepdims=True)
                for j in range(n_out // LANES)
            ]
            o_ref[0, r:r + 1, :] += jnp.concatenate(parts, axis=1)


def _ada(cvecs, ada_w, ada_b):
    depth, d_in, n_out = ada_w.shape
    cb = jnp.broadcast_to(cvecs[:, :, None], cvecs.shape + (LANES,))
    return pl.pallas_call(
        _ada_kernel,
        grid=(depth, d_in // ADA_ROWS),
        in_specs=[
            pl.BlockSpec(cb.shape, lambda l, k: (0, 0, 0)),
            pl.BlockSpec((1, ADA_ROWS // 2, n_out), lambda l, k: (l, 2 * k, 0)),
            pl.BlockSpec((1, ADA_ROWS // 2, n_out), lambda l, k: (l, 2 * k + 1, 0)),
            pl.BlockSpec((1, 1, n_out), lambda l, k: (l, 0, 0)),
        ],
        out_specs=pl.BlockSpec((1, 8, n_out), lambda l, k: (l, 0, 0)),
        out_shape=jax.ShapeDtypeStruct((depth, 8, n_out), F32),
        scratch_shapes=[pltpu.VMEM(cb.shape, F32)],
        compiler_params=_params(2),
        name="ada",
    )(cb, ada_w, ada_w, ada_b.reshape(depth, 1, n_out))


def _inproj0_kernel(x_ref, mod_ref, nw_ref, w_ref, lbp_ref, *refs, n_casts):
    cast_in, (q_ref, ff_ref, fb_ref, v_ref, g_ref) = refs[:n_casts], refs[n_casts:n_casts + 5]
    _cast_slabs(cast_in, refs[n_casts + 5:])
    x = _rows(x_ref)
    tb, _, tl, _ = q_ref.shape
    half = x.shape[0] // 2
    hs = [_modulate(x[s * half:(s + 1) * half], nw_ref[...], mod_ref[0, 0:1, :],
                    mod_ref[0, 1:2, :]).astype(BF16) for s in range(2)]

    n_lb = lbp_ref.shape[0] // 2

    def lower_bound(d):
        rows = [lbp_ref[d * n_lb + i:d * n_lb + i + 1, :] for i in range(n_lb)]
        top = functools.reduce(jnp.maximum, rows)
        e = [jnp.exp(r - top) for r in rows]
        return e[0] / functools.reduce(jnp.add, e)

    lb = [lower_bound(0), lower_bound(1)]

    def emit(j, ref, act):
        vals = [_dot(h, w_ref[:, j * D_MODEL:(j + 1) * D_MODEL]) for h in hs]
        for s, val in enumerate(vals):
            val = act(val)
            b, r0 = divmod(s * half, tl)
            for hh in range(N_HEADS):
                ref[b, hh, r0:r0 + half, :] = val[:, hh * D_HEAD:(hh + 1) * D_HEAD].astype(ref.dtype)

    emit(0, q_ref, _silu)
    emit(1, ff_ref, lambda p: lb[0] + (1.0 - lb[0]) * _sigmoid(p))
    emit(2, fb_ref, lambda p: lb[1] + (1.0 - lb[1]) * _sigmoid(p))
    emit(3, v_ref, lambda p: p)
    emit(4, g_ref, _silu)


def _inproj0(x, mod, nw, w_in, lbp, per_seq_mod, casts):
    bsz, n, _ = x.shape
    tb, tl = _tile(n)
    n_i = n // tl
    mod_map = (lambda b, i: (b, 0, 0)) if per_seq_mod else (lambda b, i: (0, 0, 0))
    head_spec = pl.BlockSpec((tb, N_HEADS, tl, D_HEAD), lambda b, i: (b, 0, i, 0))
    shp = (bsz, N_HEADS, n, D_HEAD)
    c_in, c_args, c_out, c_shape = _cast_riders(casts, (bsz // tb) * n_i, lambda b, i: b * n_i + i)
    outs = pl.pallas_call(
        functools.partial(_inproj0_kernel, n_casts=len(casts)),
        grid=(bsz // tb, n_i),
        in_specs=[
            pl.BlockSpec((tb, tl, D_MODEL), lambda b, i: (b, i, 0)),
            pl.BlockSpec((1, N_MOD, D_MODEL), mod_map),
            _const_spec((1, D_MODEL)),
            _const_spec(w_in.shape),
            _const_spec(lbp.shape),
        ] + c_in,
        out_specs=[head_spec] * 5 + c_out,
        out_shape=[
            jax.ShapeDtypeStruct(shp, BF16),
            jax.ShapeDtypeStruct(shp, F32),
            jax.ShapeDtypeStruct(shp, F32),
            jax.ShapeDtypeStruct(shp, BF16),
            jax.ShapeDtypeStruct(shp, BF16),
        ] + c_shape,
        compiler_params=_params(2),
        name="inproj0",
    )(x, mod, nw, w_in, lbp, *c_args)
    return outs[:5], _cast_results(outs[5:], casts)


def _cumsum_rows(tri, x):
    hi = x.astype(BF16)
    lo = (x - hi.astype(F32)).astype(BF16)
    s = _dot(tri, jnp.concatenate([hi, lo], axis=1))
    return s[:, :D_HEAD] + s[:, D_HEAD:]


def _gla_kernel(*refs, seq_len, heads, zero_init, emit_state, n_casts):
    q_ref, ff_ref, fb_ref, v_ref, g_ref = refs[:5]
    pos = 5
    s0_ref = None
    if not zero_init:
        s0_ref = refs[pos]
        pos += 1
    gnw_ref = refs[pos]
    cast_in = refs[pos + 1:pos + 1 + n_casts]
    pos += 1 + n_casts
    o_ref = refs[pos]
    pos += 1
    sf_ref = None
    if emit_state:
        sf_ref = refs[pos]
        pos += 1
    cast_out = refs[pos:pos + n_casts]

    _cast_slabs(cast_in, cast_out)

    c = SCAN_CHUNK
    n_chunks = seq_len // c
    row = lax.broadcasted_iota(jnp.int32, (c, c), 0)
    col = lax.broadcasted_iota(jnp.int32, (c, c), 1)
    tri_lo = (col <= row).astype(BF16)
    tri_up = (col >= row).astype(BF16)
    row2 = lax.broadcasted_iota(jnp.int32, (c, 2 * c), 0)
    col2 = lax.broadcasted_iota(jnp.int32, (c, 2 * c), 1)
    mask2 = ((col2 < c) & (col2 <= row2)) | ((col2 >= c) & (col2 - c >= row2))
    zeros = jnp.zeros((c, D_HEAD), BF16)
    gnw = gnw_ref[...]
    directions = ((tri_lo, c // 2 - 1, c - 1), (tri_up, c // 2, 0))

    lanes = [dict(h=h, r0=n * c) for h in range(heads) for n in range(n_chunks)]
    states = []
    for h in range(heads):
        for d in range(2):
            if zero_init:
                states.append(jnp.zeros((D_HEAD, D_HEAD), F32))
            else:
                states.append(s0_ref[0, 0, d, h].T)

    def stage_a(ln):
        rows = pl.ds(ln["r0"], c)
        ln["f"] = [ff_ref[0, ln["h"], rows, :], fb_ref[0, ln["h"], rows, :]]
        ln["cum"] = [_cumsum_rows(tri, jnp.log2(f))
                     for (tri, _, _), f in zip(directions, ln["f"])]

    def stage_b(ln):
        rows = pl.ds(ln["r0"], c)
        q = q_ref[0, ln["h"], rows, :].astype(F32)
        ln["v"] = v_ref[0, ln["h"], rows, :]
        q_mid, k_mid, q_dec, k_state, ln["decay"] = [], [], [], [], []
        for d, (_, mid, last) in enumerate(directions):
            cum = ln["cum"][d]
            cum_mid = cum[mid:mid + 1]
            cum_last = cum[last:last + 1]
            qm = q * jnp.exp2(cum - cum_mid)
            km = (1.0 - ln["f"][d]) * jnp.exp2(cum_mid - cum)
            q_mid.append(qm.astype(BF16))
            k_mid.append(km.astype(BF16))
            q_dec.append((qm * jnp.exp2(cum_mid)).astype(BF16))
            k_state.append((km * jnp.exp2(cum_last - cum_mid)).astype(BF16))
            ln["decay"].append(jnp.exp2(cum_last))
        k_pair = jnp.concatenate([jnp.concatenate([k_mid[0], zeros], axis=1),
                                  jnp.concatenate([zeros, k_mid[1]], axis=1)], axis=0)
        ln["scores"] = _dot_nt(jnp.concatenate(q_mid, axis=1), k_pair)
        ln["kv"] = _dot_tn(ln["v"], jnp.concatenate(k_state, axis=1))
        ln["q_dec"] = jnp.concatenate(q_dec, axis=1)

    def stage_c(h):
        mine = [ln for ln in lanes if ln["h"] == h]
        for d, order in ((0, mine), (1, mine[::-1])):
            st = states[2 * h + d]
            for ln in order:
                ln.setdefault("st", [None, None])[d] = st.astype(BF16).T
                st = st * ln["decay"][d] + ln["kv"][:, d * D_HEAD:(d + 1) * D_HEAD]
            states[2 * h + d] = st

    def stage_d(ln):
        rows = pl.ds(ln["r0"], c)
        p = jnp.where(mask2, ln["scores"], 0.0).astype(BF16)
        lhs = jnp.concatenate([ln["q_dec"], p], axis=1)
        rhs = jnp.concatenate([ln["st"][0], ln["st"][1], ln["v"], ln["v"]], axis=0)
        tot = _dot(lhs, rhs)
        gate = g_ref[0, ln["h"], rows, :].astype(F32)
        o_ref[0, ln["h"], rows, :] = (_rms(tot, gnw) * gate).astype(o_ref.dtype)

    for g0 in range(0, len(lanes), SCAN_GROUP):
        for ln in lanes[g0:g0 + SCAN_GROUP]:
            stage_a(ln)
        for ln in lanes[g0:g0 + SCAN_GROUP]:
            stage_b(ln)
    for h in range(heads):
        stage_c(h)
    for ln in lanes:
        stage_d(ln)
    if emit_state:
        for h in range(heads):
            sf_ref[0, 0, 0, h] = states[2 * h].T
            sf_ref[0, 0, 1, h] = states[2 * h + 1].T


def _gla(q, ff, fb, v, g, s0, gnw, emit_state, heads, casts):
    bsz, _, n, _ = q.shape
    n_hb = N_HEADS // heads
    steps = bsz * n_hb
    zero_init = s0 is None
    seq_spec = pl.BlockSpec((1, heads, n, D_HEAD), lambda b, h: (b, h, 0, 0))
    state_spec = pl.BlockSpec((1, 1, 2, heads, D_HEAD, D_HEAD), lambda b, h: (b, 0, 0, h, 0, 0))
    in_specs = [seq_spec] * 5
    args = [q, ff, fb, v, g]
    if not zero_init:
        in_specs.append(state_spec)
        args.append(s0)
    in_specs.append(_const_spec((1, D_HEAD)))
    args.append(gnw)
    out_specs = [seq_spec]
    out_shape = [jax.ShapeDtypeStruct(q.shape, BF16)]
    if emit_state:
        out_specs.append(state_spec)
        out_shape.append(jax.ShapeDtypeStruct((bsz, 1, 2, N_HEADS, D_HEAD, D_HEAD), F32))
    c_in, c_args, c_out, c_shape = _cast_riders(casts, steps, lambda b, h: b * n_hb + h)
    in_specs += c_in
    args += c_args
    out_specs += c_out
    out_shape += c_shape
    outs = pl.pallas_call(
        functools.partial(_gla_kernel, seq_len=n, heads=heads, zero_init=zero_init,
                          emit_state=emit_state, n_casts=len(casts)),
        grid=(bsz, n_hb),
        in_specs=in_specs,
        out_specs=out_specs,
        out_shape=out_shape,
        compiler_params=_params(2),
        name="gla",
    )(*args)
    n_main = 2 if emit_state else 1
    return outs[0], (outs[1] if emit_state else None), _cast_results(outs[n_main:], casts)


def _conv_ffn(x1s, mod_ref, nfw_ref, wup_ref, cw_ref, cb_ref, wdn_ref, hid_scr, period):
    t = x1s[0].shape[0]
    hs = [_modulate(x1, nfw_ref[...], mod_ref[0, 3:4, :], mod_ref[0, 4:5, :]).astype(BF16)
          for x1 in x1s]
    pos = lax.broadcasted_iota(jnp.int32, (t, 1), 0) % period
    first = pos == 0
    last = pos == period - 1
    bounds = list(range(0, D_FF, FF_CHUNK)) + [D_FF]
    for lo, hi in zip(bounds[:-1], bounds[1:]):
        for s, h in enumerate(hs):
            a = _dot(h, wup_ref[:, lo:hi])
            g = _dot(h, wup_ref[:, D_FF + lo:D_FF + hi])
            prev = jnp.where(first, 0.0, pltpu.roll(a, 1, 0))
            nxt = jnp.where(last, 0.0, pltpu.roll(a, t - 1, 0))
            ac = (prev * cw_ref[0:1, lo:hi] + a * cw_ref[1:2, lo:hi]
                  + nxt * cw_ref[2:3, lo:hi] + cb_ref[:, lo:hi])
            hid_scr[s * t:(s + 1) * t, lo:hi] = (_silu(ac) * g).astype(BF16)
    return [x1 + mod_ref[0, 5:6, :] * _dot(hid_scr[s * t:(s + 1) * t, :], wdn_ref[...])
            for s, x1 in enumerate(x1s)]


def _tile(n):
    return (1, TOKEN_TILE) if n >= TOKEN_TILE else (TOKEN_TILE // n, n)


def _rows(ref):
    v = ref[...]
    return v.reshape(v.shape[0] * v.shape[1], v.shape[2])


def _halves(ref):
    tb, tl, _ = ref.shape
    half = tb * tl // 2
    out = []
    for s in range(2):
        bi, r0 = divmod(s * half, tl)
        out.append((bi, r0, ref[bi, r0:r0 + half, :]))
    return out


def _ffn_specs():
    return [
        _const_spec((1, D_MODEL)),
        _const_spec((D_MODEL, 2 * D_FF)),
        _const_spec((3, D_FF)),
        _const_spec((1, D_FF)),
        _const_spec((D_FF, D_MODEL)),
    ]


def _post0_kernel(x_ref, o_ref, mod_ref, wo_ref, nfw_ref, wup_ref, cw_ref, cb_ref, wdn_ref,
                  mod1_ref, nw1_ref, w1_ref, out_ref, gb_ref, cu_ref, hid_scr, *, period):
    x1s = []
    for bi, r0, x in _halves(x_ref):
        half = x.shape[0]
        o = jnp.concatenate([o_ref[bi, hh, r0:r0 + half, :] for hh in range(N_HEADS)], axis=1)
        x1s.append(x + mod_ref[0, 2:3, :] * _dot(o, wo_ref[...]))
    x2s = _conv_ffn(x1s, mod_ref, nfw_ref, wup_ref, cw_ref, cb_ref, wdn_ref, hid_scr, period)
    hs = [_modulate(x2, nw1_ref[...], mod1_ref[0, 0:1, :], mod1_ref[0, 1:2, :]).astype(BF16)
          for x2 in x2s]
    where = [(bi, r0) for bi, r0, _ in _halves(x_ref)]
    for (bi, r0), x2, h in zip(where, x2s, hs):
        half = x2.shape[0]
        out_ref[bi, r0:r0 + half, :] = x2
        gb_ref[bi, r0:r0 + half, :] = _dot(h, w1_ref[:, 0:D_MODEL]).astype(gb_ref.dtype)
        gate_c = _dot(h, w1_ref[:, D_MODEL:2 * D_MODEL])
        u = _dot(h, w1_ref[:, 2 * D_MODEL:3 * D_MODEL])
        cu_ref[bi, r0:r0 + half, :] = (gate_c * u).astype(cu_ref.dtype)


def _post0(x, o, mod, wo, nfw, wup, cw, cb, wdn, mod1, nw1, w1, per_seq_mod, period):
    bsz, n, _ = x.shape
    tb, tl = _tile(n)
    mod_map = (lambda b, i: (b, 0, 0)) if per_seq_mod else (lambda b, i: (0, 0, 0))
    tok_spec = pl.BlockSpec((tb, tl, D_MODEL), lambda b, i: (b, i, 0))
    return pl.pallas_call(
        functools.partial(_post0_kernel, period=period),
        grid=(bsz // tb, n // tl),
        in_specs=[
            tok_spec,
            pl.BlockSpec((tb, N_HEADS, tl, D_HEAD), lambda b, i: (b, 0, i, 0)),
            pl.BlockSpec((1, N_MOD, D_MODEL), mod_map),
            _const_spec((D_MODEL, D_MODEL)),
        ] + _ffn_specs() + [
            pl.BlockSpec((1, N_MOD, D_MODEL), mod_map),
            _const_spec((1, D_MODEL)),
            _const_spec(w1.shape),
        ],
        out_specs=[tok_spec] * 3,
        out_shape=[jax.ShapeDtypeStruct(x.shape, F32), jax.ShapeDtypeStruct(x.shape, BF16),
                   jax.ShapeDtypeStruct(x.shape, BF16)],
        scratch_shapes=[pltpu.VMEM((tb * tl, D_FF), BF16)],
        compiler_params=_params(2),
        name="post0",
    )(x, o, mod, wo, nfw, wup, cw, cb, wdn, mod1, nw1, w1)


def _post1_kernel(*refs, grid_rows, period):
    if grid_rows:
        (x_ref, gb_ref, cu_ref, cup_ref, cun_ref, mod_ref, scw_ref, scb_ref, wo_ref,
         nfw_ref, wup_ref, cw_ref, cb_ref, wdn_ref, fnw_ref, out_ref, hid_scr) = refs
    else:
        (x_ref, gb_ref, cu_ref, mod_ref, scw_ref, scb_ref, wo_ref,
         nfw_ref, wup_ref, cw_ref, cb_ref, wdn_ref, fnw_ref, out_ref, hid_scr) = refs
    cu = _rows(cu_ref).astype(F32)
    t = cu.shape[0]
    if grid_rows:
        i = pl.program_id(1)
        halo_p = jnp.where(i == 0, 0.0, cup_ref[0].astype(F32))
        halo_n = jnp.where(i == pl.num_programs(1) - 1, 0.0, cun_ref[0].astype(F32))
        prev = jnp.concatenate([halo_p, cu[:t - GRID_W]], axis=0)
        nxt = jnp.concatenate([cu[GRID_W:], halo_n], axis=0)
    else:
        pos = lax.broadcasted_iota(jnp.int32, (t, 1), 0) % period
        prev = jnp.where(pos == 0, 0.0, pltpu.roll(cu, 1, 0))
        nxt = jnp.where(pos == period - 1, 0.0, pltpu.roll(cu, t - 1, 0))
    z = prev * scw_ref[0:1, :] + cu * scw_ref[1:2, :] + nxt * scw_ref[2:3, :] + scb_ref[...]
    m = (_rows(gb_ref).astype(F32) * z).astype(BF16)
    halves = _halves(x_ref)
    half = t // 2
    x1s = [x + mod_ref[0, 2:3, :] * _dot(m[s * half:(s + 1) * half], wo_ref[...])
           for s, (_, _, x) in enumerate(halves)]
    x2s = _conv_ffn(x1s, mod_ref, nfw_ref, wup_ref, cw_ref, cb_ref, wdn_ref, hid_scr, period)
    for (bi, r0, _), x2 in zip(halves, x2s):
        out_ref[bi, r0:r0 + half, :] = _rms(x2, fnw_ref[...])


def _post1(x, gb, cu, mod, scw, scb, wo, nfw, wup, cw, cb, wdn, fnw, per_seq_mod, grid_rows, period):
    bsz, n, _ = x.shape
    tb, tl = _tile(n)
    mod_map = (lambda b, i: (b, 0, 0)) if per_seq_mod else (lambda b, i: (0, 0, 0))
    tok_spec = pl.BlockSpec((tb, tl, D_MODEL), lambda b, i: (b, i, 0))
    in_specs = [tok_spec, tok_spec, tok_spec]
    args = [x, gb, cu]
    if grid_rows:
        r = tl // GRID_W
        n_rows = n // GRID_W
        in_specs += [
            pl.BlockSpec((1, GRID_W, D_MODEL), lambda b, i: (b, jnp.maximum(i * r - 1, 0), 0)),
            pl.BlockSpec((1, GRID_W, D_MODEL), lambda b, i: (b, jnp.minimum((i + 1) * r, n_rows - 1), 0)),
        ]
        args += [cu, cu]
    in_specs += [
        pl.BlockSpec((1, N_MOD, D_MODEL), mod_map),
        _const_spec((3, D_MODEL)),
        _const_spec((1, D_MODEL)),
        _const_spec((D_MODEL, D_MODEL)),
    ] + _ffn_specs() + [_const_spec((1, D_MODEL))]
    args += [mod, scw, scb, wo, nfw, wup, cw, cb, wdn, fnw]
    return pl.pallas_call(
        functools.partial(_post1_kernel, grid_rows=grid_rows, period=period),
        grid=(bsz // tb, n // tl),
        in_specs=in_specs,
        out_specs=tok_spec,
        out_shape=jax.ShapeDtypeStruct(x.shape, F32),
        scratch_shapes=[pltpu.VMEM((tb * tl, D_FF), BF16)],
        compiler_params=_params(2),
        name="post1",
    )(*args)


def kernel(x_prompt, x_sample, state_hgrn, c, c_ctx, ada_w, ada_b, norm_mix_w, norm_ffn_w,
           hgrn_w_in, hgrn_lower_bounds, hgrn_gnorm_w, hgrn_w_out,
           sconv_w_in, sconv_conv_w, sconv_conv_b, sconv_w_out,
           ffn_w_up, ffn_conv_w, ffn_conv_b, ffn_w_down, final_norm_w):
    seq_p = x_prompt.shape[1]

    cvecs = jnp.concatenate([c_ctx[None, :], c], axis=0)
    mod = _ada(cvecs, ada_w, ada_b)
    mod = mod.reshape(mod.shape[0], 8, N_MOD, D_MODEL)
    mod_p = mod[:, 0:1]
    mod_s = mod[:, 1:1 + c.shape[0]]

    row = lambda a: a.reshape(1, -1)
    w_in0 = hgrn_w_in[0].astype(BF16)
    gnw = row(hgrn_gnorm_w[0])
    lbp = hgrn_lower_bounds.reshape(-1, hgrn_lower_bounds.shape[-1])

    def mixer0(x, mods, s0, per_seq_mod, emit_state, proj_casts, casts):
        (q, ff, fb, v, g), proj_cast = _inproj0(x, mods[0], row(norm_mix_w[0]), w_in0, lbp,
                                                per_seq_mod, proj_casts)
        heads = min(N_HEADS, max(1, SCAN_LANES // (x.shape[1] // SCAN_CHUNK)))
        o, s_fin, scan_cast = _gla(q, ff, fb, v, g, s0, gnw, emit_state, heads, casts)
        return o, s_fin, proj_cast + scan_cast

    def post0(x, o, mods, per_seq_mod, ffn_period):
        return _post0(x, o, mods[0], w_out0, row(norm_ffn_w[0]), w_up0, ffn_conv_w[0],
                      row(ffn_conv_b[0]), w_dn0, mods[1], row(norm_mix_w[1]), w_in1,
                      per_seq_mod, ffn_period)

    def post1(x, gb, cu, mods, per_seq_mod, grid_rows, ffn_period):
        return _post1(x, gb, cu, mods[1], sconv_conv_w[0], row(sconv_conv_b[0]), w_out1,
                      row(norm_ffn_w[1]), w_up1, ffn_conv_w[1], row(ffn_conv_b[1]), w_dn1,
                      row(final_norm_w), per_seq_mod, grid_rows, ffn_period)

    o_p, new_state, (w_up0, w_out0, w_dn0, w_in1) = mixer0(
        x_prompt, mod_p, None, False, True,
        [(ffn_w_up, 0)], [(hgrn_w_out, 0), (ffn_w_down, 0), (sconv_w_in, 0)])
    xp, gb_p, cu_p = post0(x_prompt, o_p, mod_p, False, seq_p)
    o_s, _, (w_out1, w_up1, w_dn1) = mixer0(
        x_sample, mod_s, state_hgrn, True, False,
        [], [(sconv_w_out, 0), (ffn_w_up, 1), (ffn_w_down, 1)])
    y_prompt = post1(xp, gb_p, cu_p, mod_p, False, False, seq_p)
    xs, gb_s, cu_s = post0(x_sample, o_s, mod_s, True, GRID_W)
    y_sample = post1(xs, gb_s, cu_s, mod_s, True, True, GRID_W)
    return (y_prompt, y_sample, new_state.astype(state_hgrn.dtype))
```

```python
import functools

import jax
import jax.numpy as jnp
from jax import lax
from jax.experimental import pallas as pl
from jax.experimental.pallas import tpu as pltpu

D_MODEL = 1024
N_HEADS = 8
D_HEAD = 128
D_FF = 2816
N_MOD = 6
GRID_W = 64
EPS = 1e-6

LANES = 128
SCAN_CHUNK = 64
SCAN_LANES = 64
SCAN_GROUP = 8
TOKEN_TILE = 512
FF_CHUNK = 512
ADA_ROWS = 256
VMEM_LIMIT = 56 * 1024 * 1024

F32 = jnp.float32
BF16 = jnp.bfloat16


def _sigmoid(x):
    return 1.0 / (1.0 + jnp.exp(-x))


def _silu(x):
    return x * _sigmoid(x)


def _rms(x, w):
    return x * lax.rsqrt(jnp.mean(x * x, axis=-1, keepdims=True) + EPS) * w


def _modulate(x, w, shift, scale):
    return _rms(x, w) * (1.0 + scale) + shift


def _dot(a, b):
    return jnp.dot(a, b, preferred_element_type=F32)


def _dot_nt(a, b):
    return lax.dot_general(a, b, (((1,), (1,)), ((), ())), preferred_element_type=F32)


def _dot_tn(a, b):
    return lax.dot_general(a, b, (((0,), (0,)), ((), ())), preferred_element_type=F32)


def _params(n_grid):
    return pltpu.CompilerParams(
        dimension_semantics=("arbitrary",) * n_grid,
        vmem_limit_bytes=VMEM_LIMIT,
    )


def _const_spec(shape):
    nd = len(shape)
    return pl.BlockSpec(shape, lambda *_: (0,) * nd, pipeline_mode=pl.Buffered(1))


def _cast_riders(casts, steps, step_of):
    in_specs, args, out_specs, out_shape = [], [], [], []
    for w, layer in casts:
        _, rows, cols = w.shape
        slab = rows // steps
        in_specs.append(pl.BlockSpec((1, 1, slab, cols),
                                     lambda *g, layer=layer: (layer, step_of(*g), 0, 0)))
        args.append(w.reshape(w.shape[0], steps, slab, cols))
        out_specs.append(pl.BlockSpec((1, slab, cols), lambda *g: (step_of(*g), 0, 0)))
        out_shape.append(jax.ShapeDtypeStruct((steps, slab, cols), BF16))
    return in_specs, args, out_specs, out_shape


def _cast_slabs(cast_in, cast_out):
    for src, dst in zip(cast_in, cast_out):
        dst[0] = src[0, 0].astype(dst.dtype)


def _cast_results(outs, casts):
    return [o.reshape(w.shape[1], w.shape[2]) for o, (w, _) in zip(outs, casts)]


def _ada_kernel(c_ref, wa_ref, wb_ref, b_ref, o_ref, s_scr):
    k = pl.program_id(1)
    rows, n_out = wa_ref.shape[1], wa_ref.shape[2]

    @pl.when((pl.program_id(0) == 0) & (k == 0))
    def _():
        s_scr[...] = _silu(c_ref[...])

    @pl.when(k == 0)
    def _():
        o_ref[0] = jnp.broadcast_to(b_ref[0], o_ref.shape[1:])

    for half, w_ref in enumerate((wa_ref, wb_ref)):
        r0 = pl.multiple_of((2 * k + half) * rows, rows)
        for r in range(c_ref.shape[0]):
            s = s_scr[r, pl.ds(r0, rows), :]
            parts = [
                jnp.sum(w_ref[0, :, j * LANES:(j + 1) * LANES] * s, axis=0, keepdims=True)
                for j in range(n_out // LANES)
            ]
            o_ref[0, r:r + 1, :] += jnp.concatenate(parts, axis=1)


def _ada(cvecs, ada_w, ada_b):
    depth, d_in, n_out = ada_w.shape
    cb = jnp.broadcast_to(cvecs[:, :, None], cvecs.shape + (LANES,))
    return pl.pallas_call(
        _ada_kernel,
        grid=(depth, d_in // ADA_ROWS),
        in_specs=[
            pl.BlockSpec(cb.shape, lambda l, k: (0, 0, 0)),
            pl.BlockSpec((1, ADA_ROWS // 2, n_out), lambda l, k: (l, 2 * k, 0)),
            pl.BlockSpec((1, ADA_ROWS // 2, n_out), lambda l, k: (l, 2 * k + 1, 0)),
            pl.BlockSpec((1, 1, n_out), lambda l, k: (l, 0, 0)),
        ],
        out_specs=pl.BlockSpec((1, 8, n_out), lambda l, k: (l, 0, 0)),
        out_shape=jax.ShapeDtypeStruct((depth, 8, n_out), F32),
        scratch_shapes=[pltpu.VMEM(cb.shape, F32)],
        compiler_params=_params(2),
        name="ada",
    )(cb, ada_w, ada_w, ada_b.reshape(depth, 1, n_out))


def _inproj0_kernel(x_ref, mod_ref, nw_ref, w_ref, lbp_ref, *refs, n_casts):
    cast_in, (q_ref, ff_ref, fb_ref, v_ref, g_ref) = refs[:n_casts], refs[n_casts:n_casts + 5]
    _cast_slabs(cast_in, refs[n_casts + 5:])
    x = _rows(x_ref)
    tb, _, tl, _ = q_ref.shape
    half = x.shape[0] // 2
    hs = [_modulate(x[s * half:(s + 1) * half], nw_ref[...], mod_ref[0, 0:1, :],
                    mod_ref[0, 1:2, :]).astype(BF16) for s in range(2)]

    n_lb = lbp_ref.shape[0] // 2

    def lower_bound(d):
        rows = [lbp_ref[d * n_lb + i:d * n_lb + i + 1, :] for i in range(n_lb)]
        top = functools.reduce(jnp.maximum, rows)
        e = [jnp.exp(r - top) for r in rows]
        return e[0] / functools.reduce(jnp.add, e)

    lb = [lower_bound(0), lower_bound(1)]

    def emit(j, ref, act):
        w = w_ref[:, j * D_MODEL:(j + 1) * D_MODEL].astype(BF16)
        vals = [_dot(h, w) for h in hs]
        for s, val in enumerate(vals):
            val = act(val)
            b, r0 = divmod(s * half, tl)
            for hh in range(N_HEADS):
                ref[b, hh, r0:r0 + half, :] = val[:, hh * D_HEAD:(hh + 1) * D_HEAD].astype(ref.dtype)

    emit(0, q_ref, _silu)
    emit(1, ff_ref, lambda p: lb[0] + (1.0 - lb[0]) * _sigmoid(p))
    emit(2, fb_ref, lambda p: lb[1] + (1.0 - lb[1]) * _sigmoid(p))
    emit(3, v_ref, lambda p: p)
    emit(4, g_ref, _silu)


def _inproj0(x, mod, nw, w_in, lbp, per_seq_mod, casts):
    bsz, n, _ = x.shape
    tb, tl = _tile(n)
    n_i = n // tl
    mod_map = (lambda b, i: (b, 0, 0)) if per_seq_mod else (lambda b, i: (0, 0, 0))
    head_spec = pl.BlockSpec((tb, N_HEADS, tl, D_HEAD), lambda b, i: (b, 0, i, 0))
    shp = (bsz, N_HEADS, n, D_HEAD)
    c_in, c_args, c_out, c_shape = _cast_riders(casts, (bsz // tb) * n_i, lambda b, i: b * n_i + i)
    outs = pl.pallas_call(
        functools.partial(_inproj0_kernel, n_casts=len(casts)),
        grid=(bsz // tb, n_i),
        in_specs=[
            pl.BlockSpec((tb, tl, D_MODEL), lambda b, i: (b, i, 0)),
            pl.BlockSpec((1, N_MOD, D_MODEL), mod_map),
            _const_spec((1, D_MODEL)),
            _const_spec(w_in.shape),
            _const_spec(lbp.shape),
        ] + c_in,
        out_specs=[head_spec] * 5 + c_out,
        out_shape=[
            jax.ShapeDtypeStruct(shp, BF16),
            jax.ShapeDtypeStruct(shp, F32),
            jax.ShapeDtypeStruct(shp, F32),
            jax.ShapeDtypeStruct(shp, BF16),
            jax.ShapeDtypeStruct(shp, BF16),
        ] + c_shape,
        compiler_params=_params(2),
        name="inproj0",
    )(x, mod, nw, w_in, lbp, *c_args)
    return outs[:5], _cast_results(outs[5:], casts)


def _cumsum_rows(tri, x):
    hi = x.astype(BF16)
    lo = (x - hi.astype(F32)).astype(BF16)
    s = _dot(tri, jnp.concatenate([hi, lo], axis=1))
    return s[:, :D_HEAD] + s[:, D_HEAD:]


def _gla_kernel(*refs, seq_len, heads, zero_init, emit_state, n_casts):
    q_ref, ff_ref, fb_ref, v_ref, g_ref = refs[:5]
    pos = 5
    s0_ref = None
    if not zero_init:
        s0_ref = refs[pos]
        pos += 1
    gnw_ref = refs[pos]
    cast_in = refs[pos + 1:pos + 1 + n_casts]
    pos += 1 + n_casts
    o_ref = refs[pos]
    pos += 1
    sf_ref = None
    if emit_state:
        sf_ref = refs[pos]
        pos += 1
    cast_out = refs[pos:pos + n_casts]

    _cast_slabs(cast_in, cast_out)

    c = SCAN_CHUNK
    n_chunks = seq_len // c
    row = lax.broadcasted_iota(jnp.int32, (c, c), 0)
    col = lax.broadcasted_iota(jnp.int32, (c, c), 1)
    tri_lo = (col <= row).astype(BF16)
    tri_up = (col >= row).astype(BF16)
    row2 = lax.broadcasted_iota(jnp.int32, (c, 2 * c), 0)
    col2 = lax.broadcasted_iota(jnp.int32, (c, 2 * c), 1)
    mask2 = ((col2 < c) & (col2 <= row2)) | ((col2 >= c) & (col2 - c >= row2))
    zeros = jnp.zeros((c, D_HEAD), BF16)
    gnw = gnw_ref[...]
    directions = ((tri_lo, c // 2 - 1, c - 1), (tri_up, c // 2, 0))

    lanes = [dict(h=h, r0=n * c) for h in range(heads) for n in range(n_chunks)]
    states = []
    for h in range(heads):
        for d in range(2):
            if zero_init:
                states.append(jnp.zeros((D_HEAD, D_HEAD), F32))
            else:
                states.append(s0_ref[0, 0, d, h].T)

    def stage_a(ln):
        rows = pl.ds(ln["r0"], c)
        ln["f"] = [ff_ref[0, ln["h"], rows, :], fb_ref[0, ln["h"], rows, :]]
        ln["cum"] = [_cumsum_rows(tri, jnp.log2(f))
                     for (tri, _, _), f in zip(directions, ln["f"])]

    def stage_b(ln):
        rows = pl.ds(ln["r0"], c)
        q = q_ref[0, ln["h"], rows, :].astype(F32)
        ln["v"] = v_ref[0, ln["h"], rows, :]
        q_mid, k_mid, q_dec, k_state, ln["decay"] = [], [], [], [], []
        for d, (_, mid, last) in enumerate(directions):
            cum = ln["cum"][d]
            cum_mid = cum[mid:mid + 1]
            cum_last = cum[last:last + 1]
            qm = q * jnp.exp2(cum - cum_mid)
            km = (1.0 - ln["f"][d]) * jnp.exp2(cum_mid - cum)
            q_mid.append(qm.astype(BF16))
            k_mid.append(km.astype(BF16))
            q_dec.append((qm * jnp.exp2(cum_mid)).astype(BF16))
            k_state.append((km * jnp.exp2(cum_last - cum_mid)).astype(BF16))
            ln["decay"].append(jnp.exp2(cum_last))
        k_pair = jnp.concatenate([jnp.concatenate([k_mid[0], zeros], axis=1),
                                  jnp.concatenate([zeros, k_mid[1]], axis=1)], axis=0)
        ln["scores"] = _dot_nt(jnp.concatenate(q_mid, axis=1), k_pair)
        ln["kv"] = _dot_tn(ln["v"], jnp.concatenate(k_state, axis=1))
        ln["q_dec"] = jnp.concatenate(q_dec, axis=1)

    def stage_c(h):
        mine = [ln for ln in lanes if ln["h"] == h]
        for d, order in ((0, mine), (1, mine[::-1])):
            st = states[2 * h + d]
            for ln in order:
                ln.setdefault("st", [None, None])[d] = st.astype(BF16).T
                st = st * ln["decay"][d] + ln["kv"][:, d * D_HEAD:(d + 1) * D_HEAD]
            states[2 * h + d] = st

    def stage_d(ln):
        rows = pl.ds(ln["r0"], c)
        p = jnp.where(mask2, ln["scores"], 0.0).astype(BF16)
        lhs = jnp.concatenate([ln["q_dec"], p], axis=1)
        rhs = jnp.concatenate([ln["st"][0], ln["st"][1], ln["v"], ln["v"]], axis=0)
        tot = _dot(lhs, rhs)
        gate = g_ref[0, ln["h"], rows, :].astype(F32)
        o_ref[0, ln["h"], rows, :] = (_rms(tot, gnw) * gate).astype(o_ref.dtype)

    for g0 in range(0, len(lanes), SCAN_GROUP):
        for ln in lanes[g0:g0 + SCAN_GROUP]:
            stage_a(ln)
        for ln in lanes[g0:g0 + SCAN_GROUP]:
            stage_b(ln)
    for h in range(heads):
        stage_c(h)
    for ln in lanes:
        stage_d(ln)
    if emit_state:
        for h in range(heads):
            sf_ref[0, 0, 0, h] = states[2 * h].T
            sf_ref[0, 0, 1, h] = states[2 * h + 1].T


def _gla(q, ff, fb, v, g, s0, gnw, emit_state, heads, casts):
    bsz, _, n, _ = q.shape
    n_hb = N_HEADS // heads
    steps = bsz * n_hb
    zero_init = s0 is None
    seq_spec = pl.BlockSpec((1, heads, n, D_HEAD), lambda b, h: (b, h, 0, 0))
    state_spec = pl.BlockSpec((1, 1, 2, heads, D_HEAD, D_HEAD), lambda b, h: (b, 0, 0, h, 0, 0))
    in_specs = [seq_spec] * 5
    args = [q, ff, fb, v, g]
    if not zero_init:
        in_specs.append(state_spec)
        args.append(s0)
    in_specs.append(_const_spec((1, D_HEAD)))
    args.append(gnw)
    out_specs = [seq_spec]
    out_shape = [jax.ShapeDtypeStruct(q.shape, BF16)]
    if emit_state:
        out_specs.append(state_spec)
        out_shape.append(jax.ShapeDtypeStruct((bsz, 1, 2, N_HEADS, D_HEAD, D_HEAD), F32))
    c_in, c_args, c_out, c_shape = _cast_riders(casts, steps, lambda b, h: b * n_hb + h)
    in_specs += c_in
    args += c_args
    out_specs += c_out
    out_shape += c_shape
    outs = pl.pallas_call(
        functools.partial(_gla_kernel, seq_len=n, heads=heads, zero_init=zero_init,
                          emit_state=emit_state, n_casts=len(casts)),
        grid=(bsz, n_hb),
        in_specs=in_specs,
        out_specs=out_specs,
        out_shape=out_shape,
        compiler_params=_params(2),
        name="gla",
    )(*args)
    n_main = 2 if emit_state else 1
    return outs[0], (outs[1] if emit_state else None), _cast_results(outs[n_main:], casts)


def _conv_ffn(x1s, mod_ref, nfw_ref, wup_ref, cw_ref, cb_ref, wdn_ref, hid_scr, period):
    t = x1s[0].shape[0]
    hs = [_modulate(x1, nfw_ref[...], mod_ref[0, 3:4, :], mod_ref[0, 4:5, :]).astype(BF16)
          for x1 in x1s]
    pos = lax.broadcasted_iota(jnp.int32, (t, 1), 0) % period
    first = pos == 0
    last = pos == period - 1
    bounds = list(range(0, D_FF, FF_CHUNK)) + [D_FF]
    for lo, hi in zip(bounds[:-1], bounds[1:]):
        for s, h in enumerate(hs):
            a = _dot(h, wup_ref[:, lo:hi])
            g = _dot(h, wup_ref[:, D_FF + lo:D_FF + hi])
            prev = jnp.where(first, 0.0, pltpu.roll(a, 1, 0))
            nxt = jnp.where(last, 0.0, pltpu.roll(a, t - 1, 0))
            ac = (prev * cw_ref[0:1, lo:hi] + a * cw_ref[1:2, lo:hi]
                  + nxt * cw_ref[2:3, lo:hi] + cb_ref[:, lo:hi])
            hid_scr[s * t:(s + 1) * t, lo:hi] = (_silu(ac) * g).astype(BF16)
    return [x1 + mod_ref[0, 5:6, :] * _dot(hid_scr[s * t:(s + 1) * t, :], wdn_ref[...])
            for s, x1 in enumerate(x1s)]


def _tile(n):
    return (1, TOKEN_TILE) if n >= TOKEN_TILE else (TOKEN_TILE // n, n)


def _rows(ref):
    v = ref[...]
    return v.reshape(v.shape[0] * v.shape[1], v.shape[2])


def _halves(ref):
    tb, tl, _ = ref.shape
    half = tb * tl // 2
    out = []
    for s in range(2):
        bi, r0 = divmod(s * half, tl)
        out.append((bi, r0, ref[bi, r0:r0 + half, :]))
    return out


def _ffn_specs():
    return [
        _const_spec((1, D_MODEL)),
        _const_spec((D_MODEL, 2 * D_FF)),
        _const_spec((3, D_FF)),
        _const_spec((1, D_FF)),
        _const_spec((D_FF, D_MODEL)),
    ]


def _post0_kernel(x_ref, o_ref, mod_ref, wo_ref, nfw_ref, wup_ref, cw_ref, cb_ref, wdn_ref,
                  mod1_ref, nw1_ref, w1_ref, out_ref, gb_ref, cu_ref, hid_scr, *, period):
    x1s = []
    for bi, r0, x in _halves(x_ref):
        half = x.shape[0]
        o = jnp.concatenate([o_ref[bi, hh, r0:r0 + half, :] for hh in range(N_HEADS)], axis=1)
        x1s.append(x + mod_ref[0, 2:3, :] * _dot(o, wo_ref[...]))
    x2s = _conv_ffn(x1s, mod_ref, nfw_ref, wup_ref, cw_ref, cb_ref, wdn_ref, hid_scr, period)
    hs = [_modulate(x2, nw1_ref[...], mod1_ref[0, 0:1, :], mod1_ref[0, 1:2, :]).astype(BF16)
          for x2 in x2s]
    where = [(bi, r0) for bi, r0, _ in _halves(x_ref)]
    for (bi, r0), x2, h in zip(where, x2s, hs):
        half = x2.shape[0]
        out_ref[bi, r0:r0 + half, :] = x2
        gb_ref[bi, r0:r0 + half, :] = _dot(h, w1_ref[:, 0:D_MODEL]).astype(gb_ref.dtype)
        gate_c = _dot(h, w1_ref[:, D_MODEL:2 * D_MODEL])
        u = _dot(h, w1_ref[:, 2 * D_MODEL:3 * D_MODEL])
        cu_ref[bi, r0:r0 + half, :] = (gate_c * u).astype(cu_ref.dtype)


def _post0(x, o, mod, wo, nfw, wup, cw, cb, wdn, mod1, nw1, w1, per_seq_mod, period):
    bsz, n, _ = x.shape
    tb, tl = _tile(n)
    mod_map = (lambda b, i: (b, 0, 0)) if per_seq_mod else (lambda b, i: (0, 0, 0))
    tok_spec = pl.BlockSpec((tb, tl, D_MODEL), lambda b, i: (b, i, 0))
    return pl.pallas_call(
        functools.partial(_post0_kernel, period=period),
        grid=(bsz // tb, n // tl),
        in_specs=[
            tok_spec,
            pl.BlockSpec((tb, N_HEADS, tl, D_HEAD), lambda b, i: (b, 0, i, 0)),
            pl.BlockSpec((1, N_MOD, D_MODEL), mod_map),
            _const_spec((D_MODEL, D_MODEL)),
        ] + _ffn_specs() + [
            pl.BlockSpec((1, N_MOD, D_MODEL), mod_map),
            _const_spec((1, D_MODEL)),
            _const_spec(w1.shape),
        ],
        out_specs=[tok_spec] * 3,
        out_shape=[jax.ShapeDtypeStruct(x.shape, F32), jax.ShapeDtypeStruct(x.shape, BF16),
                   jax.ShapeDtypeStruct(x.shape, BF16)],
        scratch_shapes=[pltpu.VMEM((tb * tl, D_FF), BF16)],
        compiler_params=_params(2),
        name="post0",
    )(x, o, mod, wo, nfw, wup, cw, cb, wdn, mod1, nw1, w1)


def _post1_kernel(*refs, grid_rows, period):
    if grid_rows:
        (x_ref, gb_ref, cu_ref, cup_ref, cun_ref, mod_ref, scw_ref, scb_ref, wo_ref,
         nfw_ref, wup_ref, cw_ref, cb_ref, wdn_ref, fnw_ref, out_ref, hid_scr) = refs
    else:
        (x_ref, gb_ref, cu_ref, mod_ref, scw_ref, scb_ref, wo_ref,
         nfw_ref, wup_ref, cw_ref, cb_ref, wdn_ref, fnw_ref, out_ref, hid_scr) = refs
    cu = _rows(cu_ref).astype(F32)
    t = cu.shape[0]
    if grid_rows:
        i = pl.program_id(1)
        halo_p = jnp.where(i == 0, 0.0, cup_ref[0].astype(F32))
        halo_n = jnp.where(i == pl.num_programs(1) - 1, 0.0, cun_ref[0].astype(F32))
        prev = jnp.concatenate([halo_p, cu[:t - GRID_W]], axis=0)
        nxt = jnp.concatenate([cu[GRID_W:], halo_n], axis=0)
    else:
        pos = lax.broadcasted_iota(jnp.int32, (t, 1), 0) % period
        prev = jnp.where(pos == 0, 0.0, pltpu.roll(cu, 1, 0))
        nxt = jnp.where(pos == period - 1, 0.0, pltpu.roll(cu, t - 1, 0))
    z = prev * scw_ref[0:1, :] + cu * scw_ref[1:2, :] + nxt * scw_ref[2:3, :] + scb_ref[...]
    m = (_rows(gb_ref).astype(F32) * z).astype(BF16)
    halves = _halves(x_ref)
    half = t // 2
    x1s = [x + mod_ref[0, 2:3, :] * _dot(m[s * half:(s + 1) * half], wo_ref[...])
           for s, (_, _, x) in enumerate(halves)]
    x2s = _conv_ffn(x1s, mod_ref, nfw_ref, wup_ref, cw_ref, cb_ref, wdn_ref, hid_scr, period)
    for (bi, r0, _), x2 in zip(halves, x2s):
        out_ref[bi, r0:r0 + half, :] = _rms(x2, fnw_ref[...])


def _post1(x, gb, cu, mod, scw, scb, wo, nfw, wup, cw, cb, wdn, fnw, per_seq_mod, grid_rows, period):
    bsz, n, _ = x.shape
    tb, tl = _tile(n)
    mod_map = (lambda b, i: (b, 0, 0)) if per_seq_mod else (lambda b, i: (0, 0, 0))
    tok_spec = pl.BlockSpec((tb, tl, D_MODEL), lambda b, i: (b, i, 0))
    in_specs = [tok_spec, tok_spec, tok_spec]
    args = [x, gb, cu]
    if grid_rows:
        r = tl // GRID_W
        n_rows = n // GRID_W
        in_specs += [
            pl.BlockSpec((1, GRID_W, D_MODEL), lambda b, i: (b, jnp.maximum(i * r - 1, 0), 0)),
            pl.BlockSpec((1, GRID_W, D_MODEL), lambda b, i: (b, jnp.minimum((i + 1) * r, n_rows - 1), 0)),
        ]
        args += [cu, cu]
    in_specs += [
        pl.BlockSpec((1, N_MOD, D_MODEL), mod_map),
        _const_spec((3, D_MODEL)),
        _const_spec((1, D_MODEL)),
        _const_spec((D_MODEL, D_MODEL)),
    ] + _ffn_specs() + [_const_spec((1, D_MODEL))]
    args += [mod, scw, scb, wo, nfw, wup, cw, cb, wdn, fnw]
    return pl.pallas_call(
        functools.partial(_post1_kernel, grid_rows=grid_rows, period=period),
        grid=(bsz // tb, n // tl),
        in_specs=in_specs,
        out_specs=tok_spec,
        out_shape=jax.ShapeDtypeStruct(x.shape, F32),
        scratch_shapes=[pltpu.VMEM((tb * tl, D_FF), BF16)],
        compiler_params=_params(2),
        name="post1",
    )(*args)


def kernel(x_prompt, x_sample, state_hgrn, c, c_ctx, ada_w, ada_b, norm_mix_w, norm_ffn_w,
           hgrn_w_in, hgrn_lower_bounds, hgrn_gnorm_w, hgrn_w_out,
           sconv_w_in, sconv_conv_w, sconv_conv_b, sconv_w_out,
           ffn_w_up, ffn_conv_w, ffn_conv_b, ffn_w_down, final_norm_w):
    seq_p = x_prompt.shape[1]

    cvecs = jnp.concatenate([c_ctx[None, :], c], axis=0)
    mod = _ada(cvecs, ada_w, ada_b)
    mod = mod.reshape(mod.shape[0], 8, N_MOD, D_MODEL)
    mod_p = mod[:, 0:1]
    mod_s = mod[:, 1:1 + c.shape[0]]

    row = lambda a: a.reshape(1, -1)
    gnw = row(hgrn_gnorm_w[0])
    lbp = hgrn_lower_bounds.reshape(-1, hgrn_lower_bounds.shape[-1])

    def mixer0(x, mods, w_in0, s0, per_seq_mod, emit_state, proj_casts, casts):
        (q, ff, fb, v, g), proj_cast = _inproj0(x, mods[0], row(norm_mix_w[0]), w_in0, lbp,
                                                per_seq_mod, proj_casts)
        heads = min(N_HEADS, max(1, SCAN_LANES // (x.shape[1] // SCAN_CHUNK)))
        o, s_fin, scan_cast = _gla(q, ff, fb, v, g, s0, gnw, emit_state, heads, casts)
        return o, s_fin, proj_cast + scan_cast

    def post0(x, o, mods, per_seq_mod, ffn_period):
        return _post0(x, o, mods[0], w_out0, row(norm_ffn_w[0]), w_up0, ffn_conv_w[0],
                      row(ffn_conv_b[0]), w_dn0, mods[1], row(norm_mix_w[1]), w_in1,
                      per_seq_mod, ffn_period)

    def post1(x, gb, cu, mods, per_seq_mod, grid_rows, ffn_period):
        return _post1(x, gb, cu, mods[1], sconv_conv_w[0], row(sconv_conv_b[0]), w_out1,
                      row(norm_ffn_w[1]), w_up1, ffn_conv_w[1], row(ffn_conv_b[1]), w_dn1,
                      row(final_norm_w), per_seq_mod, grid_rows, ffn_period)

    o_p, new_state, (w_in0, w_up0, w_out0, w_dn0, w_in1) = mixer0(
        x_prompt, mod_p, hgrn_w_in[0], None, False, True,
        [(hgrn_w_in, 0), (ffn_w_up, 0)], [(hgrn_w_out, 0), (ffn_w_down, 0), (sconv_w_in, 0)])
    xp, gb_p, cu_p = post0(x_prompt, o_p, mod_p, False, seq_p)
    o_s, _, (w_out1, w_up1, w_dn1) = mixer0(
        x_sample, mod_s, w_in0, state_hgrn, True, False,
        [], [(sconv_w_out, 0), (ffn_w_up, 1), (ffn_w_down, 1)])
    y_prompt = post1(xp, gb_p, cu_p, mod_p, False, False, seq_p)
    xs, gb_s, cu_s = post0(x_sample, o_s, mod_s, True, GRID_W)
    y_sample = post1(xs, gb_s, cu_s, mod_s, True, True, GRID_W)
    return (y_prompt, y_sample, new_state.astype(state_hgrn.dtype))
```

```python
import functools

import jax
import jax.numpy as jnp
from jax import lax
from jax.experimental import pallas as pl
from jax.experimental.pallas import tpu as pltpu

D_MODEL = 1024
N_HEADS = 8
D_HEAD = 128
D_FF = 2816
N_MOD = 6
GRID_W = 64
EPS = 1e-6

LANES = 128
SCAN_CHUNK = 64
SCAN_LANES = 64
SCAN_GROUP = 8
TOKEN_TILE = 512
FF_CHUNK = 1024
ADA_ROWS = 256
VMEM_LIMIT = 56 * 1024 * 1024

F32 = jnp.float32
BF16 = jnp.bfloat16


def _sigmoid(x):
    return 1.0 / (1.0 + jnp.exp(-x))


def _silu(x):
    return x * _sigmoid(x)


def _rms(x, w):
    return x * lax.rsqrt(jnp.mean(x * x, axis=-1, keepdims=True) + EPS) * w


def _modulate(x, w, shift, scale):
    return _rms(x, w) * (1.0 + scale) + shift


def _dot(a, b):
    return jnp.dot(a, b, preferred_element_type=F32)


def _dot_nt(a, b):
    return lax.dot_general(a, b, (((1,), (1,)), ((), ())), preferred_element_type=F32)


def _dot_tn(a, b):
    return lax.dot_general(a, b, (((0,), (0,)), ((), ())), preferred_element_type=F32)


def _params(n_grid):
    return pltpu.CompilerParams(
        dimension_semantics=("arbitrary",) * n_grid,
        vmem_limit_bytes=VMEM_LIMIT,
    )


def _const_spec(shape):
    nd = len(shape)
    return pl.BlockSpec(shape, lambda *_: (0,) * nd, pipeline_mode=pl.Buffered(1))


def _cast_riders(casts, steps, step_of):
    in_specs, args, out_specs, out_shape = [], [], [], []
    for w, layer in casts:
        _, rows, cols = w.shape
        slab = rows // steps
        in_specs.append(pl.BlockSpec((1, 1, slab, cols),
                                     lambda *g, layer=layer: (layer, step_of(*g), 0, 0)))
        args.append(w.reshape(w.shape[0], steps, slab, cols))
        out_specs.append(pl.BlockSpec((1, slab, cols), lambda *g: (step_of(*g), 0, 0)))
        out_shape.append(jax.ShapeDtypeStruct((steps, slab, cols), BF16))
    return in_specs, args, out_specs, out_shape


def _cast_slabs(cast_in, cast_out):
    for src, dst in zip(cast_in, cast_out):
        dst[0] = src[0, 0].astype(dst.dtype)


def _cast_results(outs, casts):
    return [o.reshape(w.shape[1], w.shape[2]) for o, (w, _) in zip(outs, casts)]


def _ada_kernel(c_ref, wa_ref, wb_ref, b_ref, o_ref, s_scr):
    k = pl.program_id(1)
    rows, n_out = wa_ref.shape[1], wa_ref.shape[2]

    @pl.when((pl.program_id(0) == 0) & (k == 0))
    def _():
        s_scr[...] = _silu(c_ref[...])

    @pl.when(k == 0)
    def _():
        o_ref[0] = jnp.broadcast_to(b_ref[0], o_ref.shape[1:])

    for half, w_ref in enumerate((wa_ref, wb_ref)):
        r0 = pl.multiple_of((2 * k + half) * rows, rows)
        for r in range(c_ref.shape[0]):
            s = s_scr[r, pl.ds(r0, rows), :]
            parts = [
                jnp.sum(w_ref[0, :, j * LANES:(j + 1) * LANES] * s, axis=0, keepdims=True)
                for j in range(n_out // LANES)
            ]
            o_ref[0, r:r + 1, :] += jnp.concatenate(parts, axis=1)


def _ada(cvecs, ada_w, ada_b):
    depth, d_in, n_out = ada_w.shape
    cb = jnp.broadcast_to(cvecs[:, :, None], cvecs.shape + (LANES,))
    return pl.pallas_call(
        _ada_kernel,
        grid=(depth, d_in // ADA_ROWS),
        in_specs=[
            pl.BlockSpec(cb.shape, lambda l, k: (0, 0, 0)),
            pl.BlockSpec((1, ADA_ROWS // 2, n_out), lambda l, k: (l, 2 * k, 0)),
            pl.BlockSpec((1, ADA_ROWS // 2, n_out), lambda l, k: (l, 2 * k + 1, 0)),
            pl.BlockSpec((1, 1, n_out), lambda l, k: (l, 0, 0)),
        ],
        out_specs=pl.BlockSpec((1, 8, n_out), lambda l, k: (l, 0, 0)),
        out_shape=jax.ShapeDtypeStruct((depth, 8, n_out), F32),
        scratch_shapes=[pltpu.VMEM(cb.shape, F32)],
        compiler_params=_params(2),
        name="ada",
    )(cb, ada_w, ada_w, ada_b.reshape(depth, 1, n_out))


def _inproj0_kernel(x_ref, mod_ref, nw_ref, w_ref, lbp_ref, *refs, n_casts):
    cast_in, (q_ref, ff_ref, fb_ref, v_ref, g_ref) = refs[:n_casts], refs[n_casts:n_casts + 5]
    _cast_slabs(cast_in, refs[n_casts + 5:])
    x = _rows(x_ref)
    tb, _, tl, _ = q_ref.shape
    half = x.shape[0] // 2
    hs = [_modulate(x[s * half:(s + 1) * half], nw_ref[...], mod_ref[0, 0:1, :],
                    mod_ref[0, 1:2, :]).astype(BF16) for s in range(2)]

    n_lb = lbp_ref.shape[0] // 2

    def lower_bound(d):
        rows = [lbp_ref[d * n_lb + i:d * n_lb + i + 1, :] for i in range(n_lb)]
        top = functools.reduce(jnp.maximum, rows)
        e = [jnp.exp(r - top) for r in rows]
        return e[0] / functools.reduce(jnp.add, e)

    lb = [lower_bound(0), lower_bound(1)]

    def emit(j, ref, act):
        w = w_ref[:, j * D_MODEL:(j + 1) * D_MODEL].astype(BF16)
        vals = [_dot(h, w) for h in hs]
        for s, val in enumerate(vals):
            val = act(val)
            b, r0 = divmod(s * half, tl)
            for hh in range(N_HEADS):
                ref[b, hh, r0:r0 + half, :] = val[:, hh * D_HEAD:(hh + 1) * D_HEAD].astype(ref.dtype)

    emit(0, q_ref, _silu)
    emit(1, ff_ref, lambda p: lb[0] + (1.0 - lb[0]) * _sigmoid(p))
    emit(2, fb_ref, lambda p: lb[1] + (1.0 - lb[1]) * _sigmoid(p))
    emit(3, v_ref, lambda p: p)
    emit(4, g_ref, _silu)


def _inproj0(x, mod, nw, w_in, lbp, per_seq_mod, casts):
    bsz, n, _ = x.shape
    tb, tl = _tile(n)
    n_i = n // tl
    mod_map = (lambda b, i: (b, 0, 0)) if per_seq_mod else (lambda b, i: (0, 0, 0))
    head_spec = pl.BlockSpec((tb, N_HEADS, tl, D_HEAD), lambda b, i: (b, 0, i, 0))
    shp = (bsz, N_HEADS, n, D_HEAD)
    c_in, c_args, c_out, c_shape = _cast_riders(casts, (bsz // tb) * n_i, lambda b, i: b * n_i + i)
    outs = pl.pallas_call(
        functools.partial(_inproj0_kernel, n_casts=len(casts)),
        grid=(bsz // tb, n_i),
        in_specs=[
            pl.BlockSpec((tb, tl, D_MODEL), lambda b, i: (b, i, 0)),
            pl.BlockSpec((1, N_MOD, D_MODEL), mod_map),
            _const_spec((1, D_MODEL)),
            _const_spec(w_in.shape),
            _const_spec(lbp.shape),
        ] + c_in,
        out_specs=[head_spec] * 5 + c_out,
        out_shape=[
            jax.ShapeDtypeStruct(shp, BF16),
            jax.ShapeDtypeStruct(shp, F32),
            jax.ShapeDtypeStruct(shp, F32),
            jax.ShapeDtypeStruct(shp, BF16),
            jax.ShapeDtypeStruct(shp, BF16),
        ] + c_shape,
        compiler_params=_params(2),
        name="inproj0",
    )(x, mod, nw, w_in, lbp, *c_args)
    return outs[:5], _cast_results(outs[5:], casts)


def _cumsum_rows(tri, x):
    hi = x.astype(BF16)
    lo = (x - hi.astype(F32)).astype(BF16)
    s = _dot(tri, jnp.concatenate([hi, lo], axis=1))
    return s[:, :D_HEAD] + s[:, D_HEAD:]


def _gla_kernel(*refs, seq_len, heads, zero_init, emit_state, n_casts):
    q_ref, ff_ref, fb_ref, v_ref, g_ref = refs[:5]
    pos = 5
    s0_ref = None
    if not zero_init:
        s0_ref = refs[pos]
        pos += 1
    gnw_ref = refs[pos]
    cast_in = refs[pos + 1:pos + 1 + n_casts]
    pos += 1 + n_casts
    o_ref = refs[pos]
    pos += 1
    sf_ref = None
    if emit_state:
        sf_ref = refs[pos]
        pos += 1
    cast_out = refs[pos:pos + n_casts]

    _cast_slabs(cast_in, cast_out)

    c = SCAN_CHUNK
    n_chunks = seq_len // c
    row = lax.broadcasted_iota(jnp.int32, (c, c), 0)
    col = lax.broadcasted_iota(jnp.int32, (c, c), 1)
    tri_lo = (col <= row).astype(BF16)
    tri_up = (col >= row).astype(BF16)
    row2 = lax.broadcasted_iota(jnp.int32, (c, 2 * c), 0)
    col2 = lax.broadcasted_iota(jnp.int32, (c, 2 * c), 1)
    mask2 = ((col2 < c) & (col2 <= row2)) | ((col2 >= c) & (col2 - c >= row2))
    zeros = jnp.zeros((c, D_HEAD), BF16)
    gnw = gnw_ref[...]
    directions = ((tri_lo, c // 2 - 1, c - 1), (tri_up, c // 2, 0))

    lanes = [dict(h=h, r0=n * c) for h in range(heads) for n in range(n_chunks)]
    states = []
    for h in range(heads):
        for d in range(2):
            if zero_init:
                states.append(jnp.zeros((D_HEAD, D_HEAD), F32))
            else:
                states.append(s0_ref[0, 0, d, h].T)

    def stage_a(ln):
        rows = pl.ds(ln["r0"], c)
        ln["f"] = [ff_ref[0, ln["h"], rows, :], fb_ref[0, ln["h"], rows, :]]
        ln["cum"] = [_cumsum_rows(tri, jnp.log2(f))
                     for (tri, _, _), f in zip(directions, ln["f"])]

    def stage_b(ln):
        rows = pl.ds(ln["r0"], c)
        q = q_ref[0, ln["h"], rows, :].astype(F32)
        ln["v"] = v_ref[0, ln["h"], rows, :]
        q_mid, k_mid, q_dec, k_state, ln["decay"] = [], [], [], [], []
        for d, (_, mid, last) in enumerate(directions):
            cum = ln["cum"][d]
            cum_mid = cum[mid:mid + 1]
            cum_last = cum[last:last + 1]
            qm = q * jnp.exp2(cum - cum_mid)
            km = (1.0 - ln["f"][d]) * jnp.exp2(cum_mid - cum)
            q_mid.append(qm.astype(BF16))
            k_mid.append(km.astype(BF16))
            q_dec.append((qm * jnp.exp2(cum_mid)).astype(BF16))
            k_state.append((km * jnp.exp2(cum_last - cum_mid)).astype(BF16))
            ln["decay"].append(jnp.exp2(cum_last))
        k_pair = jnp.concatenate([jnp.concatenate([k_mid[0], zeros], axis=1),
                                  jnp.concatenate([zeros, k_mid[1]], axis=1)], axis=0)
        ln["scores"] = _dot_nt(jnp.concatenate(q_mid, axis=1), k_pair)
        ln["kv"] = _dot_tn(ln["v"], jnp.concatenate(k_state, axis=1))
        ln["q_dec"] = jnp.concatenate(q_dec, axis=1)

    def stage_c(h):
        mine = [ln for ln in lanes if ln["h"] == h]
        for d, order in ((0, mine), (1, mine[::-1])):
            st = states[2 * h + d]
            for ln in order:
                ln.setdefault("st", [None, None])[d] = st.astype(BF16).T
                st = st * ln["decay"][d] + ln["kv"][:, d * D_HEAD:(d + 1) * D_HEAD]
            states[2 * h + d] = st

    def stage_d(ln):
        rows = pl.ds(ln["r0"], c)
        p = jnp.where(mask2, ln["scores"], 0.0).astype(BF16)
        lhs = jnp.concatenate([ln["q_dec"], p], axis=1)
        rhs = jnp.concatenate([ln["st"][0], ln["st"][1], ln["v"], ln["v"]], axis=0)
        tot = _dot(lhs, rhs)
        gate = g_ref[0, ln["h"], rows, :].astype(F32)
        o_ref[0, ln["h"], rows, :] = (_rms(tot, gnw) * gate).astype(o_ref.dtype)

    for g0 in range(0, len(lanes), SCAN_GROUP):
        for ln in lanes[g0:g0 + SCAN_GROUP]:
            stage_a(ln)
        for ln in lanes[g0:g0 + SCAN_GROUP]:
            stage_b(ln)
    for h in range(heads):
        stage_c(h)
    for ln in lanes:
        stage_d(ln)
    if emit_state:
        for h in range(heads):
            sf_ref[0, 0, 0, h] = states[2 * h].T
            sf_ref[0, 0, 1, h] = states[2 * h + 1].T


def _gla(q, ff, fb, v, g, s0, gnw, emit_state, heads, casts):
    bsz, _, n, _ = q.shape
    n_hb = N_HEADS // heads
    steps = bsz * n_hb
    zero_init = s0 is None
    seq_spec = pl.BlockSpec((1, heads, n, D_HEAD), lambda b, h: (b, h, 0, 0))
    state_spec = pl.BlockSpec((1, 1, 2, heads, D_HEAD, D_HEAD), lambda b, h: (b, 0, 0, h, 0, 0))
    in_specs = [seq_spec] * 5
    args = [q, ff, fb, v, g]
    if not zero_init:
        in_specs.append(state_spec)
        args.append(s0)
    in_specs.append(_const_spec((1, D_HEAD)))
    args.append(gnw)
    out_specs = [seq_spec]
    out_shape = [jax.ShapeDtypeStruct(q.shape, BF16)]
    if emit_state:
        out_specs.append(state_spec)
        out_shape.append(jax.ShapeDtypeStruct((bsz, 1, 2, N_HEADS, D_HEAD, D_HEAD), F32))
    c_in, c_args, c_out, c_shape = _cast_riders(casts, steps, lambda b, h: b * n_hb + h)
    in_specs += c_in
    args += c_args
    out_specs += c_out
    out_shape += c_shape
    outs = pl.pallas_call(
        functools.partial(_gla_kernel, seq_len=n, heads=heads, zero_init=zero_init,
                          emit_state=emit_state, n_casts=len(casts)),
        grid=(bsz, n_hb),
        in_specs=in_specs,
        out_specs=out_specs,
        out_shape=out_shape,
        compiler_params=_params(2),
        name="gla",
    )(*args)
    n_main = 2 if emit_state else 1
    return outs[0], (outs[1] if emit_state else None), _cast_results(outs[n_main:], casts)


def _conv_ffn(x1s, mod_ref, nfw_ref, wup_ref, cw_ref, cb_ref, wdn_ref, hid_scr, period):
    t = x1s[0].shape[0]
    hs = [_modulate(x1, nfw_ref[...], mod_ref[0, 3:4, :], mod_ref[0, 4:5, :]).astype(BF16)
          for x1 in x1s]
    pos = lax.broadcasted_iota(jnp.int32, (t, 1), 0) % period
    first = pos == 0
    last = pos == period - 1
    bounds = list(range(0, D_FF, FF_CHUNK)) + [D_FF]
    for lo, hi in zip(bounds[:-1], bounds[1:]):
        for s, h in enumerate(hs):
            a = _dot(h, wup_ref[:, lo:hi])
            g = _dot(h, wup_ref[:, D_FF + lo:D_FF + hi])
            prev = jnp.where(first, 0.0, pltpu.roll(a, 1, 0))
            nxt = jnp.where(last, 0.0, pltpu.roll(a, t - 1, 0))
            ac = (prev * cw_ref[0:1, lo:hi] + a * cw_ref[1:2, lo:hi]
                  + nxt * cw_ref[2:3, lo:hi] + cb_ref[:, lo:hi])
            hid_scr[s * t:(s + 1) * t, lo:hi] = (_silu(ac) * g).astype(BF16)
    return [x1 + mod_ref[0, 5:6, :] * _dot(hid_scr[s * t:(s + 1) * t, :], wdn_ref[...])
            for s, x1 in enumerate(x1s)]


def _tile(n):
    return (1, TOKEN_TILE) if n >= TOKEN_TILE else (TOKEN_TILE // n, n)


def _rows(ref):
    v = ref[...]
    return v.reshape(v.shape[0] * v.shape[1], v.shape[2])


def _halves(ref):
    tb, tl, _ = ref.shape
    half = tb * tl // 2
    out = []
    for s in range(2):
        bi, r0 = divmod(s * half, tl)
        out.append((bi, r0, ref[bi, r0:r0 + half, :]))
    return out


def _ffn_specs():
    return [
        _const_spec((1, D_MODEL)),
        _const_spec((D_MODEL, 2 * D_FF)),
        _const_spec((3, D_FF)),
        _const_spec((1, D_FF)),
        _const_spec((D_FF, D_MODEL)),
    ]


def _post0_kernel(x_ref, o_ref, mod_ref, wo_ref, nfw_ref, wup_ref, cw_ref, cb_ref, wdn_ref,
                  mod1_ref, nw1_ref, w1_ref, out_ref, gb_ref, cu_ref, hid_scr, *, period):
    x1s = []
    for bi, r0, x in _halves(x_ref):
        half = x.shape[0]
        o = jnp.concatenate([o_ref[bi, hh, r0:r0 + half, :] for hh in range(N_HEADS)], axis=1)
        x1s.append(x + mod_ref[0, 2:3, :] * _dot(o, wo_ref[...]))
    x2s = _conv_ffn(x1s, mod_ref, nfw_ref, wup_ref, cw_ref, cb_ref, wdn_ref, hid_scr, period)
    hs = [_modulate(x2, nw1_ref[...], mod1_ref[0, 0:1, :], mod1_ref[0, 1:2, :]).astype(BF16)
          for x2 in x2s]
    where = [(bi, r0) for bi, r0, _ in _halves(x_ref)]
    for (bi, r0), x2, h in zip(where, x2s, hs):
        half = x2.shape[0]
        out_ref[bi, r0:r0 + half, :] = x2
        gb_ref[bi, r0:r0 + half, :] = _dot(h, w1_ref[:, 0:D_MODEL]).astype(gb_ref.dtype)
        gate_c = _dot(h, w1_ref[:, D_MODEL:2 * D_MODEL])
        u = _dot(h, w1_ref[:, 2 * D_MODEL:3 * D_MODEL])
        cu_ref[bi, r0:r0 + half, :] = (gate_c * u).astype(cu_ref.dtype)


def _post0(x, o, mod, wo, nfw, wup, cw, cb, wdn, mod1, nw1, w1, per_seq_mod, period):
    bsz, n, _ = x.shape
    tb, tl = _tile(n)
    mod_map = (lambda b, i: (b, 0, 0)) if per_seq_mod else (lambda b, i: (0, 0, 0))
    tok_spec = pl.BlockSpec((tb, tl, D_MODEL), lambda b, i: (b, i, 0))
    return pl.pallas_call(
        functools.partial(_post0_kernel, period=period),
        grid=(bsz // tb, n // tl),
        in_specs=[
            tok_spec,
            pl.BlockSpec((tb, N_HEADS, tl, D_HEAD), lambda b, i: (b, 0, i, 0)),
            pl.BlockSpec((1, N_MOD, D_MODEL), mod_map),
            _const_spec((D_MODEL, D_MODEL)),
        ] + _ffn_specs() + [
            pl.BlockSpec((1, N_MOD, D_MODEL), mod_map),
            _const_spec((1, D_MODEL)),
            _const_spec(w1.shape),
        ],
        out_specs=[tok_spec] * 3,
        out_shape=[jax.ShapeDtypeStruct(x.shape, F32), jax.ShapeDtypeStruct(x.shape, BF16),
                   jax.ShapeDtypeStruct(x.shape, BF16)],
        scratch_shapes=[pltpu.VMEM((tb * tl, D_FF), BF16)],
        compiler_params=_params(2),
        name="post0",
    )(x, o, mod, wo, nfw, wup, cw, cb, wdn, mod1, nw1, w1)


def _post1_kernel(*refs, grid_rows, period):
    if grid_rows:
        (x_ref, gb_ref, cu_ref, cup_ref, cun_ref, mod_ref, scw_ref, scb_ref, wo_ref,
         nfw_ref, wup_ref, cw_ref, cb_ref, wdn_ref, fnw_ref, out_ref, hid_scr) = refs
    else:
        (x_ref, gb_ref, cu_ref, mod_ref, scw_ref, scb_ref, wo_ref,
         nfw_ref, wup_ref, cw_ref, cb_ref, wdn_ref, fnw_ref, out_ref, hid_scr) = refs
    cu = _rows(cu_ref).astype(F32)
    t = cu.shape[0]
    if grid_rows:
        i = pl.program_id(1)
        halo_p = jnp.where(i == 0, 0.0, cup_ref[0].astype(F32))
        halo_n = jnp.where(i == pl.num_programs(1) - 1, 0.0, cun_ref[0].astype(F32))
        prev = jnp.concatenate([halo_p, cu[:t - GRID_W]], axis=0)
        nxt = jnp.concatenate([cu[GRID_W:], halo_n], axis=0)
    else:
        pos = lax.broadcasted_iota(jnp.int32, (t, 1), 0) % period
        prev = jnp.where(pos == 0, 0.0, pltpu.roll(cu, 1, 0))
        nxt = jnp.where(pos == period - 1, 0.0, pltpu.roll(cu, t - 1, 0))
    z = prev * scw_ref[0:1, :] + cu * scw_ref[1:2, :] + nxt * scw_ref[2:3, :] + scb_ref[...]
    m = (_rows(gb_ref).astype(F32) * z).astype(BF16)
    halves = _halves(x_ref)
    half = t // 2
    x1s = [x + mod_ref[0, 2:3, :] * _dot(m[s * half:(s + 1) * half], wo_ref[...])
           for s, (_, _, x) in enumerate(halves)]
    x2s = _conv_ffn(x1s, mod_ref, nfw_ref, wup_ref, cw_ref, cb_ref, wdn_ref, hid_scr, period)
    for (bi, r0, _), x2 in zip(halves, x2s):
        out_ref[bi, r0:r0 + half, :] = _rms(x2, fnw_ref[...])


def _post1(x, gb, cu, mod, scw, scb, wo, nfw, wup, cw, cb, wdn, fnw, per_seq_mod, grid_rows, period):
    bsz, n, _ = x.shape
    tb, tl = _tile(n)
    mod_map = (lambda b, i: (b, 0, 0)) if per_seq_mod else (lambda b, i: (0, 0, 0))
    tok_spec = pl.BlockSpec((tb, tl, D_MODEL), lambda b, i: (b, i, 0))
    in_specs = [tok_spec, tok_spec, tok_spec]
    args = [x, gb, cu]
    if grid_rows:
        r = tl // GRID_W
        n_rows = n // GRID_W
        in_specs += [
            pl.BlockSpec((1, GRID_W, D_MODEL), lambda b, i: (b, jnp.maximum(i * r - 1, 0), 0)),
            pl.BlockSpec((1, GRID_W, D_MODEL), lambda b, i: (b, jnp.minimum((i + 1) * r, n_rows - 1), 0)),
        ]
        args += [cu, cu]
    in_specs += [
        pl.BlockSpec((1, N_MOD, D_MODEL), mod_map),
        _const_spec((3, D_MODEL)),
        _const_spec((1, D_MODEL)),
        _const_spec((D_MODEL, D_MODEL)),
    ] + _ffn_specs() + [_const_spec((1, D_MODEL))]
    args += [mod, scw, scb, wo, nfw, wup, cw, cb, wdn, fnw]
    return pl.pallas_call(
        functools.partial(_post1_kernel, grid_rows=grid_rows, period=period),
        grid=(bsz // tb, n // tl),
        in_specs=in_specs,
        out_specs=tok_spec,
        out_shape=jax.ShapeDtypeStruct(x.shape, F32),
        scratch_shapes=[pltpu.VMEM((tb * tl, D_FF), BF16)],
        compiler_params=_params(2),
        name="post1",
    )(*args)


def kernel(x_prompt, x_sample, state_hgrn, c, c_ctx, ada_w, ada_b, norm_mix_w, norm_ffn_w,
           hgrn_w_in, hgrn_lower_bounds, hgrn_gnorm_w, hgrn_w_out,
           sconv_w_in, sconv_conv_w, sconv_conv_b, sconv_w_out,
           ffn_w_up, ffn_conv_w, ffn_conv_b, ffn_w_down, final_norm_w):
    seq_p = x_prompt.shape[1]

    cvecs = jnp.concatenate([c_ctx[None, :], c], axis=0)
    mod = _ada(cvecs, ada_w, ada_b)
    mod = mod.reshape(mod.shape[0], 8, N_MOD, D_MODEL)
    mod_p = mod[:, 0:1]
    mod_s = mod[:, 1:1 + c.shape[0]]

    row = lambda a: a.reshape(1, -1)
    gnw = row(hgrn_gnorm_w[0])
    lbp = hgrn_lower_bounds.reshape(-1, hgrn_lower_bounds.shape[-1])

    def mixer0(x, mods, w_in0, s0, per_seq_mod, emit_state, proj_casts, casts):
        (q, ff, fb, v, g), proj_cast = _inproj0(x, mods[0], row(norm_mix_w[0]), w_in0, lbp,
                                                per_seq_mod, proj_casts)
        heads = min(N_HEADS, max(1, SCAN_LANES // (x.shape[1] // SCAN_CHUNK)))
        o, s_fin, scan_cast = _gla(q, ff, fb, v, g, s0, gnw, emit_state, heads, casts)
        return o, s_fin, proj_cast + scan_cast

    def post0(x, o, mods, per_seq_mod, ffn_period):
        return _post0(x, o, mods[0], w_out0, row(norm_ffn_w[0]), w_up0, ffn_conv_w[0],
                      row(ffn_conv_b[0]), w_dn0, mods[1], row(norm_mix_w[1]), w_in1,
                      per_seq_mod, ffn_period)

    def post1(x, gb, cu, mods, per_seq_mod, grid_rows, ffn_period):
        return _post1(x, gb, cu, mods[1], sconv_conv_w[0], row(sconv_conv_b[0]), w_out1,
                      row(norm_ffn_w[1]), w_up1, ffn_conv_w[1], row(ffn_conv_b[1]), w_dn1,
                      row(final_norm_w), per_seq_mod, grid_rows, ffn_period)

    o_p, new_state, (w_in0, w_up0, w_out0, w_dn0, w_in1) = mixer0(
        x_prompt, mod_p, hgrn_w_in[0], None, False, True,
        [(hgrn_w_in, 0), (ffn_w_up, 0)], [(hgrn_w_out, 0), (ffn_w_down, 0), (sconv_w_in, 0)])
    xp, gb_p, cu_p = post0(x_prompt, o_p, mod_p, False, seq_p)
    o_s, _, (w_out1, w_up1, w_dn1) = mixer0(
        x_sample, mod_s, w_in0, state_hgrn, True, False,
        [], [(sconv_w_out, 0), (ffn_w_up, 1), (ffn_w_down, 1)])
    y_prompt = post1(xp, gb_p, cu_p, mod_p, False, False, seq_p)
    xs, gb_s, cu_s = post0(x_sample, o_s, mod_s, True, GRID_W)
    y_sample = post1(xs, gb_s, cu_s, mod_s, True, True, GRID_W)
    return (y_prompt, y_sample, new_state.astype(state_hgrn.dtype))
```
